```python
import math
import jax
import jax.numpy as jnp
from jax import lax
import numpy as np

D_MODEL = 1024
BATCH = 8
SEQ = 2048
DEPTH = 4
DEC_BATCH = 128
DEC_SEQ = 8
PAST_LEN = 16384
PAGE_SIZE = 128

HEAD_DIM = 64
W_A = D_MODEL // 4
H_A = W_A // HEAD_DIM
HD_A = HEAD_DIM
SGU_CHUNK = 128
W_B = D_MODEL // 4
H_B = W_B // HEAD_DIM
HD_B = HEAD_DIM
GDN_CONV = 4
GDN_CHUNK = 64
W_C = D_MODEL // 4
G_C = W_C // HEAD_DIM
CC_WIDTH = 31
W_D = D_MODEL - W_A - W_B - W_C
H_D = W_D // HEAD_DIM
HD_D = HEAD_DIM
LORA_W = 32
LORA_A = 32
LORA_G = 64

D_MIX = W_A + W_B + W_C + W_D
COLS_A = 2 * W_A
COLS_B = 3 * W_B + 2 * H_B + W_B
COLS_C = 2 * W_C
COLS_D = 3 * W_D + LORA_W + LORA_A + LORA_G
D_PROJ = COLS_A + COLS_B + COLS_C + COLS_D

N_EXPERTS = 32
TOP_K = 4
D_FF = D_MODEL
SWIGLU_ALPHA = 1.702
SWIGLU_LIMIT = 7.0
MOE_BLOCK = 128

DN_ALPHA = (2 * DEPTH) ** 0.25
DN_BETA = (8 * DEPTH) ** -0.25
LN_EPS = 1e-5
RMS_EPS = 1e-6
GN_EPS = 64e-5

kernel_name = 'hybrid_sgu_gdn_conformer_rwkv7_moe_step'


def layer_norm(x, g, b, eps=LN_EPS):
    xf = x.astype(jnp.float32)
    xc = xf - jnp.mean(xf, -1, keepdims=True)
    var = jnp.mean(xc * xc, -1, keepdims=True)
    return (xc * lax.rsqrt(var + eps) * g.astype(jnp.float32) + b.astype(jnp.float32)).astype(x.dtype)


def rms_norm(x, g, eps=RMS_EPS):
    return x * lax.rsqrt(jnp.mean(x * x, -1, keepdims=True) + eps) * g.astype(jnp.float32)


def l2norm(x, eps=1e-6):
    return x * lax.rsqrt(jnp.sum(x * x, -1, keepdims=True) + eps)


def causal_dwconv(x, buf, w):
    width = w.shape[0]
    xp = jnp.concatenate([buf.astype(x.dtype), x], axis=1)
    y = lax.conv_general_dilated(xp, w[:, None, :].astype(x.dtype), (1,), 'VALID',
                                 dimension_numbers=('NWC', 'WIO', 'NWC'),
                                 feature_group_count=x.shape[-1])
    return y, xp[:, xp.shape[1] - (width - 1):]


def sgu_mixer(p, ln_g, ln_b, w_s, b_s):
    B, T, _ = p.shape
    h = jax.nn.gelu(p, approximate=False)
    u, v = h[..., :W_A], h[..., W_A:]
    v = layer_norm(v, ln_g, ln_b)
    v_rows = v[:, ((T - 1) // SGU_CHUNK) * SGU_CHUNK:]
    n = -(-T // SGU_CHUNK)
    vc = jnp.pad(v, ((0, 0), (0, n * SGU_CHUNK - T), (0, 0))).reshape(B, n, SGU_CHUNK, H_A, HD_A)
    causal = jnp.tril(jnp.ones((SGU_CHUNK, SGU_CHUNK), bool))
    s = jnp.einsum('hij,bnjhd->bnihd', jnp.where(causal, w_s, 0), vc) + b_s.T[None, None, :, :, None]
    s = s.reshape(B, n * SGU_CHUNK, W_A)[:, :T]
    return u * s, v_rows


def gated_delta_rule(q, k, v, g, beta, S0):
    B, T, H, Dk = q.shape
    Dv = v.shape[-1]
    C = math.gcd(T, GDN_CHUNK)
    N = T // C

    def chunks(t):
        t = t.reshape((B, N, C, H) + t.shape[3:])
        return jnp.swapaxes(jnp.moveaxis(t, 3, 2), 0, 1)

    qc, kc, vc, gc, bc = chunks(q), chunks(k), chunks(v), chunks(g), chunks(beta)
    gcum = jnp.cumsum(gc, axis=-1)
    incl = jnp.tril(jnp.ones((C, C), bool))
    strict = jnp.tril(jnp.ones((C, C), bool), -1)
    decay = jnp.exp(jnp.where(incl, gcum[..., :, None] - gcum[..., None, :], -jnp.inf))
    kb = kc * bc[..., None]
    Lm = jnp.where(strict, jnp.einsum('nbhid,nbhjd->nbhij', kb, kc) * decay, 0.0)
    eye = jnp.eye(C, dtype=Lm.dtype)
    t_inv = lax.linalg.triangular_solve(Lm + eye, jnp.broadcast_to(eye, Lm.shape),
                                        left_side=True, lower=True)
    u_c = t_inv @ (vc * bc[..., None])
    w_c = t_inv @ (kb * jnp.exp(gcum)[..., None])
    a_qk = jnp.einsum('nbhid,nbhjd->nbhij', qc, kc) * decay
    q_dec = qc * jnp.exp(gcum)[..., None]
    k_dec = kc * jnp.exp(gcum[..., -1:] - gcum)[..., None]
    g_last = jnp.exp(gcum[..., -1])

    def step(S, inp):
        u_n, w_n, a_n, qd_n, kd_n, gl_n = inp
        v_new = u_n - jnp.einsum('bhck,bhkv->bhcv', w_n, S)
        o = jnp.einsum('bhck,bhkv->bhcv', qd_n, S) + jnp.einsum('bhij,bhjv->bhiv', a_n, v_new)
        S = S * gl_n[..., None, None] + jnp.einsum('bhck,bhcv->bhkv', kd_n, v_new)
        return S, o

    S, o = lax.scan(step, S0, (u_c, w_c, a_qk, q_dec, k_dec, g_last))
    o = jnp.moveaxis(jnp.swapaxes(o, 0, 1), 2, 3).reshape(B, T, H, Dv)
    return o, S


def gdn_mixer(p, conv_buf, S0, conv_w, A_log, dt_bias, norm_g):
    B, T, _ = p.shape
    qkv, new_buf = causal_dwconv(p[..., :3 * W_B], conv_buf, conv_w)
    qkv = jax.nn.silu(qkv.astype(jnp.float32))
    heads = lambda t: t.reshape(B, T, H_B, HD_B)
    q = l2norm(heads(qkv[..., :W_B])) * HD_B ** -0.5
    k = l2norm(heads(qkv[..., W_B:2 * W_B]))
    v = heads(qkv[..., 2 * W_B:])
    i0 = 3 * W_B
    beta = jax.nn.sigmoid(p[..., i0:i0 + H_B].astype(jnp.float32))
    g = -jnp.exp(A_log.astype(jnp.float32)) * jax.nn.softplus(
        p[..., i0 + H_B:i0 + 2 * H_B].astype(jnp.float32) + dt_bias.astype(jnp.float32))
    z = heads(p[..., i0 + 2 * H_B:].astype(jnp.float32))
    o, S = gated_delta_rule(q, k, v, g, beta, S0.astype(jnp.float32))
    o = rms_norm(o, norm_g) * jax.nn.silu(z)
    return o.reshape(B, T, W_B).astype(p.dtype), new_buf, S.astype(S0.dtype)


def conformer_conv_mixer(p, buf, dw_w, dw_b, ln_g, ln_b):
    B, T, _ = p.shape
    h = p[..., :W_C] * jax.nn.sigmoid(p[..., W_C:])
    h, new_buf = causal_dwconv(h, buf, dw_w)
    h = (h + dw_b).reshape(B, T, G_C, W_C // G_C)
    h = layer_norm(h, ln_g.reshape(G_C, -1), ln_b.reshape(G_C, -1))
    return jax.nn.silu(h).reshape(B, T, W_C), new_buf


def rwkv7_recurrence(r, decay, k, v, kk, a, S0):
    def step(S, inp):
        r_t, w_t, k_t, v_t, kk_t, a_t = inp
        sa = jnp.einsum('bhvk,bhk->bhv', S, -kk_t)
        S = (S * w_t[:, :, None, :] + sa[..., None] * (kk_t * a_t)[:, :, None, :]
             + v_t[..., None] * k_t[:, :, None, :])
        return S, jnp.einsum('bhvk,bhk->bhv', S, r_t)
    xs = tuple(jnp.moveaxis(t, 1, 0) for t in (r, decay, k, v, kk, a))
    S, y = lax.scan(step, S0, xs)
    return jnp.moveaxis(y, 0, 1), S


def rwkv7_mixer(p, shift_buf, S0, mu, w0, w2, a0, a2, g2, k_k, k_a, r_k, ln_g, ln_b):
    B, T, _ = p.shape
    pf = p.astype(jnp.float32)
    prev = jnp.concatenate([shift_buf[:, None].astype(jnp.float32), pf[:, :-1]], axis=1)
    pl = pf + (prev - pf) * mu.astype(jnp.float32)
    i1, i2, i3 = W_D, 2 * W_D, 3 * W_D
    i4, i5 = i3 + LORA_W, i3 + LORA_W + LORA_A
    r, k, v = pl[..., :i1], pl[..., i1:i2], pl[..., i2:i3]
    xw, xa, xg = pl[..., i3:i4], pl[..., i4:i5], pl[..., i5:]
    w = -jax.nn.softplus(-(w0.astype(jnp.float32) + jnp.tanh(xw) @ w2.astype(jnp.float32))) - 0.5
    decay = jnp.exp(-jnp.exp(w))
    a = jax.nn.sigmoid(a0.astype(jnp.float32) + xa @ a2.astype(jnp.float32))
    g = jax.nn.sigmoid(xg) @ g2.astype(jnp.float32)
    heads = lambda t: t.reshape(B, T, H_D, HD_D)
    kk = l2norm(heads(k * k_k.astype(jnp.float32)))
    k = k * (1.0 + (a - 1.0) * k_a.astype(jnp.float32))
    r_h, k_h, v_h = heads(r), heads(k), heads(v)
    y, S = rwkv7_recurrence(r_h, heads(decay), k_h, v_h, kk, heads(a), S0.astype(jnp.float32))
    y = layer_norm(y, ln_g.reshape(H_D, HD_D), ln_b.reshape(H_D, HD_D), GN_EPS)
    y = y + jnp.sum(r_h * k_h * r_k.astype(jnp.float32), -1, keepdims=True) * v_h
    y = y.reshape(B, T, W_D) * g
    return y.astype(p.dtype), p[:, -1], S.astype(S0.dtype)


def swiglu_clamped(h):
    h_glu = jnp.minimum(h[..., ::2], SWIGLU_LIMIT)
    h_lin = jnp.clip(h[..., 1::2], -SWIGLU_LIMIT, SWIGLU_LIMIT)
    return h_glu * jax.nn.sigmoid(SWIGLU_ALPHA * h_glu) * (h_lin + 1.0)


def moe_ffn(x, router_w, router_b, w1, b1, w2, b2):
    N, D = x.shape
    logits = x.astype(jnp.float32) @ router_w.astype(jnp.float32) + router_b.astype(jnp.float32)
    top_val, top_idx = lax.top_k(logits, TOP_K)
    gates = jax.nn.softmax(top_val, axis=-1)
    NK = N * TOP_K
    flat_e = top_idx.reshape(NK)
    order = jnp.argsort(flat_e)
    e_sorted = flat_e[order]
    tok_sorted = order // TOP_K
    gate_sorted = gates.reshape(NK)[order]
    counts = jnp.bincount(flat_e, length=N_EXPERTS)
    padded = (counts + MOE_BLOCK - 1) // MOE_BLOCK * MOE_BLOCK
    pad_end = jnp.cumsum(padded)
    pad_start = pad_end - padded
    start = jnp.cumsum(counts) - counts
    dest = pad_start[e_sorted] + jnp.arange(NK) - start[e_sorted]
    n_blocks = -(-NK // MOE_BLOCK) + N_EXPERTS
    slot_tok = jnp.full((n_blocks * MOE_BLOCK,), N, jnp.int32).at[dest].set(tok_sorted)
    block_e = jnp.minimum(jnp.searchsorted(pad_end, jnp.arange(n_blocks) * MOE_BLOCK, side='right'),
                          N_EXPERTS - 1)
    x_pad = jnp.concatenate([x, jnp.zeros((1, D), x.dtype)], axis=0)
    xb = x_pad[slot_tok].reshape(n_blocks, MOE_BLOCK, D)

    def expert_block(args):
        xblk, e = args
        h = swiglu_clamped(xblk @ w1[e] + b1[e])
        return h @ w2[e] + b2[e]

    yb = lax.map(expert_block, (xb, block_e)).reshape(n_blocks * MOE_BLOCK, D)
    y_sorted = yb[dest].astype(jnp.float32) * gate_sorted[:, None]
    return jnp.zeros((N, D), jnp.float32).at[tok_sorted].add(y_sorted).astype(x.dtype)


def trunk_layer(x, st, lp):
    gdn_buf, gdn_S, cc_buf, rw_shift, rw_S = st
    p = x @ lp['w_in']
    o1 = COLS_A
    o2 = o1 + COLS_B
    o3 = o2 + COLS_C
    y_a, v_rows = sgu_mixer(p[..., :o1], lp['sgu_ln_g'], lp['sgu_ln_b'], lp['sgu_w'], lp['sgu_b'])
    y_b, gdn_buf, gdn_S = gdn_mixer(p[..., o1:o2], gdn_buf, gdn_S, lp['gdn_conv_w'],
                                    lp['gdn_A_log'], lp['gdn_dt_bias'], lp['gdn_norm_g'])
    y_c, cc_buf = conformer_conv_mixer(p[..., o2:o3], cc_buf, lp['cc_dw_w'], lp['cc_dw_b'],
                                       lp['cc_ln_g'], lp['cc_ln_b'])
    y_d, rw_shift, rw_S = rwkv7_mixer(p[..., o3:], rw_shift, rw_S, lp['rw_mu'], lp['rw_w0'],
                                      lp['rw_w2'], lp['rw_a0'], lp['rw_a2'], lp['rw_g2'],
                                      lp['rw_k_k'], lp['rw_k_a'], lp['rw_r_k'],
                                      lp['rw_ln_g'], lp['rw_ln_b'])
    mix = jnp.concatenate([y_a, y_b, y_c, y_d], axis=-1) @ lp['w_out']
    x = layer_norm(DN_ALPHA * x + mix, lp['ln_mix_g'], lp['ln_mix_b'])
    B, T, D = x.shape
    f = moe_ffn(x.reshape(B * T, D), lp['router_w'], lp['router_b'], lp['moe_w1'],
                lp['moe_b1'], lp['moe_w2'], lp['moe_b2']).reshape(B, T, D)
    x = layer_norm(DN_ALPHA * x + f, lp['ln_ffn_g'], lp['ln_ffn_b'])
    return x, (v_rows, gdn_buf, gdn_S, cc_buf, rw_shift, rw_S)


def setup_inputs(seed: int = 0) -> dict:
    key = jax.random.key(seed)
    keys = iter(jax.random.split(key, 64))

    def nrm(shape, scale):
        return scale * jax.random.normal(next(keys), shape, jnp.float32)

    def unif(shape, lo, hi):
        return jax.random.uniform(next(keys), shape, jnp.float32, lo, hi)

    L = DEPTH
    dt0 = jnp.exp(unif((L, H_B), math.log(1e-3), math.log(1e-1)))
    return {
        'x_prompt': nrm((BATCH, SEQ, D_MODEL), 1.0),
        'x_sample': nrm((DEC_BATCH, DEC_SEQ, D_MODEL), 1.0),
        'state_gdn_conv': nrm((L, DEC_BATCH, GDN_CONV - 1, 3 * W_B), 1.0),
        'state_gdn_S': nrm((L, DEC_BATCH, H_B, HD_B, HD_B), 0.1),
        'state_cc_conv': nrm((L, DEC_BATCH, CC_WIDTH - 1, W_C), 0.5),
        'state_rwkv_shift': nrm((L, DEC_BATCH, COLS_D), 1.0),
        'state_rwkv_S': nrm((L, DEC_BATCH, H_D, HD_D, HD_D), 0.1),
        'ln_in_g': 1.0 + nrm((D_MODEL,), 0.02),
        'ln_in_b': nrm((D_MODEL,), 0.02),
        'w_in': nrm((L, D_MODEL, D_PROJ), D_MODEL ** -0.5),
        'sgu_ln_g': 1.0 + nrm((L, W_A), 0.02),
        'sgu_ln_b': nrm((L, W_A), 0.02),
        'sgu_w': nrm((L, H_A, SGU_CHUNK, SGU_CHUNK), 0.5 * SGU_CHUNK ** -0.5),
        'sgu_b': 1.0 + nrm((L, H_A, SGU_CHUNK), 0.1),
        'gdn_conv_w': nrm((L, GDN_CONV, 3 * W_B), GDN_CONV ** -0.5),
        'gdn_A_log': jnp.log(unif((L, H_B), 1.0, 16.0)),
        'gdn_dt_bias': dt0 + jnp.log(-jnp.expm1(-dt0)),
        'gdn_norm_g': 1.0 + nrm((L, HD_B), 0.02),
        'cc_dw_w': nrm((L, CC_WIDTH, W_C), CC_WIDTH ** -0.5),
        'cc_dw_b': nrm((L, W_C), 0.02),
        'cc_ln_g': 1.0 + nrm((L, W_C), 0.02),
        'cc_ln_b': nrm((L, W_C), 0.02),
        'rw_mu': unif((L, COLS_D), 0.0, 1.0),
        'rw_w0': unif((L, W_D), -6.0, -1.0),
        'rw_w2': nrm((L, LORA_W, W_D), 0.1),
        'rw_a0': nrm((L, W_D), 0.1),
        'rw_a2': nrm((L, LORA_A, W_D), 0.1),
        'rw_g2': nrm((L, LORA_G, W_D), LORA_G ** -0.5),
        'rw_k_k': 0.85 + nrm((L, W_D), 0.02),
        'rw_k_a': 1.0 + nrm((L, W_D), 0.02),
        'rw_r_k': nrm((L, H_D, HD_D), 0.1),
        'rw_ln_g': 1.0 + nrm((L, W_D), 0.02),
        'rw_ln_b': nrm((L, W_D), 0.02),
        'w_out': nrm((L, D_MIX, D_MODEL), DN_BETA * D_MIX ** -0.5),
        'ln_mix_g': 1.0 + nrm((L, D_MODEL), 0.02),
        'ln_mix_b': nrm((L, D_MODEL), 0.02),
        'router_w': nrm((L, D_MODEL, N_EXPERTS), D_MODEL ** -0.5),
        'router_b': nrm((L, N_EXPERTS), 0.01),
        'moe_w1': nrm((L, N_EXPERTS, D_MODEL, 2 * D_FF), D_MODEL ** -0.5),
        'moe_b1': nrm((L, N_EXPERTS, 2 * D_FF), 0.02),
        'moe_w2': nrm((L, N_EXPERTS, D_FF, D_MODEL), DN_BETA * D_FF ** -0.5),
        'moe_b2': nrm((L, N_EXPERTS, D_MODEL), 0.02),
        'ln_ffn_g': 1.0 + nrm((L, D_MODEL), 0.02),
        'ln_ffn_b': nrm((L, D_MODEL), 0.02),
    }


def reference(x_prompt, x_sample, state_gdn_conv, state_gdn_S, state_cc_conv, state_rwkv_shift,
              state_rwkv_S, ln_in_g, ln_in_b, w_in, sgu_ln_g, sgu_ln_b, sgu_w, sgu_b, gdn_conv_w,
              gdn_A_log, gdn_dt_bias, gdn_norm_g, cc_dw_w, cc_dw_b, cc_ln_g, cc_ln_b, rw_mu, rw_w0,
              rw_w2, rw_a0, rw_a2, rw_g2, rw_k_k, rw_k_a, rw_r_k, rw_ln_g, rw_ln_b, w_out, ln_mix_g,
              ln_mix_b, router_w, router_b, moe_w1, moe_b1, moe_w2, moe_b2, ln_ffn_g, ln_ffn_b):
    xp = layer_norm(x_prompt, ln_in_g, ln_in_b)
    xs = layer_norm(x_sample, ln_in_g, ln_in_b)
    bp = x_prompt.shape[0]
    dt = x_prompt.dtype
    zero_state = (jnp.zeros((bp, GDN_CONV - 1, 3 * W_B), dt),
                  jnp.zeros((bp, H_B, HD_B, HD_B), dt),
                  jnp.zeros((bp, CC_WIDTH - 1, W_C), dt),
                  jnp.zeros((bp, COLS_D), dt),
                  jnp.zeros((bp, H_D, HD_D, HD_D), dt))
    st_p = []
    st_s = []
    for l in range(DEPTH):
        lp = {'w_in': w_in[l], 'sgu_ln_g': sgu_ln_g[l], 'sgu_ln_b': sgu_ln_b[l],
              'sgu_w': sgu_w[l], 'sgu_b': sgu_b[l], 'gdn_conv_w': gdn_conv_w[l],
              'gdn_A_log': gdn_A_log[l], 'gdn_dt_bias': gdn_dt_bias[l],
              'gdn_norm_g': gdn_norm_g[l], 'cc_dw_w': cc_dw_w[l], 'cc_dw_b': cc_dw_b[l],
              'cc_ln_g': cc_ln_g[l], 'cc_ln_b': cc_ln_b[l], 'rw_mu': rw_mu[l], 'rw_w0': rw_w0[l],
              'rw_w2': rw_w2[l], 'rw_a0': rw_a0[l], 'rw_a2': rw_a2[l], 'rw_g2': rw_g2[l],
              'rw_k_k': rw_k_k[l], 'rw_k_a': rw_k_a[l], 'rw_r_k': rw_r_k[l],
              'rw_ln_g': rw_ln_g[l], 'rw_ln_b': rw_ln_b[l], 'w_out': w_out[l],
              'ln_mix_g': ln_mix_g[l], 'ln_mix_b': ln_mix_b[l], 'router_w': router_w[l],
              'router_b': router_b[l], 'moe_w1': moe_w1[l], 'moe_b1': moe_b1[l],
              'moe_w2': moe_w2[l], 'moe_b2': moe_b2[l], 'ln_ffn_g': ln_ffn_g[l],
              'ln_ffn_b': ln_ffn_b[l]}
        xp, sp = trunk_layer(xp, zero_state, lp)
        xs, ss = trunk_layer(xs, (state_gdn_conv[l], state_gdn_S[l], state_cc_conv[l],
                                  state_rwkv_shift[l], state_rwkv_S[l]), lp)
        st_p.append(sp)
        st_s.append(ss)
    return (xp, xs,
            jnp.stack([s[0] for s in st_p]), jnp.stack([s[0] for s in st_s]),
            jnp.stack([s[1] for s in st_p]), jnp.stack([s[1] for s in st_s]),
            jnp.stack([s[2] for s in st_p]), jnp.stack([s[2] for s in st_s]),
            jnp.stack([s[3] for s in st_p]), jnp.stack([s[3] for s in st_s]),
            jnp.stack([s[4] for s in st_p]), jnp.stack([s[4] for s in st_s]),
            jnp.stack([s[5] for s in st_p]), jnp.stack([s[5] for s in st_s]))
```

```python
import functools
import math

import jax
import jax.numpy as jnp
from jax import lax
from jax.experimental import pallas as pl
from jax.experimental.pallas import tpu as pltpu

F32 = jnp.float32
BF16 = jnp.bfloat16
HI = lax.Precision.HIGHEST

D_MODEL = 1024
HEAD_DIM = 64
W_MIX = 256
N_HEADS = W_MIX // HEAD_DIM
SGU_CHUNK = 128
GDN_CONV = 4
GDN_CHUNK = 64
CC_WIDTH = 31
RWKV_CHUNK = 32
LORA_W, LORA_A, LORA_G = 32, 32, 64
COLS_D = 3 * W_MIX + LORA_W + LORA_A + LORA_G
N_EXPERTS = 32
TOP_K = 4
D_FF = D_MODEL
SWIGLU_ALPHA = 1.702
SWIGLU_LIMIT = 7.0
LN_EPS = 1e-5
RMS_EPS = 1e-6
GN_EPS = 64e-5
LANES = 128
SUBLANES = 8
VMEM_LIMIT = 56 * 1024 * 1024
NEG_BIG = -1e30

P_D = COLS_D
P_BG = LANES
P_B = 4 * W_MIX
P_A = 2 * W_MIX
P_C = 2 * W_MIX
P_TOTAL = P_D + P_BG + P_B + P_A + P_C


def _pick(n, cands):
    for c in cands:
        if n % c == 0:
            return c
    raise ValueError(f"no tile in {cands} divides {n}")


def _params(sem):
    return pltpu.CompilerParams(dimension_semantics=sem, vmem_limit_bytes=VMEM_LIMIT)


def _layer_norm(x, g, b, eps):
    xc = x - jnp.mean(x, -1, keepdims=True)
    var = jnp.mean(xc * xc, -1, keepdims=True)
    return xc * lax.rsqrt(var + eps) * g + b


def _sigmoid(x):
    return 1.0 / (1.0 + jnp.exp(-x))


def _silu(x):
    return x * _sigmoid(x)


def _softplus(x):
    return jnp.maximum(x, 0.0) + jnp.log(1.0 + jnp.exp(-jnp.abs(x)))


def _dot(a, b):
    return jnp.dot(a, b, preferred_element_type=F32)


def _dot_hi(a, b):
    return jnp.dot(a, b, precision=HI, preferred_element_type=F32)


def _dot_nt_hi(a, b):
    return lax.dot_general(a, b, (((1,), (1,)), ((), ())), precision=HI, preferred_element_type=F32)


def _head_ones():
    r = lax.broadcasted_iota(jnp.int32, (W_MIX, W_MIX), 0) // HEAD_DIM
    c = lax.broadcasted_iota(jnp.int32, (W_MIX, W_MIX), 1) // HEAD_DIM
    return (r == c).astype(F32)


def _tri(n, strict):
    r = lax.broadcasted_iota(jnp.int32, (n, n), 0)
    c = lax.broadcasted_iota(jnp.int32, (n, n), 1)
    return (r > c) if strict else (r >= c)


def _neumann_inverse(x, n):
    eye = (lax.broadcasted_iota(jnp.int32, (n, n), 0) == lax.broadcasted_iota(jnp.int32, (n, n), 1)).astype(F32)
    acc = eye + x
    p = x
    k = 2
    while k < n:
        p = _dot_hi(p, p)
        acc = acc + _dot_hi(acc, p)
        k *= 2
    return acc


def _ln_kernel(x_ref, g_ref, b_ref, o_ref):
    o_ref[...] = _layer_norm(x_ref[...], g_ref[...], b_ref[...], LN_EPS)


def _ln_call(x, g, b):
    n = x.shape[0]
    tm = _pick(n, (1024, 512, 256, 128))
    return pl.pallas_call(
        _ln_kernel, grid=(n // tm,),
        in_specs=[pl.BlockSpec((tm, D_MODEL), lambda i: (i, 0)),
                  pl.BlockSpec((1, D_MODEL), lambda i: (0, 0)),
                  pl.BlockSpec((1, D_MODEL), lambda i: (0, 0))],
        out_specs=pl.BlockSpec((tm, D_MODEL), lambda i: (i, 0)),
        out_shape=jax.ShapeDtypeStruct((n, D_MODEL), F32),
        compiler_params=_params(("parallel",)), name="ln_in",
    )(x, g.reshape(1, -1), b.reshape(1, -1))


def _proj_kernel(x_ref, w_ref, pd_ref, pbg_ref, pb_ref, pa_ref, pc_ref):
    p = _dot(x_ref[...].astype(BF16), w_ref[...])
    o = 0
    for ref, w in ((pd_ref, P_D), (pbg_ref, P_BG), (pb_ref, P_B), (pa_ref, P_A), (pc_ref, P_C)):
        ref[...] = p[:, o:o + w]
        o += w


def _proj_call(x, w_cat):
    n = x.shape[0]
    tm = _pick(n, (512, 256, 128))
    widths = (P_D, P_BG, P_B, P_A, P_C)
    return pl.pallas_call(
        _proj_kernel, grid=(n // tm,),
        in_specs=[pl.BlockSpec((tm, D_MODEL), lambda i: (i, 0)),
                  pl.BlockSpec((D_MODEL, P_TOTAL), lambda i: (0, 0))],
        out_specs=[pl.BlockSpec((tm, w), lambda i: (i, 0)) for w in widths],
        out_shape=[jax.ShapeDtypeStruct((n, w), F32) for w in widths],
        compiler_params=_params(("parallel",)), name="proj_in",
    )(x, w_cat)


def _reorder_w_in(w_in, h_b):
    cols_a = 2 * W_MIX
    cols_b = 3 * W_MIX + 2 * h_b + W_MIX
    o1, o2 = cols_a, cols_a + cols_b
    o3 = o2 + 2 * W_MIX
    wa, wb, wc, wd = w_in[:, :o1], w_in[:, o1:o2], w_in[:, o2:o3], w_in[:, o3:]
    qkv, bg, z = wb[:, :3 * W_MIX], wb[:, 3 * W_MIX:3 * W_MIX + 2 * h_b], wb[:, 3 * W_MIX + 2 * h_b:]
    bg = jnp.pad(bg, ((0, 0), (0, P_BG - 2 * h_b)))
    return jnp.concatenate([wd, bg, qkv, z, wa, wc], axis=1).astype(BF16)


def _sgu_kernel(p_ref, w_ref, b_ref, g_ref, beta_ref, y_ref, v_ref, *, n_chunks):
    w = w_ref[0]
    bias = b_ref[0]
    lane_head = lax.broadcasted_iota(jnp.int32, (SGU_CHUNK, W_MIX), 1) // HEAD_DIM
    for c in range(n_chunks):
        rows = pl.ds(c * SGU_CHUNK, SGU_CHUNK)
        x = p_ref[rows, :]
        h = 0.5 * x * (1.0 + lax.erf(x * (1.0 / math.sqrt(2.0))))
        u = h[:, :W_MIX]
        v = _layer_norm(h[:, W_MIX:], g_ref[...], beta_ref[...], LN_EPS)
        v_ref[rows, :] = v
        vb = jnp.concatenate([jnp.where(lane_head == hh, v, 0.0) for hh in range(N_HEADS)], axis=0)
        s = _dot(w, vb.astype(BF16)) + bias
        y_ref[rows, :] = u * s


def _sgu_weights(sgu_w, sgu_b, t_s):
    causal = jnp.tril(jnp.ones((SGU_CHUNK, SGU_CHUNK), bool))
    wp = jnp.where(causal, sgu_w, 0.0)
    reps = SGU_CHUNK // t_s
    ws = jnp.stack([jnp.kron(jnp.eye(reps, dtype=F32), wp[h, :t_s, :t_s]) for h in range(N_HEADS)])
    cat = lambda w: jnp.concatenate([w[h] for h in range(N_HEADS)], axis=1)
    w_eff = jnp.stack([cat(wp), cat(ws)]).astype(BF16)
    bp = jnp.repeat(sgu_b.T, HEAD_DIM, axis=1)
    bs = jnp.tile(bp[:t_s], (reps, 1))
    return w_eff, jnp.stack([bp, bs])


def _sgu_call(p_a, w_eff, b_eff, ln_g, ln_b, n_prompt_rows):
    n = p_a.shape[0]
    tb = _pick(math.gcd(n_prompt_rows, n - n_prompt_rows), (1024, 512, 256, 128))
    n_prompt_tiles = n_prompt_rows // tb
    grp = lambda i: jnp.minimum(i // n_prompt_tiles, 1)
    return pl.pallas_call(
        functools.partial(_sgu_kernel, n_chunks=tb // SGU_CHUNK), grid=(n // tb,),
        in_specs=[pl.BlockSpec((tb, P_A), lambda i: (i, 0)),
                  pl.BlockSpec((1, SGU_CHUNK, N_HEADS * SGU_CHUNK), lambda i: (grp(i), 0, 0)),
                  pl.BlockSpec((1, SGU_CHUNK, W_MIX), lambda i: (grp(i), 0, 0)),
                  pl.BlockSpec((1, W_MIX), lambda i: (0, 0)),
                  pl.BlockSpec((1, W_MIX), lambda i: (0, 0))],
        out_specs=[pl.BlockSpec((tb, W_MIX), lambda i: (i, 0))] * 2,
        out_shape=[jax.ShapeDtypeStruct((n, W_MIX), F32)] * 2,
        compiler_params=_params(("parallel",)), name="sgu",
    )(p_a, w_eff, b_eff, ln_g.reshape(1, -1), ln_b.reshape(1, -1))


CC_HDR = 32


def _cc_kernel(p_ref, buf_ref, w_ref, wb_ref, g_ref, b_ref, y_ref, nb_ref, xp_ref, *, sb, t, tt):
    hist = CC_WIDTH - 1
    ones = _head_ones()
    w = w_ref[...]
    for s in range(sb):
        x = p_ref[pl.ds(s * t, t), :]
        xp_ref[pl.ds(CC_HDR - hist, hist), :] = buf_ref[s]
        xp_ref[pl.ds(CC_HDR, t), :] = x[:, :W_MIX] * _sigmoid(x[:, W_MIX:])
        nb_ref[s] = xp_ref[pl.ds(t + CC_HDR - hist, hist), :]

        def tile(i, carry):
            base = pl.multiple_of(i * tt, SUBLANES)
            win = xp_ref[pl.ds(base, tt + CC_HDR), :]
            acc = jnp.zeros((tt, W_MIX), F32)
            for j in range(CC_WIDTH):
                o = j + CC_HDR - hist
                acc = acc + win[o:o + tt] * w[j:j + 1]
            hh = acc + wb_ref[...]
            mean = _dot_hi(hh, ones) * (1.0 / HEAD_DIM)
            xc = hh - mean
            var = _dot_hi(xc * xc, ones) * (1.0 / HEAD_DIM)
            yy = xc * lax.rsqrt(var + LN_EPS) * g_ref[...] + b_ref[...]
            y_ref[pl.ds(pl.multiple_of(s * t + base, SUBLANES), tt), :] = _silu(yy)
            return carry

        lax.fori_loop(0, t // tt, tile, 0)


def _cc_call(p_c, buf, w, wb, g, b, row0, t):
    nseq = buf.shape[0]
    sb = 1 if t >= 256 else _pick(nseq, (16, 8, 4, 2, 1))
    tt = min(t, 256)
    rows = sb * t
    blk0 = row0 // rows
    assert row0 % rows == 0 and t % tt == 0
    kern = functools.partial(_cc_kernel, sb=sb, t=t, tt=tt)
    in_specs = [pl.BlockSpec((rows, P_C), lambda i: (blk0 + i, 0)),
                pl.BlockSpec((sb, CC_WIDTH - 1, W_MIX), lambda i: (i, 0, 0)),
                pl.BlockSpec((CC_WIDTH, W_MIX), lambda i: (0, 0)),
                pl.BlockSpec((1, W_MIX), lambda i: (0, 0)),
                pl.BlockSpec((1, W_MIX), lambda i: (0, 0)),
                pl.BlockSpec((1, W_MIX), lambda i: (0, 0))]
    args = [p_c, buf, w, wb.reshape(1, -1), g.reshape(1, -1), b.reshape(1, -1)]
    return pl.pallas_call(
        kern, grid=(nseq // sb,), in_specs=in_specs,
        out_specs=[pl.BlockSpec((rows, W_MIX), lambda i: (i, 0)),
                   pl.BlockSpec((sb, CC_WIDTH - 1, W_MIX), lambda i: (i, 0, 0))],
        out_shape=[jax.ShapeDtypeStruct((nseq * t, W_MIX), F32),
                   jax.ShapeDtypeStruct((nseq, CC_WIDTH - 1, W_MIX), F32)],
        scratch_shapes=[pltpu.VMEM((t + CC_HDR, W_MIX), F32)],
        compiler_params=_params(("arbitrary",)), name="cc",
    )(*args)


GDN_HDR = 8
GDN_QKV = 3 * W_MIX


def _dot_tn_hi(a, b):
    return lax.dot_general(a, b, (((0,), (0,)), ((), ())), precision=HI, preferred_element_type=F32)


def _lane_expand(src_lane0):
    r = lax.broadcasted_iota(jnp.int32, (LANES, W_MIX), 0)
    c = lax.broadcasted_iota(jnp.int32, (LANES, W_MIX), 1) // HEAD_DIM
    return (r == c + src_lane0).astype(F32)


def _gdn_kernel(p_ref, bg_ref, buf_ref, s0_ref, cw_ref, alog_ref, dt_ref, ng_ref,
                y_ref, nb_ref, s_ref, xp_ref, *, sb, t, c):
    hist = GDN_CONV - 1
    ones = _head_ones()
    e_beta = _lane_expand(0)
    e_g = _lane_expand(N_HEADS)
    incl = _tri(c, False)
    strict = _tri(c, True)
    eye = (lax.broadcasted_iota(jnp.int32, (c, c), 0) == lax.broadcasted_iota(jnp.int32, (c, c), 1)).astype(F32)
    tri_ones = incl.astype(F32)
    all_ones = jnp.ones((c, c), F32)
    cw = cw_ref[...]
    neg_a = -jnp.exp(alog_ref[...])

    def seq(s, carry0):
        row0 = pl.multiple_of(s * t, SUBLANES)
        xp_ref[pl.ds(GDN_HDR - hist, hist), :] = buf_ref[s]
        xp_ref[pl.ds(GDN_HDR, t), :] = p_ref[pl.ds(row0, t), :GDN_QKV]
        nb_ref[s] = xp_ref[pl.ds(t + GDN_HDR - hist, hist), :]

        def chunk(n, states):
            base = pl.multiple_of(n * c, SUBLANES)
            rows = pl.ds(pl.multiple_of(row0 + base, SUBLANES), c)
            win = xp_ref[pl.ds(base, c + GDN_HDR), :]
            conv = jnp.zeros((c, GDN_QKV), F32)
            for j in range(GDN_CONV):
                o = j + GDN_HDR - hist
                conv = conv + win[o:o + c] * cw[j:j + 1]
            act = _silu(conv)
            q, k, v = act[:, :W_MIX], act[:, W_MIX:2 * W_MIX], act[:, 2 * W_MIX:]
            q = q * lax.rsqrt(_dot_hi(q * q, ones) + 1e-6) * (HEAD_DIM ** -0.5)
            k = k * lax.rsqrt(_dot_hi(k * k, ones) + 1e-6)
            bg = bg_ref[rows, :]
            beta = _dot_hi(_sigmoid(bg), e_beta)
            g = neg_a * _softplus(bg + dt_ref[...])
            gcum = _dot_hi(_dot_hi(tri_ones, g), e_g)
            kb = k * beta
            vb = v * beta
            egc = jnp.exp(gcum)
            glast = gcum[c - 1:c, :]
            kw = kb * egc
            q_dec = q * egc
            k_dec = k * jnp.exp(glast - gcum)
            g_last = jnp.exp(glast)
            z = p_ref[rows, GDN_QKV:]
            outs = []
            new_states = []
            for h in range(N_HEADS):
                hs = slice(h * HEAD_DIM, (h + 1) * HEAD_DIM)
                gcol = gcum[:, h * HEAD_DIM:h * HEAD_DIM + c]
                grow = _dot_hi(all_ones, eye * gcol)
                decay = jnp.where(incl, jnp.exp(gcol - grow), 0.0)
                lm = jnp.where(strict, _dot_nt_hi(kb[:, hs], k[:, hs]) * decay, 0.0)
                t_inv = _neumann_inverse(-lm, c)
                u = _dot_hi(t_inv, vb[:, hs])
                w = _dot_hi(t_inv, kw[:, hs])
                a_qk = _dot_nt_hi(q[:, hs], k[:, hs]) * decay
                st = states[h]
                v_new = u - _dot_hi(w, st)
                outs.append(_dot_hi(q_dec[:, hs], st) + _dot_hi(a_qk, v_new))
                new_states.append(st * g_last[:, hs] + _dot_tn_hi(k_dec[:, hs], v_new))
            o = jnp.concatenate(outs, axis=1)
            o = o * lax.rsqrt(_dot_hi(o * o, ones) * (1.0 / HEAD_DIM) + RMS_EPS) * ng_ref[...]
            y_ref[rows, :] = o * _silu(z)
            return tuple(new_states)

        states = lax.fori_loop(0, t // c, chunk, tuple(s0_ref[s, h] for h in range(N_HEADS)))
        for h in range(N_HEADS):
            s_ref[s, h] = states[h]
        return carry0

    lax.fori_loop(0, sb, seq, 0)


def _gdn_call(p_b, p_bg, buf, s0, conv_w, a_log, dt_bias, norm_g, row0, t):
    nseq = buf.shape[0]
    h_b = a_log.shape[0]
    assert h_b == N_HEADS
    c = math.gcd(t, GDN_CHUNK)
    sb = 1 if t >= 256 else _pick(nseq, (16, 8, 4, 2, 1))
    rows = sb * t
    blk0 = row0 // rows
    assert row0 % rows == 0 and c % SUBLANES == 0
    lane_pad = lambda x: jnp.pad(x.reshape(1, -1), ((0, 0), (h_b, LANES - 2 * h_b)))
    kern = functools.partial(_gdn_kernel, sb=sb, t=t, c=c)
    in_specs = [pl.BlockSpec((rows, P_B), lambda i: (blk0 + i, 0)),
                pl.BlockSpec((rows, P_BG), lambda i: (blk0 + i, 0)),
                pl.BlockSpec((sb, GDN_CONV - 1, GDN_QKV), lambda i: (i, 0, 0)),
                pl.BlockSpec((sb, N_HEADS, HEAD_DIM, HEAD_DIM), lambda i: (i, 0, 0, 0)),
                pl.BlockSpec((GDN_CONV, GDN_QKV), lambda i: (0, 0)),
                pl.BlockSpec((1, LANES), lambda i: (0, 0)),
                pl.BlockSpec((1, LANES), lambda i: (0, 0)),
                pl.BlockSpec((1, W_MIX), lambda i: (0, 0))]
    args = [p_b, p_bg, buf, s0, conv_w, lane_pad(a_log), lane_pad(dt_bias), jnp.tile(norm_g, N_HEADS).reshape(1, -1)]
    return pl.pallas_call(
        kern, grid=(nseq // sb,), in_specs=in_specs,
        out_specs=[pl.BlockSpec((rows, W_MIX), lambda i: (i, 0)),
                   pl.BlockSpec((sb, GDN_CONV - 1, GDN_QKV), lambda i: (i, 0, 0)),
                   pl.BlockSpec((sb, N_HEADS, HEAD_DIM, HEAD_DIM), lambda i: (i, 0, 0, 0))],
        out_shape=[jax.ShapeDtypeStruct((nseq * t, W_MIX), F32),
                   jax.ShapeDtypeStruct((nseq, GDN_CONV - 1, GDN_QKV), F32),
                   jax.ShapeDtypeStruct((nseq, N_HEADS, HEAD_DIM, HEAD_DIM), F32)],
        scratch_shapes=[pltpu.VMEM((t + GDN_HDR, GDN_QKV), F32)],
        compiler_params=_params(("arbitrary",)), name="gdn",
    )(*args)


def _rwkv_kernel(p_ref, sh_ref, s0_ref, mu_ref, w0_ref, a0_ref, lora_ref, kk_ref, ka_ref, rk_ref, g_ref, b_ref,
                 y_ref, sho_ref, s_ref, *, sb, t, c):
    ones = _head_ones()
    incl = _tri(c, False)
    strict = _tri(c, True)
    tri_ones = incl.astype(F32)
    lane = lax.broadcasted_iota(jnp.int32, (c, LANES), 1)
    row_id = lax.broadcasted_iota(jnp.int32, (c, COLS_D), 0)

    def seq(s, carry0):
        row0 = pl.multiple_of(s * t, SUBLANES)

        def chunk(n, carry):
            prev_row, states = carry
            rows = pl.ds(pl.multiple_of(row0 + n * c, SUBLANES), c)
            x = p_ref[rows, :]
            prev = jnp.where(row_id == 0, prev_row, pltpu.roll(x, 1, 0))
            xl = x + (prev - x) * mu_ref[...]
            r, k, v = xl[:, :W_MIX], xl[:, W_MIX:2 * W_MIX], xl[:, 2 * W_MIX:3 * W_MIX]
            lo = xl[:, 3 * W_MIX:]
            act = jnp.where(lane < LORA_W, jnp.tanh(lo), jnp.where(lane < LORA_W + LORA_A, lo, _sigmoid(lo)))
            lora = _dot_hi(act, lora_ref[...])
            wl = -_softplus(-(w0_ref[...] + lora[:, :W_MIX])) - 0.5
            lw = -jnp.exp(wl)
            a = _sigmoid(a0_ref[...] + lora[:, W_MIX:2 * W_MIX])
            g = lora[:, 2 * W_MIX:]
            kkp = k * kk_ref[...]
            kk = kkp * lax.rsqrt(_dot_hi(kkp * kkp, ones) + 1e-6)
            k2 = k * (1.0 + (a - 1.0) * ka_ref[...])
            cum = _dot_hi(tri_ones, lw)
            inv = jnp.exp(-cum)
            a_hat = -kk * jnp.exp(cum - lw)
            b_hat = kk * a * inv
            c_hat = k2 * inv
            q_hat = r * jnp.exp(cum)
            gam_c = jnp.exp(cum[c - 1:c, :])
            outs = []
            new_states = []
            for h in range(N_HEADS):
                hs = slice(h * HEAD_DIM, (h + 1) * HEAD_DIM)
                ah, bh, ch, qh, vh = a_hat[:, hs], b_hat[:, hs], c_hat[:, hs], q_hat[:, hs], v[:, hs]
                a_m = jnp.where(strict, _dot_nt_hi(ah, bh), 0.0)
                b_m = jnp.where(strict, _dot_nt_hi(ah, ch), 0.0)
                p_qb = jnp.where(incl, _dot_nt_hi(qh, bh), 0.0)
                p_qc = jnp.where(incl, _dot_nt_hi(qh, ch), 0.0)
                t_inv = _neumann_inverse(a_m, c)
                st = states[h]
                z = _dot_hi(t_inv, _dot_nt_hi(ah, st) + _dot_hi(b_m, vh))
                outs.append(_dot_nt_hi(qh, st) + _dot_hi(p_qb, z) + _dot_hi(p_qc, vh))
                new_states.append((st + _dot_tn_hi(z, bh) + _dot_tn_hi(vh, ch)) * gam_c[:, hs])
            y = jnp.concatenate(outs, axis=1)
            mean = _dot_hi(y, ones) * (1.0 / HEAD_DIM)
            yc = y - mean
            var = _dot_hi(yc * yc, ones) * (1.0 / HEAD_DIM)
            yn = yc * lax.rsqrt(var + GN_EPS) * g_ref[...] + b_ref[...]
            yn = yn + _dot_hi(r * k2 * rk_ref[...], ones) * v
            y_ref[rows, :] = yn * g
            return x[c - 1:c, :], tuple(new_states)

        init = (sh_ref[s], tuple(s0_ref[s, h] for h in range(N_HEADS)))
        last_row, states = lax.fori_loop(0, t // c, chunk, init)
        sho_ref[s] = last_row
        for h in range(N_HEADS):
            s_ref[s, h] = states[h]
        return carry0

    lax.fori_loop(0, sb, seq, 0)


def _rwkv_lora_weights(w2, a2, g2):
    m = jnp.zeros((LANES, 3 * W_MIX), F32)
    m = m.at[:LORA_W, :W_MIX].set(w2)
    m = m.at[LORA_W:LORA_W + LORA_A, W_MIX:2 * W_MIX].set(a2)
    return m.at[LORA_W + LORA_A:, 2 * W_MIX:].set(g2)


def _rwkv_call(p_d, shift, s0, mu, w0, a0, lora_w, k_k, k_a, r_k, ln_g, ln_b, row0, t):
    nseq = shift.shape[0]
    c = math.gcd(t, RWKV_CHUNK)
    sb = 1 if t >= 256 else _pick(nseq, (16, 8, 4, 2, 1))
    rows = sb * t
    blk0 = row0 // rows
    assert row0 % rows == 0 and c % SUBLANES == 0
    row = lambda x: x.reshape(1, -1)
    vec = lambda w: pl.BlockSpec((1, w), lambda i: (0, 0))
    kern = functools.partial(_rwkv_kernel, sb=sb, t=t, c=c)
    in_specs = [pl.BlockSpec((rows, P_D), lambda i: (blk0 + i, 0)),
                pl.BlockSpec((sb, 1, COLS_D), lambda i: (i, 0, 0)),
                pl.BlockSpec((sb, N_HEADS, HEAD_DIM, HEAD_DIM), lambda i: (i, 0, 0, 0)),
                vec(COLS_D), vec(W_MIX), vec(W_MIX),
                pl.BlockSpec((LANES, 3 * W_MIX), lambda i: (0, 0)),
                vec(W_MIX), vec(W_MIX), vec(W_MIX), vec(W_MIX), vec(W_MIX)]
    args = [p_d, shift.reshape(nseq, 1, COLS_D), s0, row(mu), row(w0), row(a0), lora_w,
            row(k_k), row(k_a), row(r_k), row(ln_g), row(ln_b)]
    y, sh, st = pl.pallas_call(
        kern, grid=(nseq // sb,), in_specs=in_specs,
        out_specs=[pl.BlockSpec((rows, W_MIX), lambda i: (i, 0)),
                   pl.BlockSpec((sb, 1, COLS_D), lambda i: (i, 0, 0)),
                   pl.BlockSpec((sb, N_HEADS, HEAD_DIM, HEAD_DIM), lambda i: (i, 0, 0, 0))],
        out_shape=[jax.ShapeDtypeStruct((nseq * t, W_MIX), F32),
                   jax.ShapeDtypeStruct((nseq, 1, COLS_D), F32),
                   jax.ShapeDtypeStruct((nseq, N_HEADS, HEAD_DIM, HEAD_DIM), F32)],
        compiler_params=_params(("arbitrary",)), name="rwkv",
    )(*args)
    return y, sh.reshape(nseq, COLS_D), st


ROUTE_IDX, ROUTE_GATE, ROUTE_RANK = 0, TOP_K, 2 * TOP_K


def _outproj_router_kernel(*refs, tm, alpha, n_prompt_tiles):
    yp_refs, ys_refs = refs[0:4], refs[4:8]
    x_ref, wo_ref, g_ref, b_ref, rw_ref, rb_ref, x1_ref, route_ref, cnt_ref = refs[8:]

    @pl.when(pl.program_id(0) == 0)
    def _():
        cnt_ref[...] = jnp.zeros_like(cnt_ref)

    is_prompt = pl.program_id(0) < n_prompt_tiles
    mix = jnp.zeros((tm, D_MODEL), F32)
    for i in range(4):
        y = jnp.where(is_prompt, yp_refs[i][...], ys_refs[i][...])
        mix = mix + _dot(y.astype(BF16), wo_ref[pl.ds(i * W_MIX, W_MIX), :])
    x1 = _layer_norm(alpha * x_ref[...] + mix, g_ref[...], b_ref[...], LN_EPS)
    x1_ref[...] = x1

    logits = _dot_hi(x1, rw_ref[...]) + rb_ref[...]
    lane = lax.broadcasted_iota(jnp.int32, (tm, LANES), 1)
    work = logits
    vals, hots = [], []
    for _ in range(TOP_K):
        m = jnp.max(work, axis=-1, keepdims=True)
        idx = jnp.min(jnp.where(work == m, lane, LANES), axis=-1, keepdims=True)
        hot = lane == idx
        vals.append(m)
        hots.append(hot)
        work = jnp.where(hot, -jnp.inf, work)
    exps = [jnp.exp(v - vals[0]) for v in vals]
    denom = exps[0] + exps[1] + exps[2] + exps[3]

    any_hot = jnp.zeros((tm, LANES), F32)
    for hot in hots:
        any_hot = any_hot + hot.astype(F32)
    before = _dot(_tri(tm, True).astype(BF16), any_hot.astype(BF16)) + cnt_ref[...]
    cnt_ref[...] = cnt_ref[...] + jnp.sum(any_hot, axis=0, keepdims=True)

    route = jnp.zeros((tm, LANES), F32)
    for kk in range(TOP_K):
        e_id = jnp.sum(jnp.where(hots[kk], lane, 0), axis=-1, keepdims=True).astype(F32)
        rank = jnp.sum(jnp.where(hots[kk], before, 0.0), axis=-1, keepdims=True)
        route = jnp.where(lane == ROUTE_IDX + kk, e_id, route)
        route = jnp.where(lane == ROUTE_GATE + kk, exps[kk] / denom, route)
        route = jnp.where(lane == ROUTE_RANK + kk, rank, route)
    route_ref[...] = route


def _outproj_router_call(ys_prompt, ys_sample, x, w_out, ln_g, ln_b, router_w, router_b, alpha):
    n = x.shape[0]
    n_p, n_s = ys_prompt[-1].shape[0], ys_sample[-1].shape[0]
    tm = _pick(math.gcd(n_p, n_s), (256, 128))
    npt = n_p // tm
    pmap = lambda y: (lambda i: (i, 0)) if y.shape[0] == n else (lambda i: (jnp.minimum(i, npt - 1), 0))
    smap = lambda y: (lambda i: (i, 0)) if y.shape[0] == n else (lambda i: (jnp.maximum(i - npt, 0), 0))
    row = lambda v: v.reshape(1, -1)
    vec = lambda w: pl.BlockSpec((1, w), lambda i: (0, 0))
    rw = jnp.pad(router_w, ((0, 0), (0, LANES - N_EXPERTS)))
    rb = jnp.pad(router_b, (0, LANES - N_EXPERTS), constant_values=NEG_BIG)
    return pl.pallas_call(
        functools.partial(_outproj_router_kernel, tm=tm, alpha=alpha, n_prompt_tiles=npt), grid=(n // tm,),
        in_specs=[pl.BlockSpec((tm, W_MIX), pmap(y)) for y in ys_prompt] + [
            pl.BlockSpec((tm, W_MIX), smap(y)) for y in ys_sample] + [
            pl.BlockSpec((tm, D_MODEL), lambda i: (i, 0)),
            pl.BlockSpec((D_MODEL, D_MODEL), lambda i: (0, 0)),
            vec(D_MODEL), vec(D_MODEL),
            pl.BlockSpec((D_MODEL, LANES), lambda i: (0, 0)), vec(LANES)],
        out_specs=[pl.BlockSpec((tm, D_MODEL), lambda i: (i, 0)),
                   pl.BlockSpec((tm, LANES), lambda i: (i, 0)),
                   pl.BlockSpec((1, LANES), lambda i: (0, 0))],
        out_shape=[jax.ShapeDtypeStruct((n, D_MODEL), F32),
                   jax.ShapeDtypeStruct((n, LANES), F32),
                   jax.ShapeDtypeStruct((1, LANES), F32)],
        compiler_params=_params(("arbitrary",)), name="outproj_router",
    )(*ys_prompt, *ys_sample, x, w_out, row(ln_g), row(ln_b), rw, row(rb))


MOE_TB = 256


def _moe_plan(route, counts, n):
    e_idx = route[:, ROUTE_IDX:ROUTE_IDX + TOP_K].astype(jnp.int32)
    rank = route[:, ROUTE_RANK:ROUTE_RANK + TOP_K].astype(jnp.int32)
    cnt = counts[0, :N_EXPERTS].astype(jnp.int32)
    padded = (cnt + MOE_TB - 1) // MOE_TB * MOE_TB
    pad_end = jnp.cumsum(padded)
    pad_start = pad_end - padded
    dest = (pad_start[e_idx] + rank).reshape(n * TOP_K)
    n_blocks = -(-n * TOP_K // MOE_TB) + N_EXPERTS
    n_used = pad_end[-1] // MOE_TB
    blk = jnp.minimum(jnp.arange(n_blocks), n_used - 1) * MOE_TB
    block_e = jnp.minimum(jnp.searchsorted(pad_end, blk, side='right'), N_EXPERTS - 1).astype(jnp.int32)
    last_block_row = jnp.where(padded > 0, pad_end - MOE_TB, -1)
    tail = n_used + jnp.arange(N_EXPERTS)
    tail_row = jnp.where(tail < n_blocks, tail * MOE_TB, -1)
    zero_rows = jnp.concatenate([last_block_row, tail_row]).astype(jnp.int32)
    return dest, block_e, n_used.reshape(1).astype(jnp.int32), zero_rows, n_blocks


def _dispatch_kernel(zrow_ref, dest_ref, x_ref, xs_ref, zbuf_ref, zsem, sem, *, tm):
    def zero_copy(e):
        row = pl.multiple_of(jnp.maximum(zrow_ref[e], 0), MOE_TB)
        return pltpu.make_async_copy(zbuf_ref, xs_ref.at[pl.ds(row, MOE_TB)], zsem)

    @pl.when(pl.program_id(0) == 0)
    def _():
        zbuf_ref[...] = jnp.zeros_like(zbuf_ref)
        for e in range(2 * N_EXPERTS):
            @pl.when(zrow_ref[e] >= 0)
            def _():
                zero_copy(e).start()
        for e in range(2 * N_EXPERTS):
            @pl.when(zrow_ref[e] >= 0)
            def _():
                zero_copy(e).wait()

    def row_copy(t, kk):
        return pltpu.make_async_copy(x_ref.at[pl.ds(t, 1)], xs_ref.at[pl.ds(dest_ref[t * TOP_K + kk], 1)], sem)

    def issue(t, c):
        for kk in range(TOP_K):
            row_copy(t, kk).start()
        return c

    def drain(t, c):
        for kk in range(TOP_K):
            row_copy(t, kk).wait()
        return c

    lax.fori_loop(0, tm, issue, 0)
    lax.fori_loop(0, tm, drain, 0)


def _dispatch_call(x1, dest, last_block_row, n_blocks):
    n = x1.shape[0]
    tm = _pick(n, (256, 128))
    return pl.pallas_call(
        functools.partial(_dispatch_kernel, tm=tm),
        grid_spec=pltpu.PrefetchScalarGridSpec(
            num_scalar_prefetch=1, grid=(n // tm,),
            in_specs=[pl.BlockSpec((tm * TOP_K,), lambda i, z: (i,), memory_space=pltpu.SMEM),
                      pl.BlockSpec((tm, D_MODEL), lambda i, z: (i, 0))],
            out_specs=pl.BlockSpec(memory_space=pl.ANY),
            scratch_shapes=[pltpu.VMEM((MOE_TB, D_MODEL), F32),
                            pltpu.SemaphoreType.DMA(()), pltpu.SemaphoreType.DMA(())]),
        out_shape=jax.ShapeDtypeStruct((n_blocks * MOE_TB, D_MODEL), F32),
        compiler_params=_params(("arbitrary",)), name="moe_dispatch",
    )(last_block_row, dest, x1)


def _expert_kernel(be_ref, nu_ref, x_ref, w1g_ref, w1l_ref, b1g_ref, b1l_ref, w2_ref, b2_ref, y_ref):
    @pl.when(pl.program_id(0) < nu_ref[0])
    def _():
        xb = x_ref[...].astype(BF16)
        hg = jnp.minimum(_dot(xb, w1g_ref[0]) + b1g_ref[0], SWIGLU_LIMIT)
        hl = jnp.clip(_dot(xb, w1l_ref[0]) + b1l_ref[0], -SWIGLU_LIMIT, SWIGLU_LIMIT)
        act = hg * _sigmoid(SWIGLU_ALPHA * hg) * (hl + 1.0)
        y_ref[...] = _dot(act.astype(BF16), w2_ref[0]) + b2_ref[0]

    @pl.when(pl.program_id(0) >= nu_ref[0])
    def _():
        y_ref[...] = jnp.zeros_like(y_ref)


def _expert_call(xs, block_e, n_used, w1g, w1l, b1g, b1l, w2, b2):
    n_blocks = xs.shape[0] // MOE_TB
    xmap = lambda i, be, nu: (jnp.minimum(i, nu[0] - 1), 0)
    emap3 = lambda i, be, nu: (be[i], 0, 0)
    return pl.pallas_call(
        _expert_kernel,
        grid_spec=pltpu.PrefetchScalarGridSpec(
            num_scalar_prefetch=2, grid=(n_blocks,),
            in_specs=[pl.BlockSpec((MOE_TB, D_MODEL), xmap),
                      pl.BlockSpec((1, D_MODEL, D_FF), emap3), pl.BlockSpec((1, D_MODEL, D_FF), emap3),
                      pl.BlockSpec((1, 1, D_FF), emap3), pl.BlockSpec((1, 1, D_FF), emap3),
                      pl.BlockSpec((1, D_FF, D_MODEL), emap3), pl.BlockSpec((1, 1, D_MODEL), emap3)],
            out_specs=pl.BlockSpec((MOE_TB, D_MODEL), lambda i, be, nu: (i, 0))),
        out_shape=jax.ShapeDtypeStruct(xs.shape, F32),
        compiler_params=_params(("arbitrary",)), name="moe_experts",
    )(block_e, n_used, xs, w1g, w1l, b1g, b1l, w2, b2)


def _combine_kernel(dest_ref, route_ref, x1_ref, ys_ref, g_ref, b_ref, x2_ref, buf_ref, sem, *, tm, alpha):
    def row_copy(t, kk):
        return pltpu.make_async_copy(ys_ref.at[pl.ds(dest_ref[t * TOP_K + kk], 1)], buf_ref.at[kk, pl.ds(t, 1)], sem)

    def issue(t, c):
        for kk in range(TOP_K):
            row_copy(t, kk).start()
        return c

    def drain(t, c):
        for kk in range(TOP_K):
            row_copy(t, kk).wait()
        return c

    lax.fori_loop(0, tm, issue, 0)
    lax.fori_loop(0, tm, drain, 0)
    route = route_ref[...]
    f = jnp.zeros((tm, D_MODEL), F32)
    for kk in range(TOP_K):
        f = f + buf_ref[kk] * route[:, ROUTE_GATE + kk:ROUTE_GATE + kk + 1]
    x2_ref[...] = _layer_norm(alpha * x1_ref[...] + f, g_ref[...], b_ref[...], LN_EPS)


def _combine_call(x1, route, dest, ys, ln_g, ln_b, alpha):
    n = x1.shape[0]
    tm = _pick(n, (256, 128))
    vec = pl.BlockSpec((1, D_MODEL), lambda i: (0, 0))
    return pl.pallas_call(
        functools.partial(_combine_kernel, tm=tm, alpha=alpha), grid=(n // tm,),
        in_specs=[pl.BlockSpec((tm * TOP_K,), lambda i: (i,), memory_space=pltpu.SMEM),
                  pl.BlockSpec((tm, LANES), lambda i: (i, 0)),
                  pl.BlockSpec((tm, D_MODEL), lambda i: (i, 0)),
                  pl.BlockSpec(memory_space=pl.ANY), vec, vec],
        out_specs=pl.BlockSpec((tm, D_MODEL), lambda i: (i, 0)),
        out_shape=jax.ShapeDtypeStruct((n, D_MODEL), F32),
        scratch_shapes=[pltpu.VMEM((TOP_K, tm, D_MODEL), F32), pltpu.SemaphoreType.DMA(())],
        compiler_params=_params(("arbitrary",)), name="moe_combine",
    )(dest, route, x1, ys, ln_g.reshape(1, -1), ln_b.reshape(1, -1))


def _moe_ffn(x1, route, counts, w1g, w1l, b1g, b1l, w2, b2, ln_g, ln_b, alpha):
    n = x1.shape[0]
    dest, block_e, n_used, zero_rows, n_blocks = _moe_plan(route, counts, n)
    xs = _dispatch_call(x1, dest, zero_rows, n_blocks)
    ys = _expert_call(xs, block_e, n_used, w1g, w1l, b1g, b1l, w2, b2)
    return _combine_call(x1, route, dest, ys, ln_g, ln_b, alpha)


def kernel(x_prompt, x_sample, state_gdn_conv, state_gdn_S, state_cc_conv, state_rwkv_shift, state_rwkv_S, ln_in_g, ln_in_b, w_in, sgu_ln_g, sgu_ln_b, sgu_w, sgu_b, gdn_conv_w, gdn_A_log, gdn_dt_bias, gdn_norm_g, cc_dw_w, cc_dw_b, cc_ln_g, cc_ln_b, rw_mu, rw_w0, rw_w2, rw_a0, rw_a2, rw_g2, rw_k_k, rw_k_a, rw_r_k, rw_ln_g, rw_ln_b, w_out, ln_mix_g, ln_mix_b, router_w, router_b, moe_w1, moe_b1, moe_w2, moe_b2, ln_ffn_g, ln_ffn_b):
    bp, tp, _ = x_prompt.shape
    bs, ts, _ = x_sample.shape
    n_p, n_s = bp * tp, bs * ts
    depth = w_in.shape[0]
    alpha = (2 * depth) ** 0.25
    assert tp % SGU_CHUNK == 0 and SGU_CHUNK % ts == 0

    x = jnp.concatenate([x_prompt.reshape(n_p, D_MODEL), x_sample.reshape(n_s, D_MODEL)], axis=0)
    x = _ln_call(x, ln_in_g, ln_in_b)
    zeros = lambda *s: jnp.zeros(s, F32)
    outs_p, outs_s = [], []
    for l in range(depth):
        p_d, p_bg, p_b, p_a, p_c = _proj_call(x, _reorder_w_in(w_in[l], gdn_A_log.shape[1]))

        w_eff, b_eff = _sgu_weights(sgu_w[l], sgu_b[l], ts)
        y_a, v = _sgu_call(p_a, w_eff, b_eff, sgu_ln_g[l], sgu_ln_b[l], n_p)
        v_p = v[:n_p].reshape(bp, tp, W_MIX)[:, ((tp - 1) // SGU_CHUNK) * SGU_CHUNK:]
        v_s = v[n_p:].reshape(bs, ts, W_MIX)

        gdn_w = (gdn_conv_w[l], gdn_A_log[l], gdn_dt_bias[l], gdn_norm_g[l])
        yb_p, gbuf_p, gs_p = _gdn_call(p_b, p_bg, zeros(bp, GDN_CONV - 1, GDN_QKV),
                                       zeros(bp, N_HEADS, HEAD_DIM, HEAD_DIM), *gdn_w, 0, tp)
        yb_s, gbuf_s, gs_s = _gdn_call(p_b, p_bg, state_gdn_conv[l], state_gdn_S[l], *gdn_w, n_p, ts)

        cc_w = (cc_dw_w[l], cc_dw_b[l], cc_ln_g[l], cc_ln_b[l])
        yc_p, cbuf_p = _cc_call(p_c, zeros(bp, CC_WIDTH - 1, W_MIX), *cc_w, 0, tp)
        yc_s, cbuf_s = _cc_call(p_c, state_cc_conv[l], *cc_w, n_p, ts)

        rw_w = (rw_mu[l], rw_w0[l], rw_a0[l], _rwkv_lora_weights(rw_w2[l], rw_a2[l], rw_g2[l]),
                rw_k_k[l], rw_k_a[l], rw_r_k[l].reshape(-1), rw_ln_g[l], rw_ln_b[l])
        yd_p, rsh_p, rs_p = _rwkv_call(p_d, zeros(bp, COLS_D), zeros(bp, N_HEADS, HEAD_DIM, HEAD_DIM), *rw_w, 0, tp)
        yd_s, rsh_s, rs_s = _rwkv_call(p_d, state_rwkv_shift[l], state_rwkv_S[l], *rw_w, n_p, ts)

        x1, route, counts = _outproj_router_call((y_a, yb_p, yc_p, yd_p), (y_a, yb_s, yc_s, yd_s), x,
                                                 w_out[l].astype(BF16), ln_mix_g[l], ln_mix_b[l],
                                                 router_w[l], router_b[l], alpha)
        w1, b1 = moe_w1[l], moe_b1[l]
        x = _moe_ffn(x1, route, counts, w1[:, :, 0::2].astype(BF16), w1[:, :, 1::2].astype(BF16),
                     b1[:, None, 0::2], b1[:, None, 1::2], moe_w2[l].astype(BF16), moe_b2[l][:, None, :],
                     ln_ffn_g[l], ln_ffn_b[l], alpha)
        outs_p.append((v_p, gbuf_p, gs_p, cbuf_p, rsh_p, rs_p))
        outs_s.append((v_s, gbuf_s, gs_s, cbuf_s, rsh_s, rs_s))

    stack = lambda outs, i: jnp.stack([o[i] for o in outs])
    res = [x[:n_p].reshape(bp, tp, D_MODEL), x[n_p:].reshape(bs, ts, D_MODEL)]
    for i in range(6):
        res += [stack(outs_p, i), stack(outs_s, i)]
    return tuple(res)
```

```python
import functools
import math

import jax
import jax.numpy as jnp
from jax import lax
from jax.experimental import pallas as pl
from jax.experimental.pallas import tpu as pltpu

F32 = jnp.float32
BF16 = jnp.bfloat16
HI = lax.Precision.HIGHEST

D_MODEL = 1024
HEAD_DIM = 64
W_MIX = 256
N_HEADS = W_MIX // HEAD_DIM
SGU_CHUNK = 128
GDN_CONV = 4
GDN_CHUNK = 64
CC_WIDTH = 31
RWKV_CHUNK = 32
RWKV_PASSES = 1
GDN_PASSES = 1
LORA_W, LORA_A, LORA_G = 32, 32, 64
COLS_D = 3 * W_MIX + LORA_W + LORA_A + LORA_G
N_EXPERTS = 32
TOP_K = 4
D_FF = D_MODEL
SWIGLU_ALPHA = 1.702
SWIGLU_LIMIT = 7.0
LN_EPS = 1e-5
RMS_EPS = 1e-6
GN_EPS = 64e-5
LANES = 128
SUBLANES = 8
VMEM_LIMIT = 56 * 1024 * 1024
NEG_BIG = -1e30

P_D = COLS_D
P_BG = LANES
P_B = 4 * W_MIX
P_A = 2 * W_MIX
P_C = 2 * W_MIX
P_TOTAL = P_D + P_BG + P_B + P_A + P_C


def _pick(n, cands):
    for c in cands:
        if n % c == 0:
            return c
    raise ValueError(f"no tile in {cands} divides {n}")


def _params(sem):
    return pltpu.CompilerParams(dimension_semantics=sem, vmem_limit_bytes=VMEM_LIMIT)


def _layer_norm(x, g, b, eps):
    xc = x - jnp.mean(x, -1, keepdims=True)
    var = jnp.mean(xc * xc, -1, keepdims=True)
    return xc * lax.rsqrt(var + eps) * g + b


def _sigmoid(x):
    return 1.0 / (1.0 + jnp.exp(-x))


def _silu(x):
    return x * _sigmoid(x)


def _softplus(x):
    return jnp.maximum(x, 0.0) + jnp.log(1.0 + jnp.exp(-jnp.abs(x)))


def _dot(a, b):
    return jnp.dot(a, b, preferred_element_type=F32)


def _dot_hi(a, b):
    return jnp.dot(a, b, precision=HI, preferred_element_type=F32)


def _dot_nt_hi(a, b):
    return lax.dot_general(a, b, (((1,), (1,)), ((), ())), precision=HI, preferred_element_type=F32)


def _head_ones():
    r = lax.broadcasted_iota(jnp.int32, (W_MIX, W_MIX), 0) // HEAD_DIM
    c = lax.broadcasted_iota(jnp.int32, (W_MIX, W_MIX), 1) // HEAD_DIM
    return (r == c).astype(F32)


def _tri(n, strict):
    r = lax.broadcasted_iota(jnp.int32, (n, n), 0)
    c = lax.broadcasted_iota(jnp.int32, (n, n), 1)
    return (r > c) if strict else (r >= c)


def _neumann_inverse(x, n):
    eye = (lax.broadcasted_iota(jnp.int32, (n, n), 0) == lax.broadcasted_iota(jnp.int32, (n, n), 1)).astype(F32)
    acc = eye + x
    p = x
    k = 2
    while k < n:
        p = _dot_hi(p, p)
        acc = acc + _dot_hi(acc, p)
        k *= 2
    return acc


_NN = (((1,), (0,)), ((), ()))
_NT = (((1,), (1,)), ((), ()))
_TN = (((0,), (0,)), ((), ()))


def _split2(x):
    hi = x.astype(BF16)
    return hi, (x - hi.astype(F32)).astype(BF16)


def _split3(x):
    hi = x.astype(BF16)
    r = x - hi.astype(F32)
    mid = r.astype(BF16)
    return hi, mid, (r - mid.astype(F32)).astype(BF16)


def _mm(a, b, dn=_NN, passes=1):
    d = lambda x, y: lax.dot_general(x, y, dn, preferred_element_type=F32)
    if passes == 1:
        return d(a.astype(BF16), b.astype(BF16))
    a_hi, a_lo = _split2(a)
    b_hi, b_lo = _split2(b)
    return d(a_hi, b_hi) + (d(a_lo, b_hi) + d(a_hi, b_lo))


def _mm_exact_rhs(a, sel, dn=_NN):
    d = lambda x: lax.dot_general(x, sel, dn, preferred_element_type=F32)
    hi, mid, lo = _split3(a)
    return d(hi) + (d(mid) + d(lo))


def _mm_exact_lhs(sel, b, dn=_NN):
    d = lambda x: lax.dot_general(sel, x, dn, preferred_element_type=F32)
    hi, mid, lo = _split3(b)
    return d(hi) + (d(mid) + d(lo))


def _block_neumann_inverse(x, block, passes):
    n = x.shape[0]
    eye = (lax.broadcasted_iota(jnp.int32, (n, n), 0) == lax.broadcasted_iota(jnp.int32, (n, n), 1)).astype(F32)
    acc = eye + x
    p = x
    k = 2
    while k < block:
        p = _mm(p, p, passes=passes)
        acc = acc + _mm(acc, p, passes=passes)
        k *= 2
    return acc


def _stack_masked(x, c):
    lane_head = lax.broadcasted_iota(jnp.int32, (c, W_MIX), 1) // HEAD_DIM
    return jnp.concatenate([jnp.where(lane_head == h, x, 0.0) for h in range(N_HEADS)], axis=0)


def _stack_heads(x):
    return jnp.concatenate([x[:, h * HEAD_DIM:(h + 1) * HEAD_DIM] for h in range(N_HEADS)], axis=0)


def _unstack_heads(x, c):
    return jnp.concatenate([x[h * c:(h + 1) * c] for h in range(N_HEADS)], axis=1)


def _block_tri(c, strict):
    n = N_HEADS * c
    r = lax.broadcasted_iota(jnp.int32, (n, n), 0)
    q = lax.broadcasted_iota(jnp.int32, (n, n), 1)
    same = (r // c) == (q // c)
    return same & ((r > q) if strict else (r >= q))


def _ln_kernel(x_ref, g_ref, b_ref, o_ref):
    o_ref[...] = _layer_norm(x_ref[...], g_ref[...], b_ref[...], LN_EPS)


def _ln_call(x, g, b):
    n = x.shape[0]
    tm = _pick(n, (1024, 512, 256, 128))
    return pl.pallas_call(
        _ln_kernel, grid=(n // tm,),
        in_specs=[pl.BlockSpec((tm, D_MODEL), lambda i: (i, 0)),
                  pl.BlockSpec((1, D_MODEL), lambda i: (0, 0)),
                  pl.BlockSpec((1, D_MODEL), lambda i: (0, 0))],
        out_specs=pl.BlockSpec((tm, D_MODEL), lambda i: (i, 0)),
        out_shape=jax.ShapeDtypeStruct((n, D_MODEL), F32),
        compiler_params=_params(("parallel",)), name="ln_in",
    )(x, g.reshape(1, -1), b.reshape(1, -1))


def _proj_kernel(x_ref, w_ref, pd_ref, pbg_ref, pb_ref, pa_ref, pc_ref):
    p = _dot(x_ref[...].astype(BF16), w_ref[...])
    o = 0
    for ref, w in ((pd_ref, P_D), (pbg_ref, P_BG), (pb_ref, P_B), (pa_ref, P_A), (pc_ref, P_C)):
        ref[...] = p[:, o:o + w]
        o += w


def _proj_call(x, w_cat):
    n = x.shape[0]
    tm = _pick(n, (512, 256, 128))
    widths = (P_D, P_BG, P_B, P_A, P_C)
    return pl.pallas_call(
        _proj_kernel, grid=(n // tm,),
        in_specs=[pl.BlockSpec((tm, D_MODEL), lambda i: (i, 0)),
                  pl.BlockSpec((D_MODEL, P_TOTAL), lambda i: (0, 0))],
        out_specs=[pl.BlockSpec((tm, w), lambda i: (i, 0)) for w in widths],
        out_shape=[jax.ShapeDtypeStruct((n, w), F32) for w in widths],
        compiler_params=_params(("parallel",)), name="proj_in",
    )(x, w_cat)


def _reorder_w_in(w_in, h_b):
    cols_a = 2 * W_MIX
    cols_b = 3 * W_MIX + 2 * h_b + W_MIX
    o1, o2 = cols_a, cols_a + cols_b
    o3 = o2 + 2 * W_MIX
    wa, wb, wc, wd = w_in[:, :o1], w_in[:, o1:o2], w_in[:, o2:o3], w_in[:, o3:]
    qkv, bg, z = wb[:, :3 * W_MIX], wb[:, 3 * W_MIX:3 * W_MIX + 2 * h_b], wb[:, 3 * W_MIX + 2 * h_b:]
    bg = jnp.pad(bg, ((0, 0), (0, P_BG - 2 * h_b)))
    return jnp.concatenate([wd, bg, qkv, z, wa, wc], axis=1).astype(BF16)


def _sgu_kernel(p_ref, w_ref, b_ref, g_ref, beta_ref, y_ref, v_ref, *, n_chunks):
    w = w_ref[0]
    bias = b_ref[0]
    lane_head = lax.broadcasted_iota(jnp.int32, (SGU_CHUNK, W_MIX), 1) // HEAD_DIM
    for c in range(n_chunks):
        rows = pl.ds(c * SGU_CHUNK, SGU_CHUNK)
        x = p_ref[rows, :]
        h = 0.5 * x * (1.0 + lax.erf(x * (1.0 / math.sqrt(2.0))))
        u = h[:, :W_MIX]
        v = _layer_norm(h[:, W_MIX:], g_ref[...], beta_ref[...], LN_EPS)
        v_ref[rows, :] = v
        vb = jnp.concatenate([jnp.where(lane_head == hh, v, 0.0) for hh in range(N_HEADS)], axis=0)
        s = _dot(w, vb.astype(BF16)) + bias
        y_ref[rows, :] = u * s


def _sgu_weights(sgu_w, sgu_b, t_s):
    causal = jnp.tril(jnp.ones((SGU_CHUNK, SGU_CHUNK), bool))
    wp = jnp.where(causal, sgu_w, 0.0)
    reps = SGU_CHUNK // t_s
    ws = jnp.stack([jnp.kron(jnp.eye(reps, dtype=F32), wp[h, :t_s, :t_s]) for h in range(N_HEADS)])
    cat = lambda w: jnp.concatenate([w[h] for h in range(N_HEADS)], axis=1)
    w_eff = jnp.stack([cat(wp), cat(ws)]).astype(BF16)
    bp = jnp.repeat(sgu_b.T, HEAD_DIM, axis=1)
    bs = jnp.tile(bp[:t_s], (reps, 1))
    return w_eff, jnp.stack([bp, bs])


def _sgu_call(p_a, w_eff, b_eff, ln_g, ln_b, n_prompt_rows):
    n = p_a.shape[0]
    tb = _pick(math.gcd(n_prompt_rows, n - n_prompt_rows), (1024, 512, 256, 128))
    n_prompt_tiles = n_prompt_rows // tb
    grp = lambda i: jnp.minimum(i // n_prompt_tiles, 1)
    return pl.pallas_call(
        functools.partial(_sgu_kernel, n_chunks=tb // SGU_CHUNK), grid=(n // tb,),
        in_specs=[pl.BlockSpec((tb, P_A), lambda i: (i, 0)),
                  pl.BlockSpec((1, SGU_CHUNK, N_HEADS * SGU_CHUNK), lambda i: (grp(i), 0, 0)),
                  pl.BlockSpec((1, SGU_CHUNK, W_MIX), lambda i: (grp(i), 0, 0)),
                  pl.BlockSpec((1, W_MIX), lambda i: (0, 0)),
                  pl.BlockSpec((1, W_MIX), lambda i: (0, 0))],
        out_specs=[pl.BlockSpec((tb, W_MIX), lambda i: (i, 0))] * 2,
        out_shape=[jax.ShapeDtypeStruct((n, W_MIX), F32)] * 2,
        compiler_params=_params(("parallel",)), name="sgu",
    )(p_a, w_eff, b_eff, ln_g.reshape(1, -1), ln_b.reshape(1, -1))


CC_HDR = 32


def _cc_kernel(p_ref, buf_ref, w_ref, wb_ref, g_ref, b_ref, y_ref, nb_ref, xp_ref, *, sb, t, tt):
    hist = CC_WIDTH - 1
    ones = _head_ones()
    w = w_ref[...]
    for s in range(sb):
        x = p_ref[pl.ds(s * t, t), :]
        xp_ref[pl.ds(CC_HDR - hist, hist), :] = buf_ref[s]
        xp_ref[pl.ds(CC_HDR, t), :] = x[:, :W_MIX] * _sigmoid(x[:, W_MIX:])
        nb_ref[s] = xp_ref[pl.ds(t + CC_HDR - hist, hist), :]

        def tile(i, carry):
            base = pl.multiple_of(i * tt, SUBLANES)
            win = xp_ref[pl.ds(base, tt + CC_HDR), :]
            acc = jnp.zeros((tt, W_MIX), F32)
            for j in range(CC_WIDTH):
                o = j + CC_HDR - hist
                acc = acc + win[o:o + tt] * w[j:j + 1]
            hh = acc + wb_ref[...]
            mean = _dot_hi(hh, ones) * (1.0 / HEAD_DIM)
            xc = hh - mean
            var = _dot_hi(xc * xc, ones) * (1.0 / HEAD_DIM)
            yy = xc * lax.rsqrt(var + LN_EPS) * g_ref[...] + b_ref[...]
            y_ref[pl.ds(pl.multiple_of(s * t + base, SUBLANES), tt), :] = _silu(yy)
            return carry

        lax.fori_loop(0, t // tt, tile, 0)


def _cc_call(p_c, buf, w, wb, g, b, row0, t):
    nseq = buf.shape[0]
    sb = 1 if t >= 256 else _pick(nseq, (16, 8, 4, 2, 1))
    tt = min(t, 256)
    rows = sb * t
    blk0 = row0 // rows
    assert row0 % rows == 0 and t % tt == 0
    kern = functools.partial(_cc_kernel, sb=sb, t=t, tt=tt)
    in_specs = [pl.BlockSpec((rows, P_C), lambda i: (blk0 + i, 0)),
                pl.BlockSpec((sb, CC_WIDTH - 1, W_MIX), lambda i: (i, 0, 0)),
                pl.BlockSpec((CC_WIDTH, W_MIX), lambda i: (0, 0)),
                pl.BlockSpec((1, W_MIX), lambda i: (0, 0)),
                pl.BlockSpec((1, W_MIX), lambda i: (0, 0)),
                pl.BlockSpec((1, W_MIX), lambda i: (0, 0))]
    args = [p_c, buf, w, wb.reshape(1, -1), g.reshape(1, -1), b.reshape(1, -1)]
    return pl.pallas_call(
        kern, grid=(nseq // sb,), in_specs=in_specs,
        out_specs=[pl.BlockSpec((rows, W_MIX), lambda i: (i, 0)),
                   pl.BlockSpec((sb, CC_WIDTH - 1, W_MIX), lambda i: (i, 0, 0))],
        out_shape=[jax.ShapeDtypeStruct((nseq * t, W_MIX), F32),
                   jax.ShapeDtypeStruct((nseq, CC_WIDTH - 1, W_MIX), F32)],
        scratch_shapes=[pltpu.VMEM((t + CC_HDR, W_MIX), F32)],
        compiler_params=_params(("arbitrary",)), name="cc",
    )(*args)


GDN_HDR = 8
GDN_QKV = 3 * W_MIX


def _dot_tn_hi(a, b):
    return lax.dot_general(a, b, (((0,), (0,)), ((), ())), precision=HI, preferred_element_type=F32)


def _lane_expand(src_lane0):
    r = lax.broadcasted_iota(jnp.int32, (LANES, W_MIX), 0)
    c = lax.broadcasted_iota(jnp.int32, (LANES, W_MIX), 1) // HEAD_DIM
    return (r == c + src_lane0).astype(F32)


def _gdn_kernel(p_ref, bg_ref, buf_ref, s0_ref, cw_ref, alog_ref, dt_ref, ng_ref,
                y_ref, nb_ref, s_ref, xp_ref, *, sb, t, c):
    hist = GDN_CONV - 1
    hc = N_HEADS * c
    ones = _head_ones().astype(BF16)
    e_beta = _lane_expand(0).astype(BF16)
    e_g = _lane_expand(N_HEADS).astype(BF16)
    tri_ones = _tri(c, False).astype(BF16)
    strict_bd = _block_tri(c, True)
    incl_bd = _block_tri(c, False)
    eye_hc = (lax.broadcasted_iota(jnp.int32, (hc, hc), 0) == lax.broadcasted_iota(jnp.int32, (hc, hc), 1))
    eye_w = (lax.broadcasted_iota(jnp.int32, (W_MIX, W_MIX), 0) == lax.broadcasted_iota(jnp.int32, (W_MIX, W_MIX), 1))
    first_lane = (lax.broadcasted_iota(jnp.int32, (W_MIX, hc), 0) % HEAD_DIM == 0).astype(BF16)
    ones_hc = jnp.ones((hc, hc), BF16)
    ones_wv = jnp.ones((W_MIX, HEAD_DIM), BF16)
    cw = cw_ref[...]
    neg_a = -jnp.exp(alog_ref[...])
    mm = functools.partial(_mm, passes=GDN_PASSES)

    def seq(s, carry0):
        row0 = pl.multiple_of(s * t, SUBLANES)
        xp_ref[pl.ds(GDN_HDR - hist, hist), :] = buf_ref[s]
        xp_ref[pl.ds(GDN_HDR, t), :] = p_ref[pl.ds(row0, t), :GDN_QKV]
        nb_ref[s] = xp_ref[pl.ds(t + GDN_HDR - hist, hist), :]

        def chunk(n, st):
            base = pl.multiple_of(n * c, SUBLANES)
            rows = pl.ds(pl.multiple_of(row0 + base, SUBLANES), c)
            win = xp_ref[pl.ds(base, c + GDN_HDR), :]
            conv = jnp.zeros((c, GDN_QKV), F32)
            for j in range(GDN_CONV):
                o = j + GDN_HDR - hist
                conv = conv + win[o:o + c] * cw[j:j + 1]
            act = _silu(conv)
            q, k, v = act[:, :W_MIX], act[:, W_MIX:2 * W_MIX], act[:, 2 * W_MIX:]
            sq = _mm_exact_rhs(jnp.concatenate([q * q, k * k], axis=0), ones)
            q = q * lax.rsqrt(sq[:c] + 1e-6) * (HEAD_DIM ** -0.5)
            k = k * lax.rsqrt(sq[c:] + 1e-6)
            bg = bg_ref[rows, :]
            beta = _mm_exact_rhs(_sigmoid(bg), e_beta)
            g = neg_a * _softplus(bg + dt_ref[...])
            gcum = _mm_exact_rhs(_mm_exact_lhs(tri_ones, g), e_g)
            kb = k * beta
            vb = v * beta
            egc = jnp.exp(gcum)
            glast = gcum[c - 1:c, :]
            k_dec = k * jnp.exp(glast - gcum)
            z = p_ref[rows, GDN_QKV:]

            gcol = _mm_exact_rhs(_stack_masked(gcum, c), first_lane)
            grow = _mm_exact_lhs(ones_hc, jnp.where(eye_hc, gcol, 0.0))
            decay = jnp.where(incl_bd, jnp.exp(gcol - grow), 0.0)
            prod = mm(jnp.concatenate([_stack_masked(kb, c), _stack_masked(q, c)], axis=0),
                      _stack_masked(k, c), _NT)
            lm = jnp.where(strict_bd, prod[:hc] * decay, 0.0)
            a_qk = prod[hc:] * decay
            t_inv = _block_neumann_inverse(-lm, c, GDN_PASSES)
            uw = mm(t_inv, jnp.concatenate([_stack_heads(vb), _stack_masked(kb * egc, c)], axis=1))
            u, w_m = uw[:, :HEAD_DIM], uw[:, HEAD_DIM:]
            ws = mm(jnp.concatenate([w_m, _stack_masked(q * egc, c)], axis=0), st)
            v_new = u - ws[:hc]
            o = _unstack_heads(ws[hc:] + mm(a_qk, v_new), c)
            g_rows = _mm_exact_rhs(jnp.where(eye_w, jnp.exp(glast), 0.0), ones_wv)
            st = st * g_rows + mm(_stack_masked(k_dec, c), v_new, _TN)
            o = o * lax.rsqrt(_mm_exact_rhs(o * o, ones) * (1.0 / HEAD_DIM) + RMS_EPS) * ng_ref[...]
            y_ref[rows, :] = o * _silu(z)
            return st

        st0 = jnp.concatenate([s0_ref[s, h] for h in range(N_HEADS)], axis=0)
        st = lax.fori_loop(0, t // c, chunk, st0)
        for h in range(N_HEADS):
            s_ref[s, h] = st[h * HEAD_DIM:(h + 1) * HEAD_DIM]
        return carry0

    lax.fori_loop(0, sb, seq, 0)


def _gdn_call(p_b, p_bg, buf, s0, conv_w, a_log, dt_bias, norm_g, row0, t):
    nseq = buf.shape[0]
    h_b = a_log.shape[0]
    assert h_b == N_HEADS
    c = math.gcd(t, GDN_CHUNK)
    sb = 1 if t >= 256 else _pick(nseq, (16, 8, 4, 2, 1))
    rows = sb * t
    blk0 = row0 // rows
    assert row0 % rows == 0 and c % SUBLANES == 0
    lane_pad = lambda x: jnp.pad(x.reshape(1, -1), ((0, 0), (h_b, LANES - 2 * h_b)))
    kern = functools.partial(_gdn_kernel, sb=sb, t=t, c=c)
    in_specs = [pl.BlockSpec((rows, P_B), lambda i: (blk0 + i, 0)),
                pl.BlockSpec((rows, P_BG), lambda i: (blk0 + i, 0)),
                pl.BlockSpec((sb, GDN_CONV - 1, GDN_QKV), lambda i: (i, 0, 0)),
                pl.BlockSpec((sb, N_HEADS, HEAD_DIM, HEAD_DIM), lambda i: (i, 0, 0, 0)),
                pl.BlockSpec((GDN_CONV, GDN_QKV), lambda i: (0, 0)),
                pl.BlockSpec((1, LANES), lambda i: (0, 0)),
                pl.BlockSpec((1, LANES), lambda i: (0, 0)),
                pl.BlockSpec((1, W_MIX), lambda i: (0, 0))]
    args = [p_b, p_bg, buf, s0, conv_w, lane_pad(a_log), lane_pad(dt_bias), jnp.tile(norm_g, N_HEADS).reshape(1, -1)]
    return pl.pallas_call(
        kern, grid=(nseq // sb,), in_specs=in_specs,
        out_specs=[pl.BlockSpec((rows, W_MIX), lambda i: (i, 0)),
                   pl.BlockSpec((sb, GDN_CONV - 1, GDN_QKV), lambda i: (i, 0, 0)),
                   pl.BlockSpec((sb, N_HEADS, HEAD_DIM, HEAD_DIM), lambda i: (i, 0, 0, 0))],
        out_shape=[jax.ShapeDtypeStruct((nseq * t, W_MIX), F32),
                   jax.ShapeDtypeStruct((nseq, GDN_CONV - 1, GDN_QKV), F32),
                   jax.ShapeDtypeStruct((nseq, N_HEADS, HEAD_DIM, HEAD_DIM), F32)],
        scratch_shapes=[pltpu.VMEM((t + GDN_HDR, GDN_QKV), F32)],
        compiler_params=_params(("arbitrary",)), name="gdn",
    )(*args)


def _rwkv_kernel(p_ref, sh_ref, s0_ref, mu_ref, w0_ref, a0_ref, lora_ref, kk_ref, ka_ref, rk_ref, g_ref, b_ref,
                 y_ref, sho_ref, s_ref, *, sb, t, c):
    hc = N_HEADS * c
    ones = _head_ones().astype(BF16)
    tri_ones = _tri(c, False).astype(BF16)
    strict_bd = _block_tri(c, True)
    incl_bd = _block_tri(c, False)
    lane = lax.broadcasted_iota(jnp.int32, (c, LANES), 1)
    row_id = lax.broadcasted_iota(jnp.int32, (c, COLS_D), 0)
    mm = functools.partial(_mm, passes=RWKV_PASSES)

    def seq(s, carry0):
        row0 = pl.multiple_of(s * t, SUBLANES)

        def chunk(n, carry):
            prev_row, st = carry
            rows = pl.ds(pl.multiple_of(row0 + n * c, SUBLANES), c)
            x = p_ref[rows, :]
            prev = jnp.where(row_id == 0, prev_row, pltpu.roll(x, 1, 0))
            xl = x + (prev - x) * mu_ref[...]
            r, k, v = xl[:, :W_MIX], xl[:, W_MIX:2 * W_MIX], xl[:, 2 * W_MIX:3 * W_MIX]
            lo = xl[:, 3 * W_MIX:]
            act = jnp.where(lane < LORA_W, jnp.tanh(lo), jnp.where(lane < LORA_W + LORA_A, lo, _sigmoid(lo)))
            lora = _dot_hi(act, lora_ref[...])
            wl = -_softplus(-(w0_ref[...] + lora[:, :W_MIX])) - 0.5
            lw = -jnp.exp(wl)
            a = _sigmoid(a0_ref[...] + lora[:, W_MIX:2 * W_MIX])
            g = lora[:, 2 * W_MIX:]
            kkp = k * kk_ref[...]
            k2 = k * (1.0 + (a - 1.0) * ka_ref[...])
            sums = _mm_exact_rhs(jnp.concatenate([kkp * kkp, r * k2 * rk_ref[...]], axis=0), ones)
            kk = kkp * lax.rsqrt(sums[:c] + 1e-6)
            bonus = sums[c:]
            cum = _mm_exact_lhs(tri_ones, lw)
            inv = jnp.exp(-cum)
            a_hat = -kk * jnp.exp(cum - lw)
            b_hat = kk * a * inv
            c_hat = k2 * inv
            q_hat = r * jnp.exp(cum)
            gam_c = jnp.exp(cum[c - 1:c, :])

            xaq = jnp.concatenate([_stack_masked(a_hat, c), _stack_masked(q_hat, c)], axis=0)
            bcs = jnp.concatenate([_stack_masked(b_hat, c), _stack_masked(c_hat, c)], axis=0)
            prod = mm(xaq, bcs, _NT)
            a_m = jnp.where(strict_bd, prod[:hc, :hc], 0.0)
            b_m = jnp.where(strict_bd, prod[:hc, hc:], 0.0)
            p_q = jnp.concatenate([jnp.where(incl_bd, prod[hc:, :hc], 0.0),
                                   jnp.where(incl_bd, prod[hc:, hc:], 0.0)], axis=1)
            t_inv = _block_neumann_inverse(a_m, c, RWKV_PASSES)
            vs = _stack_heads(v)
            zo = mm(xaq, st, _NT)
            z = mm(t_inv, zo[:hc] + mm(b_m, vs))
            zv = jnp.concatenate([z, vs], axis=0)
            o = zo[hc:] + mm(p_q, zv)
            st = (st + mm(zv, bcs, _TN)) * gam_c

            y = _unstack_heads(o, c)
            mean = _mm_exact_rhs(y, ones) * (1.0 / HEAD_DIM)
            yc = y - mean
            var = _mm_exact_rhs(yc * yc, ones) * (1.0 / HEAD_DIM)
            yn = yc * lax.rsqrt(var + GN_EPS) * g_ref[...] + b_ref[...]
            y_ref[rows, :] = (yn + bonus * v) * g
            return x[c - 1:c, :], st

        st0 = jnp.concatenate([s0_ref[s, h] for h in range(N_HEADS)], axis=1)
        last_row, st = lax.fori_loop(0, t // c, chunk, (sh_ref[s], st0))
        sho_ref[s] = last_row
        for h in range(N_HEADS):
            s_ref[s, h] = st[:, h * HEAD_DIM:(h + 1) * HEAD_DIM]
        return carry0

    lax.fori_loop(0, sb, seq, 0)


def _rwkv_lora_weights(w2, a2, g2):
    m = jnp.zeros((LANES, 3 * W_MIX), F32)
    m = m.at[:LORA_W, :W_MIX].set(w2)
    m = m.at[LORA_W:LORA_W + LORA_A, W_MIX:2 * W_MIX].set(a2)
    return m.at[LORA_W + LORA_A:, 2 * W_MIX:].set(g2)


def _rwkv_call(p_d, shift, s0, mu, w0, a0, lora_w, k_k, k_a, r_k, ln_g, ln_b, row0, t):
    nseq = shift.shape[0]
    c = math.gcd(t, RWKV_CHUNK)
    sb = 1 if t >= 256 else _pick(nseq, (16, 8, 4, 2, 1))
    rows = sb * t
    blk0 = row0 // rows
    assert row0 % rows == 0 and c % SUBLANES == 0
    row = lambda x: x.reshape(1, -1)
    vec = lambda w: pl.BlockSpec((1, w), lambda i: (0, 0))
    kern = functools.partial(_rwkv_kernel, sb=sb, t=t, c=c)
    in_specs = [pl.BlockSpec((rows, P_D), lambda i: (blk0 + i, 0)),
                pl.BlockSpec((sb, 1, COLS_D), lambda i: (i, 0, 0)),
                pl.BlockSpec((sb, N_HEADS, HEAD_DIM, HEAD_DIM), lambda i: (i, 0, 0, 0)),
                vec(COLS_D), vec(W_MIX), vec(W_MIX),
                pl.BlockSpec((LANES, 3 * W_MIX), lambda i: (0, 0)),
                vec(W_MIX), vec(W_MIX), vec(W_MIX), vec(W_MIX), vec(W_MIX)]
    args = [p_d, shift.reshape(nseq, 1, COLS_D), s0, row(mu), row(w0), row(a0), lora_w,
            row(k_k), row(k_a), row(r_k), row(ln_g), row(ln_b)]
    y, sh, st = pl.pallas_call(
        kern, grid=(nseq // sb,), in_specs=in_specs,
        out_specs=[pl.BlockSpec((rows, W_MIX), lambda i: (i, 0)),
                   pl.BlockSpec((sb, 1, COLS_D), lambda i: (i, 0, 0)),
                   pl.BlockSpec((sb, N_HEADS, HEAD_DIM, HEAD_DIM), lambda i: (i, 0, 0, 0))],
        out_shape=[jax.ShapeDtypeStruct((nseq * t, W_MIX), F32),
                   jax.ShapeDtypeStruct((nseq, 1, COLS_D), F32),
                   jax.ShapeDtypeStruct((nseq, N_HEADS, HEAD_DIM, HEAD_DIM), F32)],
        compiler_params=_params(("arbitrary",)), name="rwkv",
    )(*args)
    return y, sh.reshape(nseq, COLS_D), st


ROUTE_IDX, ROUTE_GATE, ROUTE_RANK = 0, TOP_K, 2 * TOP_K


def _outproj_router_kernel(*refs, tm, alpha, n_prompt_tiles):
    yp_refs, ys_refs = refs[0:4], refs[4:8]
    x_ref, wo_ref, g_ref, b_ref, rw_ref, rb_ref, x1_ref, route_ref, cnt_ref = refs[8:]

    @pl.when(pl.program_id(0) == 0)
    def _():
        cnt_ref[...] = jnp.zeros_like(cnt_ref)

    is_prompt = pl.program_id(0) < n_prompt_tiles
    mix = jnp.zeros((tm, D_MODEL), F32)
    for i in range(4):
        y = jnp.where(is_prompt, yp_refs[i][...], ys_refs[i][...])
        mix = mix + _dot(y.astype(BF16), wo_ref[pl.ds(i * W_MIX, W_MIX), :])
    x1 = _layer_norm(alpha * x_ref[...] + mix, g_ref[...], b_ref[...], LN_EPS)
    x1_ref[...] = x1

    logits = _dot_hi(x1, rw_ref[...]) + rb_ref[...]
    lane = lax.broadcasted_iota(jnp.int32, (tm, LANES), 1)
    work = logits
    vals, hots = [], []
    for _ in range(TOP_K):
        m = jnp.max(work, axis=-1, keepdims=True)
        idx = jnp.min(jnp.where(work == m, lane, LANES), axis=-1, keepdims=True)
        hot = lane == idx
        vals.append(m)
        hots.append(hot)
        work = jnp.where(hot, -jnp.inf, work)
    exps = [jnp.exp(v - vals[0]) for v in vals]
    denom = exps[0] + exps[1] + exps[2] + exps[3]

    any_hot = jnp.zeros((tm, LANES), F32)
    for hot in hots:
        any_hot = any_hot + hot.astype(F32)
    before = _dot(_tri(tm, True).astype(BF16), any_hot.astype(BF16)) + cnt_ref[...]
    cnt_ref[...] = cnt_ref[...] + jnp.sum(any_hot, axis=0, keepdims=True)

    route = jnp.zeros((tm, LANES), F32)
    for kk in range(TOP_K):
        e_id = jnp.sum(jnp.where(hots[kk], lane, 0), axis=-1, keepdims=True).astype(F32)
        rank = jnp.sum(jnp.where(hots[kk], before, 0.0), axis=-1, keepdims=True)
        route = jnp.where(lane == ROUTE_IDX + kk, e_id, route)
        route = jnp.where(lane == ROUTE_GATE + kk, exps[kk] / denom, route)
        route = jnp.where(lane == ROUTE_RANK + kk, rank, route)
    route_ref[...] = route


def _outproj_router_call(ys_prompt, ys_sample, x, w_out, ln_g, ln_b, router_w, router_b, alpha):
    n = x.shape[0]
    n_p, n_s = ys_prompt[-1].shape[0], ys_sample[-1].shape[0]
    tm = _pick(math.gcd(n_p, n_s), (256, 128))
    npt = n_p // tm
    pmap = lambda y: (lambda i: (i, 0)) if y.shape[0] == n else (lambda i: (jnp.minimum(i, npt - 1), 0))
    smap = lambda y: (lambda i: (i, 0)) if y.shape[0] == n else (lambda i: (jnp.maximum(i - npt, 0), 0))
    row = lambda v: v.reshape(1, -1)
    vec = lambda w: pl.BlockSpec((1, w), lambda i: (0, 0))
    rw = jnp.pad(router_w, ((0, 0), (0, LANES - N_EXPERTS)))
    rb = jnp.pad(router_b, (0, LANES - N_EXPERTS), constant_values=NEG_BIG)
    return pl.pallas_call(
        functools.partial(_outproj_router_kernel, tm=tm, alpha=alpha, n_prompt_tiles=npt), grid=(n // tm,),
        in_specs=[pl.BlockSpec((tm, W_MIX), pmap(y)) for y in ys_prompt] + [
            pl.BlockSpec((tm, W_MIX), smap(y)) for y in ys_sample] + [
            pl.BlockSpec((tm, D_MODEL), lambda i: (i, 0)),
            pl.BlockSpec((D_MODEL, D_MODEL), lambda i: (0, 0)),
            vec(D_MODEL), vec(D_MODEL),
            pl.BlockSpec((D_MODEL, LANES), lambda i: (0, 0)), vec(LANES)],
        out_specs=[pl.BlockSpec((tm, D_MODEL), lambda i: (i, 0)),
                   pl.BlockSpec((tm, LANES), lambda i: (i, 0)),
                   pl.BlockSpec((1, LANES), lambda i: (0, 0))],
        out_shape=[jax.ShapeDtypeStruct((n, D_MODEL), F32),
                   jax.ShapeDtypeStruct((n, LANES), F32),
                   jax.ShapeDtypeStruct((1, LANES), F32)],
        compiler_params=_params(("arbitrary",)), name="outproj_router",
    )(*ys_prompt, *ys_sample, x, w_out, row(ln_g), row(ln_b), rw, row(rb))


MOE_TB = 256


def _moe_plan(route, counts, n):
    e_idx = route[:, ROUTE_IDX:ROUTE_IDX + TOP_K].astype(jnp.int32)
    rank = route[:, ROUTE_RANK:ROUTE_RANK + TOP_K].astype(jnp.int32)
    cnt = counts[0, :N_EXPERTS].astype(jnp.int32)
    padded = (cnt + MOE_TB - 1) // MOE_TB * MOE_TB
    pad_end = jnp.cumsum(padded)
    pad_start = pad_end - padded
    dest = (pad_start[e_idx] + rank).reshape(n * TOP_K)
    n_blocks = -(-n * TOP_K // MOE_TB) + N_EXPERTS
    n_used = pad_end[-1] // MOE_TB
    blk = jnp.minimum(jnp.arange(n_blocks), n_used - 1) * MOE_TB
    block_e = jnp.minimum(jnp.sum(pad_end[None, :] <= blk[:, None], axis=1), N_EXPERTS - 1).astype(jnp.int32)
    last_block_row = jnp.where(padded > 0, pad_end - MOE_TB, -1)
    tail = n_used + jnp.arange(N_EXPERTS)
    tail_row = jnp.where(tail < n_blocks, tail * MOE_TB, -1)
    zero_rows = jnp.concatenate([last_block_row, tail_row]).astype(jnp.int32)
    return dest, block_e, n_used.reshape(1).astype(jnp.int32), zero_rows, n_blocks


def _dispatch_kernel(zrow_ref, dest_ref, x_ref, xs_ref, zbuf_ref, zsem, sem, *, tm):
    def zero_copy(e):
        row = pl.multiple_of(jnp.maximum(zrow_ref[e], 0), MOE_TB)
        return pltpu.make_async_copy(zbuf_ref, xs_ref.at[pl.ds(row, MOE_TB)], zsem)

    @pl.when(pl.program_id(0) == 0)
    def _():
        zbuf_ref[...] = jnp.zeros_like(zbuf_ref)
        for e in range(2 * N_EXPERTS):
            @pl.when(zrow_ref[e] >= 0)
            def _():
                zero_copy(e).start()
        for e in range(2 * N_EXPERTS):
            @pl.when(zrow_ref[e] >= 0)
            def _():
                zero_copy(e).wait()

    def row_copy(t, kk):
        return pltpu.make_async_copy(x_ref.at[pl.ds(t, 1)], xs_ref.at[pl.ds(dest_ref[t * TOP_K + kk], 1)], sem)

    def issue(t, c):
        for kk in range(TOP_K):
            row_copy(t, kk).start()
        return c

    def drain(t, c):
        for kk in range(TOP_K):
            row_copy(t, kk).wait()
        return c

    lax.fori_loop(0, tm, issue, 0)
    lax.fori_loop(0, tm, drain, 0)


def _dispatch_call(x1, dest, last_block_row, n_blocks):
    n = x1.shape[0]
    tm = _pick(n, (256, 128))
    return pl.pallas_call(
        functools.partial(_dispatch_kernel, tm=tm),
        grid_spec=pltpu.PrefetchScalarGridSpec(
            num_scalar_prefetch=1, grid=(n // tm,),
            in_specs=[pl.BlockSpec((tm * TOP_K,), lambda i, z: (i,), memory_space=pltpu.SMEM),
                      pl.BlockSpec((tm, D_MODEL), lambda i, z: (i, 0))],
            out_specs=pl.BlockSpec(memory_space=pl.ANY),
            scratch_shapes=[pltpu.VMEM((MOE_TB, D_MODEL), F32),
                            pltpu.SemaphoreType.DMA(()), pltpu.SemaphoreType.DMA(())]),
        out_shape=jax.ShapeDtypeStruct((n_blocks * MOE_TB, D_MODEL), F32),
        compiler_params=_params(("arbitrary",)), name="moe_dispatch",
    )(last_block_row, dest, x1)


PAIR_GROUP = 2 * LANES


def _w1_regroup_kernel(w_ref, o_ref):
    r = lax.broadcasted_iota(jnp.int32, (PAIR_GROUP, PAIR_GROUP), 0)
    c = lax.broadcasted_iota(jnp.int32, (PAIR_GROUP, PAIR_GROUP), 1)
    perm = (r == jnp.where(c < LANES, 2 * c, 2 * (c - LANES) + 1)).astype(BF16)
    for g in range(w_ref.shape[1] // PAIR_GROUP):
        cols = pl.ds(g * PAIR_GROUP, PAIR_GROUP)
        o_ref[:, cols] = _dot(w_ref[:, cols].astype(BF16), perm).astype(BF16)


def _w1_regroup_call(w1):
    e, d, f2 = w1.shape
    rows = e * d
    tr = _pick(rows, (512, 256, 128))
    out = pl.pallas_call(
        _w1_regroup_kernel, grid=(rows // tr,),
        in_specs=[pl.BlockSpec((tr, f2), lambda i: (i, 0))],
        out_specs=pl.BlockSpec((tr, f2), lambda i: (i, 0)),
        out_shape=jax.ShapeDtypeStruct((rows, f2), BF16),
        compiler_params=_params(("parallel",)), name="w1_regroup",
    )(w1.reshape(rows, f2))
    return out.reshape(e, d, f2)


def _regroup_bias(b1):
    e, f2 = b1.shape
    return b1.reshape(e, f2 // PAIR_GROUP, LANES, 2).swapaxes(2, 3).reshape(e, 1, f2)


def _expert_kernel(be_ref, nu_ref, x_ref, w1_ref, b1_ref, w2_ref, b2_ref, y_ref):
    @pl.when(pl.program_id(0) < nu_ref[0])
    def _():
        h = _dot(x_ref[...].astype(BF16), w1_ref[0]) + b1_ref[0]
        acts = []
        for g in range(h.shape[1] // PAIR_GROUP):
            hg = jnp.minimum(h[:, g * PAIR_GROUP:g * PAIR_GROUP + LANES], SWIGLU_LIMIT)
            hl = jnp.clip(h[:, g * PAIR_GROUP + LANES:(g + 1) * PAIR_GROUP], -SWIGLU_LIMIT, SWIGLU_LIMIT)
            acts.append((hg * _sigmoid(SWIGLU_ALPHA * hg) * (hl + 1.0)).astype(BF16))
        y_ref[...] = _dot(jnp.concatenate(acts, axis=1), w2_ref[0]) + b2_ref[0]

    @pl.when(pl.program_id(0) >= nu_ref[0])
    def _():
        y_ref[...] = jnp.zeros_like(y_ref)


def _expert_call(xs, block_e, n_used, w1, b1, w2, b2):
    n_blocks = xs.shape[0] // MOE_TB
    xmap = lambda i, be, nu: (jnp.minimum(i, nu[0] - 1), 0)
    emap3 = lambda i, be, nu: (be[i], 0, 0)
    return pl.pallas_call(
        _expert_kernel,
        grid_spec=pltpu.PrefetchScalarGridSpec(
            num_scalar_prefetch=2, grid=(n_blocks,),
            in_specs=[pl.BlockSpec((MOE_TB, D_MODEL), xmap),
                      pl.BlockSpec((1, D_MODEL, 2 * D_FF), emap3), pl.BlockSpec((1, 1, 2 * D_FF), emap3),
                      pl.BlockSpec((1, D_FF, D_MODEL), emap3), pl.BlockSpec((1, 1, D_MODEL), emap3)],
            out_specs=pl.BlockSpec((MOE_TB, D_MODEL), lambda i, be, nu: (i, 0))),
        out_shape=jax.ShapeDtypeStruct(xs.shape, F32),
        compiler_params=_params(("arbitrary",)), name="moe_experts",
    )(block_e, n_used, xs, w1, b1, w2, b2)


def _combine_kernel(dest_ref, route_ref, x1_ref, ys_ref, g_ref, b_ref, x2_ref, buf_ref, sem, *, tm, alpha):
    def row_copy(t, kk):
        return pltpu.make_async_copy(ys_ref.at[pl.ds(dest_ref[t * TOP_K + kk], 1)], buf_ref.at[kk, pl.ds(t, 1)], sem)

    def issue(t, c):
        for kk in range(TOP_K):
            row_copy(t, kk).start()
        return c

    def drain(t, c):
        for kk in range(TOP_K):
            row_copy(t, kk).wait()
        return c

    lax.fori_loop(0, tm, issue, 0)
    lax.fori_loop(0, tm, drain, 0)
    route = route_ref[...]
    f = jnp.zeros((tm, D_MODEL), F32)
    for kk in range(TOP_K):
        f = f + buf_ref[kk] * route[:, ROUTE_GATE + kk:ROUTE_GATE + kk + 1]
    x2_ref[...] = _layer_norm(alpha * x1_ref[...] + f, g_ref[...], b_ref[...], LN_EPS)


def _combine_call(x1, route, dest, ys, ln_g, ln_b, alpha):
    n = x1.shape[0]
    tm = _pick(n, (256, 128))
    vec = pl.BlockSpec((1, D_MODEL), lambda i: (0, 0))
    return pl.pallas_call(
        functools.partial(_combine_kernel, tm=tm, alpha=alpha), grid=(n // tm,),
        in_specs=[pl.BlockSpec((tm * TOP_K,), lambda i: (i,), memory_space=pltpu.SMEM),
                  pl.BlockSpec((tm, LANES), lambda i: (i, 0)),
                  pl.BlockSpec((tm, D_MODEL), lambda i: (i, 0)),
                  pl.BlockSpec(memory_space=pl.ANY), vec, vec],
        out_specs=pl.BlockSpec((tm, D_MODEL), lambda i: (i, 0)),
        out_shape=jax.ShapeDtypeStruct((n, D_MODEL), F32),
        scratch_shapes=[pltpu.VMEM((TOP_K, tm, D_MODEL), F32), pltpu.SemaphoreType.DMA(())],
        compiler_params=_params(("arbitrary",)), name="moe_combine",
    )(dest, route, x1, ys, ln_g.reshape(1, -1), ln_b.reshape(1, -1))


def _moe_ffn(x1, route, counts, w1, b1, w2, b2, ln_g, ln_b, alpha):
    n = x1.shape[0]
    dest, block_e, n_used, zero_rows, n_blocks = _moe_plan(route, counts, n)
    xs = _dispatch_call(x1, dest, zero_rows, n_blocks)
    ys = _expert_call(xs, block_e, n_used, w1, b1, w2, b2)
    return _combine_call(x1, route, dest, ys, ln_g, ln_b, alpha)


def kernel(x_prompt, x_sample, state_gdn_conv, state_gdn_S, state_cc_conv, state_rwkv_shift, state_rwkv_S, ln_in_g, ln_in_b, w_in, sgu_ln_g, sgu_ln_b, sgu_w, sgu_b, gdn_conv_w, gdn_A_log, gdn_dt_bias, gdn_norm_g, cc_dw_w, cc_dw_b, cc_ln_g, cc_ln_b, rw_mu, rw_w0, rw_w2, rw_a0, rw_a2, rw_g2, rw_k_k, rw_k_a, rw_r_k, rw_ln_g, rw_ln_b, w_out, ln_mix_g, ln_mix_b, router_w, router_b, moe_w1, moe_b1, moe_w2, moe_b2, ln_ffn_g, ln_ffn_b):
    bp, tp, _ = x_prompt.shape
    bs, ts, _ = x_sample.shape
    n_p, n_s = bp * tp, bs * ts
    depth = w_in.shape[0]
    alpha = (2 * depth) ** 0.25
    assert tp % SGU_CHUNK == 0 and SGU_CHUNK % ts == 0

    x = jnp.concatenate([x_prompt.reshape(n_p, D_MODEL), x_sample.reshape(n_s, D_MODEL)], axis=0)
    x = _ln_call(x, ln_in_g, ln_in_b)
    zeros = lambda *s: jnp.zeros(s, F32)
    outs_p, outs_s = [], []
    for l in range(depth):
        p_d, p_bg, p_b, p_a, p_c = _proj_call(x, _reorder_w_in(w_in[l], gdn_A_log.shape[1]))

        w_eff, b_eff = _sgu_weights(sgu_w[l], sgu_b[l], ts)
        y_a, v = _sgu_call(p_a, w_eff, b_eff, sgu_ln_g[l], sgu_ln_b[l], n_p)
        v_p = v[:n_p].reshape(bp, tp, W_MIX)[:, ((tp - 1) // SGU_CHUNK) * SGU_CHUNK:]
        v_s = v[n_p:].reshape(bs, ts, W_MIX)

        gdn_w = (gdn_conv_w[l], gdn_A_log[l], gdn_dt_bias[l], gdn_norm_g[l])
        yb_p, gbuf_p, gs_p = _gdn_call(p_b, p_bg, zeros(bp, GDN_CONV - 1, GDN_QKV),
                                       zeros(bp, N_HEADS, HEAD_DIM, HEAD_DIM), *gdn_w, 0, tp)
        yb_s, gbuf_s, gs_s = _gdn_call(p_b, p_bg, state_gdn_conv[l], state_gdn_S[l], *gdn_w, n_p, ts)

        cc_w = (cc_dw_w[l], cc_dw_b[l], cc_ln_g[l], cc_ln_b[l])
        yc_p, cbuf_p = _cc_call(p_c, zeros(bp, CC_WIDTH - 1, W_MIX), *cc_w, 0, tp)
        yc_s, cbuf_s = _cc_call(p_c, state_cc_conv[l], *cc_w, n_p, ts)

        rw_w = (rw_mu[l], rw_w0[l], rw_a0[l], _rwkv_lora_weights(rw_w2[l], rw_a2[l], rw_g2[l]),
                rw_k_k[l], rw_k_a[l], rw_r_k[l].reshape(-1), rw_ln_g[l], rw_ln_b[l])
        yd_p, rsh_p, rs_p = _rwkv_call(p_d, zeros(bp, COLS_D), zeros(bp, N_HEADS, HEAD_DIM, HEAD_DIM), *rw_w, 0, tp)
        yd_s, rsh_s, rs_s = _rwkv_call(p_d, state_rwkv_shift[l], state_rwkv_S[l], *rw_w, n_p, ts)

        x1, route, counts = _outproj_router_call((y_a, yb_p, yc_p, yd_p), (y_a, yb_s, yc_s, yd_s), x,
                                                 w_out[l].astype(BF16), ln_mix_g[l], ln_mix_b[l],
                                                 router_w[l], router_b[l], alpha)
        x = _moe_ffn(x1, route, counts, _w1_regroup_call(moe_w1[l]), _regroup_bias(moe_b1[l]),
                     moe_w2[l].astype(BF16), moe_b2[l][:, None, :], ln_ffn_g[l], ln_ffn_b[l], alpha)
        outs_p.append((v_p, gbuf_p, gs_p, cbuf_p, rsh_p, rs_p))
        outs_s.append((v_s, gbuf_s, gs_s, cbuf_s, rsh_s, rs_s))

    stack = lambda outs, i: jnp.stack([o[i] for o in outs])
    res = [x[:n_p].reshape(bp, tp, D_MODEL), x[n_p:].reshape(bs, ts, D_MODEL)]
    for i in range(6):
        res += [stack(outs_p, i), stack(outs_s, i)]
    return tuple(res)
```

```python
import functools
import math

import jax
import jax.numpy as jnp
from jax import lax
from jax.experimental import pallas as pl
from jax.experimental.pallas import tpu as pltpu

F32 = jnp.float32
BF16 = jnp.bfloat16
HI = lax.Precision.HIGHEST

D_MODEL = 1024
HEAD_DIM = 64
W_MIX = 256
N_HEADS = W_MIX // HEAD_DIM
SGU_CHUNK = 128
GDN_CONV = 4
GDN_CHUNK = 32
CC_WIDTH = 31
RWKV_CHUNK = 32
SEQ_GROUP = 4
SEQ_TBLK = 256
RWKV_PASSES = 1
GDN_PASSES = 1
LORA_W, LORA_A, LORA_G = 32, 32, 64
COLS_D = 3 * W_MIX + LORA_W + LORA_A + LORA_G
N_EXPERTS = 32
TOP_K = 4
D_FF = D_MODEL
SWIGLU_ALPHA = 1.702
SWIGLU_LIMIT = 7.0
LN_EPS = 1e-5
RMS_EPS = 1e-6
GN_EPS = 64e-5
LANES = 128
SUBLANES = 8
VMEM_LIMIT = 56 * 1024 * 1024
NEG_BIG = -1e30

P_D = COLS_D
P_BG = LANES
P_B = 4 * W_MIX
P_A = 2 * W_MIX
P_C = 2 * W_MIX
P_TOTAL = P_D + P_BG + P_B + P_A + P_C


def _pick(n, cands):
    for c in cands:
        if n % c == 0:
            return c
    raise ValueError(f"no tile in {cands} divides {n}")


def _params(sem):
    return pltpu.CompilerParams(dimension_semantics=sem, vmem_limit_bytes=VMEM_LIMIT)


def _layer_norm(x, g, b, eps):
    xc = x - jnp.mean(x, -1, keepdims=True)
    var = jnp.mean(xc * xc, -1, keepdims=True)
    return xc * lax.rsqrt(var + eps) * g + b


def _sigmoid(x):
    return 1.0 / (1.0 + jnp.exp(-x))


def _silu(x):
    return x * _sigmoid(x)


def _softplus(x):
    return jnp.maximum(x, 0.0) + jnp.log(1.0 + jnp.exp(-jnp.abs(x)))


def _dot(a, b):
    return jnp.dot(a, b, preferred_element_type=F32)


def _dot_hi(a, b):
    return jnp.dot(a, b, precision=HI, preferred_element_type=F32)


def _dot_nt_hi(a, b):
    return lax.dot_general(a, b, (((1,), (1,)), ((), ())), precision=HI, preferred_element_type=F32)


def _head_ones():
    r = lax.broadcasted_iota(jnp.int32, (W_MIX, W_MIX), 0) // HEAD_DIM
    c = lax.broadcasted_iota(jnp.int32, (W_MIX, W_MIX), 1) // HEAD_DIM
    return (r == c).astype(F32)


def _tri(n, strict):
    r = lax.broadcasted_iota(jnp.int32, (n, n), 0)
    c = lax.broadcasted_iota(jnp.int32, (n, n), 1)
    return (r > c) if strict else (r >= c)


def _neumann_inverse(x, n):
    eye = (lax.broadcasted_iota(jnp.int32, (n, n), 0) == lax.broadcasted_iota(jnp.int32, (n, n), 1)).astype(F32)
    acc = eye + x
    p = x
    k = 2
    while k < n:
        p = _dot_hi(p, p)
        acc = acc + _dot_hi(acc, p)
        k *= 2
    return acc


_NN = (((1,), (0,)), ((), ()))
_NT = (((1,), (1,)), ((), ()))
_TN = (((0,), (0,)), ((), ()))


def _split2(x):
    hi = x.astype(BF16)
    return hi, (x - hi.astype(F32)).astype(BF16)


def _split3(x):
    hi = x.astype(BF16)
    r = x - hi.astype(F32)
    mid = r.astype(BF16)
    return hi, mid, (r - mid.astype(F32)).astype(BF16)


def _mm(a, b, dn=_NN, passes=1):
    d = lambda x, y: lax.dot_general(x, y, dn, preferred_element_type=F32)
    if passes == 1:
        return d(a.astype(BF16), b.astype(BF16))
    a_hi, a_lo = _split2(a)
    b_hi, b_lo = _split2(b)
    return d(a_hi, b_hi) + (d(a_lo, b_hi) + d(a_hi, b_lo))


def _mm_exact_rhs(a, sel, dn=_NN):
    d = lambda x: lax.dot_general(x, sel, dn, preferred_element_type=F32)
    hi, mid, lo = _split3(a)
    return d(hi) + (d(mid) + d(lo))


def _mm_exact_lhs(sel, b, dn=_NN):
    d = lambda x: lax.dot_general(sel, x, dn, preferred_element_type=F32)
    hi, mid, lo = _split3(b)
    return d(hi) + (d(mid) + d(lo))


def _block_neumann_inverse(xs, block, passes):
    n = xs[0].shape[0]
    eye = (lax.broadcasted_iota(jnp.int32, (n, n), 0) == lax.broadcasted_iota(jnp.int32, (n, n), 1)).astype(F32)
    accs = [eye + x for x in xs]
    ps = list(xs)
    k = 2
    while k < block:
        ps = [_mm(p, p, passes=passes) for p in ps]
        accs = [acc + _mm(acc, p, passes=passes) for acc, p in zip(accs, ps)]
        k *= 2
    return accs


def _each(fn, *lists):
    return [fn(*args) for args in zip(*lists)]


def _stack_masked(x, c):
    lane_head = lax.broadcasted_iota(jnp.int32, (c, W_MIX), 1) // HEAD_DIM
    return jnp.concatenate([jnp.where(lane_head == h, x, 0.0) for h in range(N_HEADS)], axis=0)


def _stack_heads(x):
    return jnp.concatenate([x[:, h * HEAD_DIM:(h + 1) * HEAD_DIM] for h in range(N_HEADS)], axis=0)


def _unstack_heads(x, c):
    return jnp.concatenate([x[h * c:(h + 1) * c] for h in range(N_HEADS)], axis=1)


def _seq_block(i, tb, *, j, ns, nt, blk0):
    return (blk0 + (i * ns + j) * nt + tb, 0)


def _block_tri(c, strict):
    n = N_HEADS * c
    r = lax.broadcasted_iota(jnp.int32, (n, n), 0)
    q = lax.broadcasted_iota(jnp.int32, (n, n), 1)
    same = (r // c) == (q // c)
    return same & ((r > q) if strict else (r >= q))


def _ln_kernel(x_ref, g_ref, b_ref, o_ref):
    o_ref[...] = _layer_norm(x_ref[...], g_ref[...], b_ref[...], LN_EPS)


def _ln_call(x, g, b):
    n = x.shape[0]
    tm = _pick(n, (1024, 512, 256, 128))
    return pl.pallas_call(
        _ln_kernel, grid=(n // tm,),
        in_specs=[pl.BlockSpec((tm, D_MODEL), lambda i: (i, 0)),
                  pl.BlockSpec((1, D_MODEL), lambda i: (0, 0)),
                  pl.BlockSpec((1, D_MODEL), lambda i: (0, 0))],
        out_specs=pl.BlockSpec((tm, D_MODEL), lambda i: (i, 0)),
        out_shape=jax.ShapeDtypeStruct((n, D_MODEL), F32),
        compiler_params=_params(("parallel",)), name="ln_in",
    )(x, g.reshape(1, -1), b.reshape(1, -1))


def _proj_kernel(x_ref, w_ref, pd_ref, pbg_ref, pb_ref, pa_ref, pc_ref):
    p = _dot(x_ref[...].astype(BF16), w_ref[...])
    o = 0
    for ref, w in ((pd_ref, P_D), (pbg_ref, P_BG), (pb_ref, P_B), (pa_ref, P_A), (pc_ref, P_C)):
        ref[...] = p[:, o:o + w]
        o += w


def _proj_call(x, w_cat):
    n = x.shape[0]
    tm = _pick(n, (512, 256, 128))
    widths = (P_D, P_BG, P_B, P_A, P_C)
    return pl.pallas_call(
        _proj_kernel, grid=(n // tm,),
        in_specs=[pl.BlockSpec((tm, D_MODEL), lambda i: (i, 0)),
                  pl.BlockSpec((D_MODEL, P_TOTAL), lambda i: (0, 0))],
        out_specs=[pl.BlockSpec((tm, w), lambda i: (i, 0)) for w in widths],
        out_shape=[jax.ShapeDtypeStruct((n, w), F32) for w in widths],
        compiler_params=_params(("parallel",)), name="proj_in",
    )(x, w_cat)


def _reorder_w_in(w_in, h_b):
    cols_a = 2 * W_MIX
    cols_b = 3 * W_MIX + 2 * h_b + W_MIX
    o1, o2 = cols_a, cols_a + cols_b
    o3 = o2 + 2 * W_MIX
    wa, wb, wc, wd = w_in[:, :o1], w_in[:, o1:o2], w_in[:, o2:o3], w_in[:, o3:]
    qkv, bg, z = wb[:, :3 * W_MIX], wb[:, 3 * W_MIX:3 * W_MIX + 2 * h_b], wb[:, 3 * W_MIX + 2 * h_b:]
    bg = jnp.pad(bg, ((0, 0), (0, P_BG - 2 * h_b)))
    return jnp.concatenate([wd, bg, qkv, z, wa, wc], axis=1).astype(BF16)


def _sgu_kernel(p_ref, w_ref, b_ref, g_ref, beta_ref, y_ref, v_ref, *, n_chunks):
    w = w_ref[0]
    bias = b_ref[0]
    lane_head = lax.broadcasted_iota(jnp.int32, (SGU_CHUNK, W_MIX), 1) // HEAD_DIM
    for c in range(n_chunks):
        rows = pl.ds(c * SGU_CHUNK, SGU_CHUNK)
        x = p_ref[rows, :]
        h = 0.5 * x * (1.0 + lax.erf(x * (1.0 / math.sqrt(2.0))))
        u = h[:, :W_MIX]
        v = _layer_norm(h[:, W_MIX:], g_ref[...], beta_ref[...], LN_EPS)
        v_ref[rows, :] = v
        vb = jnp.concatenate([jnp.where(lane_head == hh, v, 0.0) for hh in range(N_HEADS)], axis=0)
        s = _dot(w, vb.astype(BF16)) + bias
        y_ref[rows, :] = u * s


def _sgu_weights(sgu_w, sgu_b, t_s):
    causal = jnp.tril(jnp.ones((SGU_CHUNK, SGU_CHUNK), bool))
    wp = jnp.where(causal, sgu_w, 0.0)
    reps = SGU_CHUNK // t_s
    ws = jnp.stack([jnp.kron(jnp.eye(reps, dtype=F32), wp[h, :t_s, :t_s]) for h in range(N_HEADS)])
    cat = lambda w: jnp.concatenate([w[h] for h in range(N_HEADS)], axis=1)
    w_eff = jnp.stack([cat(wp), cat(ws)]).astype(BF16)
    bp = jnp.repeat(sgu_b.T, HEAD_DIM, axis=1)
    bs = jnp.tile(bp[:t_s], (reps, 1))
    return w_eff, jnp.stack([bp, bs])


def _sgu_call(p_a, w_eff, b_eff, ln_g, ln_b, n_prompt_rows):
    n = p_a.shape[0]
    tb = _pick(math.gcd(n_prompt_rows, n - n_prompt_rows), (1024, 512, 256, 128))
    n_prompt_tiles = n_prompt_rows // tb
    grp = lambda i: jnp.minimum(i // n_prompt_tiles, 1)
    return pl.pallas_call(
        functools.partial(_sgu_kernel, n_chunks=tb // SGU_CHUNK), grid=(n // tb,),
        in_specs=[pl.BlockSpec((tb, P_A), lambda i: (i, 0)),
                  pl.BlockSpec((1, SGU_CHUNK, N_HEADS * SGU_CHUNK), lambda i: (grp(i), 0, 0)),
                  pl.BlockSpec((1, SGU_CHUNK, W_MIX), lambda i: (grp(i), 0, 0)),
                  pl.BlockSpec((1, W_MIX), lambda i: (0, 0)),
                  pl.BlockSpec((1, W_MIX), lambda i: (0, 0))],
        out_specs=[pl.BlockSpec((tb, W_MIX), lambda i: (i, 0))] * 2,
        out_shape=[jax.ShapeDtypeStruct((n, W_MIX), F32)] * 2,
        compiler_params=_params(("parallel",)), name="sgu",
    )(p_a, w_eff, b_eff, ln_g.reshape(1, -1), ln_b.reshape(1, -1))


CC_HDR = 32


def _cc_kernel(p_ref, buf_ref, w_ref, wb_ref, g_ref, b_ref, y_ref, nb_ref, xp_ref, *, sb, t, tt):
    hist = CC_WIDTH - 1
    ones = _head_ones()
    w = w_ref[...]
    for s in range(sb):
        x = p_ref[pl.ds(s * t, t), :]
        xp_ref[pl.ds(CC_HDR - hist, hist), :] = buf_ref[s]
        xp_ref[pl.ds(CC_HDR, t), :] = x[:, :W_MIX] * _sigmoid(x[:, W_MIX:])
        nb_ref[s] = xp_ref[pl.ds(t + CC_HDR - hist, hist), :]

        def tile(i, carry):
            base = pl.multiple_of(i * tt, SUBLANES)
            win = xp_ref[pl.ds(base, tt + CC_HDR), :]
            acc = jnp.zeros((tt, W_MIX), F32)
            for j in range(CC_WIDTH):
                o = j + CC_HDR - hist
                acc = acc + win[o:o + tt] * w[j:j + 1]
            hh = acc + wb_ref[...]
            mean = _dot_hi(hh, ones) * (1.0 / HEAD_DIM)
            xc = hh - mean
            var = _dot_hi(xc * xc, ones) * (1.0 / HEAD_DIM)
            yy = xc * lax.rsqrt(var + LN_EPS) * g_ref[...] + b_ref[...]
            y_ref[pl.ds(pl.multiple_of(s * t + base, SUBLANES), tt), :] = _silu(yy)
            return carry

        lax.fori_loop(0, t // tt, tile, 0)


def _cc_call(p_c, buf, w, wb, g, b, row0, t):
    nseq = buf.shape[0]
    sb = 1 if t >= 256 else _pick(nseq, (16, 8, 4, 2, 1))
    tt = min(t, 256)
    rows = sb * t
    blk0 = row0 // rows
    assert row0 % rows == 0 and t % tt == 0
    kern = functools.partial(_cc_kernel, sb=sb, t=t, tt=tt)
    in_specs = [pl.BlockSpec((rows, P_C), lambda i: (blk0 + i, 0)),
                pl.BlockSpec((sb, CC_WIDTH - 1, W_MIX), lambda i: (i, 0, 0)),
                pl.BlockSpec((CC_WIDTH, W_MIX), lambda i: (0, 0)),
                pl.BlockSpec((1, W_MIX), lambda i: (0, 0)),
                pl.BlockSpec((1, W_MIX), lambda i: (0, 0)),
                pl.BlockSpec((1, W_MIX), lambda i: (0, 0))]
    args = [p_c, buf, w, wb.reshape(1, -1), g.reshape(1, -1), b.reshape(1, -1)]
    return pl.pallas_call(
        kern, grid=(nseq // sb,), in_specs=in_specs,
        out_specs=[pl.BlockSpec((rows, W_MIX), lambda i: (i, 0)),
                   pl.BlockSpec((sb, CC_WIDTH - 1, W_MIX), lambda i: (i, 0, 0))],
        out_shape=[jax.ShapeDtypeStruct((nseq * t, W_MIX), F32),
                   jax.ShapeDtypeStruct((nseq, CC_WIDTH - 1, W_MIX), F32)],
        scratch_shapes=[pltpu.VMEM((t + CC_HDR, W_MIX), F32)],
        compiler_params=_params(("arbitrary",)), name="cc",
    )(*args)


GDN_HDR = 8
GDN_QKV = 3 * W_MIX


def _dot_tn_hi(a, b):
    return lax.dot_general(a, b, (((0,), (0,)), ((), ())), precision=HI, preferred_element_type=F32)


def _lane_expand(src_lane0):
    r = lax.broadcasted_iota(jnp.int32, (LANES, W_MIX), 0)
    c = lax.broadcasted_iota(jnp.int32, (LANES, W_MIX), 1) // HEAD_DIM
    return (r == c + src_lane0).astype(F32)


def _gdn_kernel(*refs, ns, tblk, c):
    p_refs, bg_refs = refs[:ns], refs[ns:2 * ns]
    buf_ref, s0_ref, cw_ref, alog_ref, dt_ref, ng_ref, y_ref, nb_ref, s_ref, hdr_ref = refs[2 * ns:]
    hist = GDN_CONV - 1
    hc = N_HEADS * c

    @pl.when(pl.program_id(1) == 0)
    def _():
        nb_ref[...] = buf_ref[...]
        s_ref[...] = s0_ref[...]
        hdr_ref[...] = jnp.zeros_like(hdr_ref)

    ones = _head_ones().astype(BF16)
    e_beta = _lane_expand(0).astype(BF16)
    e_g = _lane_expand(N_HEADS).astype(BF16)
    tri_ones = _tri(c, False).astype(BF16)
    strict_bd = _block_tri(c, True)
    incl_bd = _block_tri(c, False)
    eye_hc = (lax.broadcasted_iota(jnp.int32, (hc, hc), 0) == lax.broadcasted_iota(jnp.int32, (hc, hc), 1))
    eye_w = (lax.broadcasted_iota(jnp.int32, (W_MIX, W_MIX), 0) == lax.broadcasted_iota(jnp.int32, (W_MIX, W_MIX), 1))
    first_lane = (lax.broadcasted_iota(jnp.int32, (W_MIX, hc), 0) % HEAD_DIM == 0).astype(BF16)
    ones_hc = jnp.ones((hc, hc), BF16)
    ones_wv = jnp.ones((W_MIX, HEAD_DIM), BF16)
    cw = cw_ref[...]
    neg_a = -jnp.exp(alog_ref[...])
    mm = functools.partial(_mm, passes=GDN_PASSES)

    def chunk(n, carries):
        rows = pl.ds(pl.multiple_of(n * c, SUBLANES), c)
        sm = functools.partial(_stack_masked, c=c)
        cat = jnp.concatenate
        tails = [cr[0] for cr in carries]
        sts = [cr[1] for cr in carries]
        wins = [cat([tails[j], p_refs[j][rows, :GDN_QKV]], axis=0) for j in range(ns)]

        def conv_act(win):
            conv = jnp.zeros((c, GDN_QKV), F32)
            for tap in range(GDN_CONV):
                o = tap + GDN_HDR - hist
                conv = conv + win[o:o + c] * cw[tap:tap + 1]
            return _silu(conv)

        acts = _each(conv_act, wins)
        qs = [a[:, :W_MIX] for a in acts]
        ks = [a[:, W_MIX:2 * W_MIX] for a in acts]
        vs = [a[:, 2 * W_MIX:] for a in acts]
        sqs = _each(lambda q, k: _mm_exact_rhs(cat([q * q, k * k], axis=0), ones), qs, ks)
        qs = _each(lambda q, sq: q * lax.rsqrt(sq[:c] + 1e-6) * (HEAD_DIM ** -0.5), qs, sqs)
        ks = _each(lambda k, sq: k * lax.rsqrt(sq[c:] + 1e-6), ks, sqs)
        bgs = [bg_refs[j][rows, :] for j in range(ns)]
        betas = _each(lambda bg: _mm_exact_rhs(_sigmoid(bg), e_beta), bgs)
        gsums = _each(lambda bg: _mm_exact_lhs(tri_ones, neg_a * _softplus(bg + dt_ref[...])), bgs)
        gcums = _each(lambda gs: _mm_exact_rhs(gs, e_g), gsums)
        kbs = _each(lambda k, b: k * b, ks, betas)
        vbs = _each(lambda v, b: v * b, vs, betas)
        egcs = _each(jnp.exp, gcums)
        glasts = [gc[c - 1:c, :] for gc in gcums]
        kdecs = _each(lambda k, gl, gc: k * jnp.exp(gl - gc), ks, glasts, gcums)

        gcols = _each(lambda gc: _mm_exact_rhs(sm(gc), first_lane), gcums)
        grows = _each(lambda gcol: _mm_exact_lhs(ones_hc, jnp.where(eye_hc, gcol, 0.0)), gcols)
        decays = _each(lambda gcol, grow: jnp.where(incl_bd, jnp.exp(gcol - grow), 0.0), gcols, grows)
        prods = _each(lambda kb, q, k: mm(cat([sm(kb), sm(q)], axis=0), sm(k), _NT), kbs, qs, ks)
        lms = _each(lambda pr, dec: jnp.where(strict_bd, pr[:hc] * dec, 0.0), prods, decays)
        aqks = _each(lambda pr, dec: pr[hc:] * dec, prods, decays)
        tinvs = _block_neumann_inverse([-lm for lm in lms], c, GDN_PASSES)
        uws = _each(lambda ti, vb, kb, egc: mm(ti, cat([_stack_heads(vb), sm(kb * egc)], axis=1)),
                    tinvs, vbs, kbs, egcs)
        wss = _each(lambda uw, q, egc, st: mm(cat([uw[:, HEAD_DIM:], sm(q * egc)], axis=0), st),
                    uws, qs, egcs, sts)
        vnews = _each(lambda uw, ws: uw[:, :HEAD_DIM] - ws[:hc], uws, wss)
        outs = _each(lambda ws, aqk, vn: _unstack_heads(ws[hc:] + mm(aqk, vn), c), wss, aqks, vnews)
        grs = _each(lambda gl: _mm_exact_rhs(jnp.where(eye_w, jnp.exp(gl), 0.0), ones_wv), glasts)
        sts = _each(lambda st, gr, kd, vn: st * gr + mm(sm(kd), vn, _TN), sts, grs, kdecs, vnews)
        outs = _each(lambda o: o * lax.rsqrt(_mm_exact_rhs(o * o, ones) * (1.0 / HEAD_DIM) + RMS_EPS) * ng_ref[...],
                     outs)
        for j in range(ns):
            y_ref[j, rows, :] = outs[j] * _silu(p_refs[j][rows, GDN_QKV:])
        return tuple((wins[j][c:c + GDN_HDR], sts[j]) for j in range(ns))

    init = []
    for j in range(ns):
        hdr_ref[j, pl.ds(GDN_HDR - hist, hist), :] = nb_ref[j]
        init.append((hdr_ref[j], jnp.concatenate([s_ref[j, h] for h in range(N_HEADS)], axis=0)))
    fin = lax.fori_loop(0, tblk // c, chunk, tuple(init))
    for j in range(ns):
        tail, st = fin[j]
        nb_ref[j] = tail[GDN_HDR - hist:]
        for h in range(N_HEADS):
            s_ref[j, h] = st[h * HEAD_DIM:(h + 1) * HEAD_DIM]


def _gdn_call(p_b, p_bg, buf, s0, conv_w, a_log, dt_bias, norm_g, row0, t):
    nseq = buf.shape[0]
    h_b = a_log.shape[0]
    assert h_b == N_HEADS
    c = math.gcd(t, GDN_CHUNK)
    ns = _pick(nseq, (SEQ_GROUP, 2, 1))
    tblk = min(t, SEQ_TBLK)
    nt = t // tblk
    blk0 = row0 // tblk
    assert row0 % tblk == 0 and t % tblk == 0 and tblk % c == 0 and c % SUBLANES == 0
    lane_pad = lambda x: jnp.pad(x.reshape(1, -1), ((0, 0), (h_b, LANES - 2 * h_b)))
    seq_rows = lambda w: [pl.BlockSpec((tblk, w), functools.partial(_seq_block, j=j, ns=ns, nt=nt, blk0=blk0))
                          for j in range(ns)]
    kern = functools.partial(_gdn_kernel, ns=ns, tblk=tblk, c=c)
    in_specs = seq_rows(P_B) + seq_rows(P_BG) + [
        pl.BlockSpec((ns, GDN_CONV - 1, GDN_QKV), lambda i, tb: (i, 0, 0)),
        pl.BlockSpec((ns, N_HEADS, HEAD_DIM, HEAD_DIM), lambda i, tb: (i, 0, 0, 0)),
        pl.BlockSpec((GDN_CONV, GDN_QKV), lambda i, tb: (0, 0)),
        pl.BlockSpec((1, LANES), lambda i, tb: (0, 0)),
        pl.BlockSpec((1, LANES), lambda i, tb: (0, 0)),
        pl.BlockSpec((1, W_MIX), lambda i, tb: (0, 0))]
    args = [p_b] * ns + [p_bg] * ns + [buf, s0, conv_w, lane_pad(a_log), lane_pad(dt_bias),
                                       jnp.tile(norm_g, N_HEADS).reshape(1, -1)]
    y, nb, st = pl.pallas_call(
        kern, grid=(nseq // ns, nt), in_specs=in_specs,
        out_specs=[pl.BlockSpec((ns, tblk, W_MIX), lambda i, tb: (i, tb, 0)),
                   pl.BlockSpec((ns, GDN_CONV - 1, GDN_QKV), lambda i, tb: (i, 0, 0)),
                   pl.BlockSpec((ns, N_HEADS, HEAD_DIM, HEAD_DIM), lambda i, tb: (i, 0, 0, 0))],
        out_shape=[jax.ShapeDtypeStruct((nseq, t, W_MIX), F32),
                   jax.ShapeDtypeStruct((nseq, GDN_CONV - 1, GDN_QKV), F32),
                   jax.ShapeDtypeStruct((nseq, N_HEADS, HEAD_DIM, HEAD_DIM), F32)],
        scratch_shapes=[pltpu.VMEM((ns, GDN_HDR, GDN_QKV), F32)],
        compiler_params=_params(("arbitrary", "arbitrary")), name="gdn",
    )(*args)
    return y.reshape(nseq * t, W_MIX), nb, st


def _rwkv_kernel(*refs, ns, tblk, c):
    p_refs = refs[:ns]
    (sh_ref, s0_ref, mu_ref, w0_ref, a0_ref, lora_ref, kk_ref, ka_ref, rk_ref, g_ref, b_ref,
     y_ref, sho_ref, s_ref) = refs[ns:]
    hc = N_HEADS * c
    ones = _head_ones().astype(BF16)
    tri_ones = _tri(c, False).astype(BF16)
    strict_bd = _block_tri(c, True)
    incl_bd = _block_tri(c, False)
    lane = lax.broadcasted_iota(jnp.int32, (c, LANES), 1)
    row_id = lax.broadcasted_iota(jnp.int32, (c, COLS_D), 0)
    mm = functools.partial(_mm, passes=RWKV_PASSES)

    @pl.when(pl.program_id(1) == 0)
    def _():
        sho_ref[...] = sh_ref[...]
        s_ref[...] = s0_ref[...]

    def chunk(n, carries):
        rows = pl.ds(pl.multiple_of(n * c, SUBLANES), c)
        sm = functools.partial(_stack_masked, c=c)
        cat = jnp.concatenate
        prev_rows = [cr[0] for cr in carries]
        sts = [cr[1] for cr in carries]
        xs = [p_refs[j][rows, :] for j in range(ns)]
        xls = _each(lambda x, pr: x + (jnp.where(row_id == 0, pr, pltpu.roll(x, 1, 0)) - x) * mu_ref[...],
                    xs, prev_rows)
        rs = [xl[:, :W_MIX] for xl in xls]
        ks = [xl[:, W_MIX:2 * W_MIX] for xl in xls]
        vs = [xl[:, 2 * W_MIX:3 * W_MIX] for xl in xls]

        def lora_act(xl):
            lo = xl[:, 3 * W_MIX:]
            return jnp.where(lane < LORA_W, jnp.tanh(lo), jnp.where(lane < LORA_W + LORA_A, lo, _sigmoid(lo)))

        loras = _each(lambda xl: _dot_hi(lora_act(xl), lora_ref[...]), xls)
        lws = _each(lambda lr: -jnp.exp(-_softplus(-(w0_ref[...] + lr[:, :W_MIX])) - 0.5), loras)
        a_s = _each(lambda lr: _sigmoid(a0_ref[...] + lr[:, W_MIX:2 * W_MIX]), loras)
        gs = [lr[:, 2 * W_MIX:] for lr in loras]
        kkps = _each(lambda k: k * kk_ref[...], ks)
        k2s = _each(lambda k, a: k * (1.0 + (a - 1.0) * ka_ref[...]), ks, a_s)
        sums = _each(lambda kkp, r, k2: _mm_exact_rhs(cat([kkp * kkp, r * k2 * rk_ref[...]], axis=0), ones),
                     kkps, rs, k2s)
        kks = _each(lambda kkp, sm_: kkp * lax.rsqrt(sm_[:c] + 1e-6), kkps, sums)
        bonuses = [sm_[c:] for sm_ in sums]
        cums = _each(lambda lw: _mm_exact_lhs(tri_ones, lw), lws)
        invs = _each(lambda cum: jnp.exp(-cum), cums)
        a_hats = _each(lambda kk, cum, lw: -kk * jnp.exp(cum - lw), kks, cums, lws)
        b_hats = _each(lambda kk, a, inv: kk * a * inv, kks, a_s, invs)
        c_hats = _each(lambda k2, inv: k2 * inv, k2s, invs)
        q_hats = _each(lambda r, cum: r * jnp.exp(cum), rs, cums)
        gam_cs = [jnp.exp(cum[c - 1:c, :]) for cum in cums]

        xaqs = _each(lambda ah, qh: cat([sm(ah), sm(qh)], axis=0), a_hats, q_hats)
        bcss = _each(lambda bh, ch: cat([sm(bh), sm(ch)], axis=0), b_hats, c_hats)
        prods = _each(lambda xaq, bcs: mm(xaq, bcs, _NT), xaqs, bcss)
        a_ms = [jnp.where(strict_bd, pr[:hc, :hc], 0.0) for pr in prods]
        b_ms = [jnp.where(strict_bd, pr[:hc, hc:], 0.0) for pr in prods]
        p_qs = [cat([jnp.where(incl_bd, pr[hc:, :hc], 0.0), jnp.where(incl_bd, pr[hc:, hc:], 0.0)], axis=1)
                for pr in prods]
        tinvs = _block_neumann_inverse(a_ms, c, RWKV_PASSES)
        vss = _each(_stack_heads, vs)
        zos = _each(lambda xaq, st: mm(xaq, st, _NT), xaqs, sts)
        bvs = _each(lambda bm, v_: mm(bm, v_), b_ms, vss)
        zs = _each(lambda ti, zo, bv: mm(ti, zo[:hc] + bv), tinvs, zos, bvs)
        zvs = _each(lambda z, v_: cat([z, v_], axis=0), zs, vss)
        outs = _each(lambda zo, pq, zv: zo[hc:] + mm(pq, zv), zos, p_qs, zvs)
        sts = _each(lambda st, zv, bcs, gc: (st + mm(zv, bcs, _TN)) * gc, sts, zvs, bcss, gam_cs)

        ys = _each(lambda o: _unstack_heads(o, c), outs)
        ycs = _each(lambda y: y - _mm_exact_rhs(y, ones) * (1.0 / HEAD_DIM), ys)
        yns = _each(lambda yc: yc * lax.rsqrt(_mm_exact_rhs(yc * yc, ones) * (1.0 / HEAD_DIM) + GN_EPS)
                    * g_ref[...] + b_ref[...], ycs)
        for j in range(ns):
            y_ref[j, rows, :] = (yns[j] + bonuses[j] * vs[j]) * gs[j]
        return tuple((xs[j][c - 1:c, :], sts[j]) for j in range(ns))

    init = tuple((sho_ref[j], jnp.concatenate([s_ref[j, h] for h in range(N_HEADS)], axis=1)) for j in range(ns))
    fin = lax.fori_loop(0, tblk // c, chunk, init)
    for j in range(ns):
        last_row, st = fin[j]
        sho_ref[j] = last_row
        for h in range(N_HEADS):
            s_ref[j, h] = st[:, h * HEAD_DIM:(h + 1) * HEAD_DIM]


def _rwkv_lora_weights(w2, a2, g2):
    m = jnp.zeros((LANES, 3 * W_MIX), F32)
    m = m.at[:LORA_W, :W_MIX].set(w2)
    m = m.at[LORA_W:LORA_W + LORA_A, W_MIX:2 * W_MIX].set(a2)
    return m.at[LORA_W + LORA_A:, 2 * W_MIX:].set(g2)


def _rwkv_call(p_d, shift, s0, mu, w0, a0, lora_w, k_k, k_a, r_k, ln_g, ln_b, row0, t):
    nseq = shift.shape[0]
    c = math.gcd(t, RWKV_CHUNK)
    ns = _pick(nseq, (SEQ_GROUP, 2, 1))
    tblk = min(t, SEQ_TBLK)
    nt = t // tblk
    blk0 = row0 // tblk
    assert row0 % tblk == 0 and t % tblk == 0 and tblk % c == 0 and c % SUBLANES == 0
    row = lambda x: x.reshape(1, -1)
    vec = lambda w: pl.BlockSpec((1, w), lambda i, tb: (0, 0))
    kern = functools.partial(_rwkv_kernel, ns=ns, tblk=tblk, c=c)
    in_specs = [pl.BlockSpec((tblk, P_D), functools.partial(_seq_block, j=j, ns=ns, nt=nt, blk0=blk0))
                for j in range(ns)] + [
        pl.BlockSpec((ns, 1, COLS_D), lambda i, tb: (i, 0, 0)),
        pl.BlockSpec((ns, N_HEADS, HEAD_DIM, HEAD_DIM), lambda i, tb: (i, 0, 0, 0)),
        vec(COLS_D), vec(W_MIX), vec(W_MIX),
        pl.BlockSpec((LANES, 3 * W_MIX), lambda i, tb: (0, 0)),
        vec(W_MIX), vec(W_MIX), vec(W_MIX), vec(W_MIX), vec(W_MIX)]
    args = [p_d] * ns + [shift.reshape(nseq, 1, COLS_D), s0, row(mu), row(w0), row(a0), lora_w,
                         row(k_k), row(k_a), row(r_k), row(ln_g), row(ln_b)]
    y, sh, st = pl.pallas_call(
        kern, grid=(nseq // ns, nt), in_specs=in_specs,
        out_specs=[pl.BlockSpec((ns, tblk, W_MIX), lambda i, tb: (i, tb, 0)),
                   pl.BlockSpec((ns, 1, COLS_D), lambda i, tb: (i, 0, 0)),
                   pl.BlockSpec((ns, N_HEADS, HEAD_DIM, HEAD_DIM), lambda i, tb: (i, 0, 0, 0))],
        out_shape=[jax.ShapeDtypeStruct((nseq, t, W_MIX), F32),
                   jax.ShapeDtypeStruct((nseq, 1, COLS_D), F32),
                   jax.ShapeDtypeStruct((nseq, N_HEADS, HEAD_DIM, HEAD_DIM), F32)],
        compiler_params=_params(("arbitrary", "arbitrary")), name="rwkv",
    )(*args)
    return y.reshape(nseq * t, W_MIX), sh.reshape(nseq, COLS_D), st


ROUTE_IDX, ROUTE_GATE, ROUTE_RANK = 0, TOP_K, 2 * TOP_K


def _outproj_router_kernel(*refs, tm, alpha, n_prompt_tiles):
    yp_refs, ys_refs = refs[0:4], refs[4:8]
    x_ref, wo_ref, g_ref, b_ref, rw_ref, rb_ref, x1_ref, route_ref, cnt_ref = refs[8:]

    @pl.when(pl.program_id(0) == 0)
    def _():
        cnt_ref[...] = jnp.zeros_like(cnt_ref)

    is_prompt = pl.program_id(0) < n_prompt_tiles
    mix = jnp.zeros((tm, D_MODEL), F32)
    for i in range(4):
        y = jnp.where(is_prompt, yp_refs[i][...], ys_refs[i][...])
        mix = mix + _dot(y.astype(BF16), wo_ref[pl.ds(i * W_MIX, W_MIX), :])
    x1 = _layer_norm(alpha * x_ref[...] + mix, g_ref[...], b_ref[...], LN_EPS)
    x1_ref[...] = x1

    logits = _dot_hi(x1, rw_ref[...]) + rb_ref[...]
    lane = lax.broadcasted_iota(jnp.int32, (tm, LANES), 1)
    work = logits
    vals, hots = [], []
    for _ in range(TOP_K):
        m = jnp.max(work, axis=-1, keepdims=True)
        idx = jnp.min(jnp.where(work == m, lane, LANES), axis=-1, keepdims=True)
        hot = lane == idx
        vals.append(m)
        hots.append(hot)
        work = jnp.where(hot, -jnp.inf, work)
    exps = [jnp.exp(v - vals[0]) for v in vals]
    denom = exps[0] + exps[1] + exps[2] + exps[3]

    any_hot = jnp.zeros((tm, LANES), F32)
    for hot in hots:
        any_hot = any_hot + hot.astype(F32)
    before = _dot(_tri(tm, True).astype(BF16), any_hot.astype(BF16)) + cnt_ref[...]
    cnt_ref[...] = cnt_ref[...] + jnp.sum(any_hot, axis=0, keepdims=True)

    route = jnp.zeros((tm, LANES), F32)
    for kk in range(TOP_K):
        e_id = jnp.sum(jnp.where(hots[kk], lane, 0), axis=-1, keepdims=True).astype(F32)
        rank = jnp.sum(jnp.where(hots[kk], before, 0.0), axis=-1, keepdims=True)
        route = jnp.where(lane == ROUTE_IDX + kk, e_id, route)
        route = jnp.where(lane == ROUTE_GATE + kk, exps[kk] / denom, route)
        route = jnp.where(lane == ROUTE_RANK + kk, rank, route)
    route_ref[...] = route


def _outproj_router_call(ys_prompt, ys_sample, x, w_out, ln_g, ln_b, router_w, router_b, alpha):
    n = x.shape[0]
    n_p, n_s = ys_prompt[-1].shape[0], ys_sample[-1].shape[0]
    tm = _pick(math.gcd(n_p, n_s), (256, 128))
    npt = n_p // tm
    pmap = lambda y: (lambda i: (i, 0)) if y.shape[0] == n else (lambda i: (jnp.minimum(i, npt - 1), 0))
    smap = lambda y: (lambda i: (i, 0)) if y.shape[0] == n else (lambda i: (jnp.maximum(i - npt, 0), 0))
    row = lambda v: v.reshape(1, -1)
    vec = lambda w: pl.BlockSpec((1, w), lambda i: (0, 0))
    rw = jnp.pad(router_w, ((0, 0), (0, LANES - N_EXPERTS)))
    rb = jnp.pad(router_b, (0, LANES - N_EXPERTS), constant_values=NEG_BIG)
    return pl.pallas_call(
        functools.partial(_outproj_router_kernel, tm=tm, alpha=alpha, n_prompt_tiles=npt), grid=(n // tm,),
        in_specs=[pl.BlockSpec((tm, W_MIX), pmap(y)) for y in ys_prompt] + [
            pl.BlockSpec((tm, W_MIX), smap(y)) for y in ys_sample] + [
            pl.BlockSpec((tm, D_MODEL), lambda i: (i, 0)),
            pl.BlockSpec((D_MODEL, D_MODEL), lambda i: (0, 0)),
            vec(D_MODEL), vec(D_MODEL),
            pl.BlockSpec((D_MODEL, LANES), lambda i: (0, 0)), vec(LANES)],
        out_specs=[pl.BlockSpec((tm, D_MODEL), lambda i: (i, 0)),
                   pl.BlockSpec((tm, LANES), lambda i: (i, 0)),
                   pl.BlockSpec((1, LANES), lambda i: (0, 0))],
        out_shape=[jax.ShapeDtypeStruct((n, D_MODEL), F32),
                   jax.ShapeDtypeStruct((n, LANES), F32),
                   jax.ShapeDtypeStruct((1, LANES), F32)],
        compiler_params=_params(("arbitrary",)), name="outproj_router",
    )(*ys_prompt, *ys_sample, x, w_out, row(ln_g), row(ln_b), rw, row(rb))


MOE_TB = 256


def _moe_plan(route, counts, n):
    e_idx = route[:, ROUTE_IDX:ROUTE_IDX + TOP_K].astype(jnp.int32)
    rank = route[:, ROUTE_RANK:ROUTE_RANK + TOP_K].astype(jnp.int32)
    cnt = counts[0, :N_EXPERTS].astype(jnp.int32)
    padded = (cnt + MOE_TB - 1) // MOE_TB * MOE_TB
    pad_end = jnp.cumsum(padded)
    pad_start = pad_end - padded
    dest = (pad_start[e_idx] + rank).reshape(n * TOP_K)
    n_blocks = -(-n * TOP_K // MOE_TB) + N_EXPERTS
    n_used = pad_end[-1] // MOE_TB
    blk = jnp.minimum(jnp.arange(n_blocks), n_used - 1) * MOE_TB
    block_e = jnp.minimum(jnp.sum(pad_end[None, :] <= blk[:, None], axis=1), N_EXPERTS - 1).astype(jnp.int32)
    last_block_row = jnp.where(padded > 0, pad_end - MOE_TB, -1)
    tail = n_used + jnp.arange(N_EXPERTS)
    tail_row = jnp.where(tail < n_blocks, tail * MOE_TB, -1)
    zero_rows = jnp.concatenate([last_block_row, tail_row]).astype(jnp.int32)
    return dest, block_e, n_used.reshape(1).astype(jnp.int32), zero_rows, n_blocks


def _dispatch_kernel(zrow_ref, dest_ref, x_ref, xs_ref, zbuf_ref, xbuf_ref, zsem, sems, *, tm, n_tiles):
    step = pl.program_id(0)
    slot = step % 2

    def zero_copy(e):
        row = pl.multiple_of(jnp.maximum(zrow_ref[e], 0), MOE_TB)
        return pltpu.make_async_copy(zbuf_ref, xs_ref.at[pl.ds(row, MOE_TB)], zsem)

    @pl.when(pl.program_id(0) == 0)
    def _():
        zbuf_ref[...] = jnp.zeros_like(zbuf_ref)
        for e in range(2 * N_EXPERTS):
            @pl.when(zrow_ref[e] >= 0)
            def _():
                zero_copy(e).start()
        for e in range(2 * N_EXPERTS):
            @pl.when(zrow_ref[e] >= 0)
            def _():
                zero_copy(e).wait()

    def row_copy(s, t, dst_row):
        return pltpu.make_async_copy(xbuf_ref.at[s, pl.ds(t, 1)], xs_ref.at[pl.ds(dst_row, 1)], sems.at[s])

    def issue(t, c):
        for kk in range(TOP_K):
            row_copy(slot, t, dest_ref[t * TOP_K + kk]).start(priority=kk % 2)
        return c

    def drain(s):
        def body(t, c):
            for kk in range(TOP_K):
                row_copy(s, 0, 0).wait()
            return c
        lax.fori_loop(0, tm, body, 0)

    xbuf_ref[slot] = x_ref[...]
    lax.fori_loop(0, tm, issue, 0)

    @pl.when(step > 0)
    def _():
        drain(1 - slot)

    @pl.when(step == n_tiles - 1)
    def _():
        drain(slot)


def _dispatch_call(x1, dest, last_block_row, n_blocks):
    n = x1.shape[0]
    tm = _pick(n, (256, 128))
    return pl.pallas_call(
        functools.partial(_dispatch_kernel, tm=tm, n_tiles=n // tm),
        grid_spec=pltpu.PrefetchScalarGridSpec(
            num_scalar_prefetch=1, grid=(n // tm,),
            in_specs=[pl.BlockSpec((tm * TOP_K,), lambda i, z: (i,), memory_space=pltpu.SMEM),
                      pl.BlockSpec((tm, D_MODEL), lambda i, z: (i, 0))],
            out_specs=pl.BlockSpec(memory_space=pl.ANY),
            scratch_shapes=[pltpu.VMEM((MOE_TB, D_MODEL), F32), pltpu.VMEM((2, tm, D_MODEL), F32),
                            pltpu.SemaphoreType.DMA(()), pltpu.SemaphoreType.DMA((2,))]),
        out_shape=jax.ShapeDtypeStruct((n_blocks * MOE_TB, D_MODEL), F32),
        compiler_params=_params(("arbitrary",)), name="moe_dispatch",
    )(last_block_row, dest, x1)


PAIR_GROUP = 2 * LANES


def _w1_regroup_kernel(w_ref, o_ref):
    r = lax.broadcasted_iota(jnp.int32, (PAIR_GROUP, PAIR_GROUP), 0)
    c = lax.broadcasted_iota(jnp.int32, (PAIR_GROUP, PAIR_GROUP), 1)
    perm = (r == jnp.where(c < LANES, 2 * c, 2 * (c - LANES) + 1)).astype(BF16)
    for g in range(w_ref.shape[1] // PAIR_GROUP):
        cols = pl.ds(g * PAIR_GROUP, PAIR_GROUP)
        o_ref[:, cols] = _dot(w_ref[:, cols].astype(BF16), perm).astype(BF16)


def _w1_regroup_call(w1):
    e, d, f2 = w1.shape
    rows = e * d
    tr = _pick(rows, (512, 256, 128))
    out = pl.pallas_call(
        _w1_regroup_kernel, grid=(rows // tr,),
        in_specs=[pl.BlockSpec((tr, f2), lambda i: (i, 0))],
        out_specs=pl.BlockSpec((tr, f2), lambda i: (i, 0)),
        out_shape=jax.ShapeDtypeStruct((rows, f2), BF16),
        compiler_params=_params(("parallel",)), name="w1_regroup",
    )(w1.reshape(rows, f2))
    return out.reshape(e, d, f2)


def _regroup_bias(b1):
    e, f2 = b1.shape
    return b1.reshape(e, f2 // PAIR_GROUP, LANES, 2).swapaxes(2, 3).reshape(e, 1, f2)


def _expert_kernel(be_ref, nu_ref, x_ref, w1_ref, b1_ref, w2_ref, b2_ref, y_ref):
    @pl.when(pl.program_id(0) < nu_ref[0])
    def _():
        h = _dot(x_ref[...].astype(BF16), w1_ref[0]) + b1_ref[0]
        acts = []
        for g in range(h.shape[1] // PAIR_GROUP):
            hg = jnp.minimum(h[:, g * PAIR_GROUP:g * PAIR_GROUP + LANES], SWIGLU_LIMIT)
            hl = jnp.clip(h[:, g * PAIR_GROUP + LANES:(g + 1) * PAIR_GROUP], -SWIGLU_LIMIT, SWIGLU_LIMIT)
            acts.append((hg * _sigmoid(SWIGLU_ALPHA * hg) * (hl + 1.0)).astype(BF16))
        y_ref[...] = _dot(jnp.concatenate(acts, axis=1), w2_ref[0]) + b2_ref[0]

    @pl.when(pl.program_id(0) >= nu_ref[0])
    def _():
        y_ref[...] = jnp.zeros_like(y_ref)


def _expert_call(xs, block_e, n_used, w1, b1, w2, b2):
    n_blocks = xs.shape[0] // MOE_TB
    xmap = lambda i, be, nu: (jnp.minimum(i, nu[0] - 1), 0)
    emap3 = lambda i, be, nu: (be[i], 0, 0)
    return pl.pallas_call(
        _expert_kernel,
        grid_spec=pltpu.PrefetchScalarGridSpec(
            num_scalar_prefetch=2, grid=(n_blocks,),
            in_specs=[pl.BlockSpec((MOE_TB, D_MODEL), xmap),
                      pl.BlockSpec((1, D_MODEL, 2 * D_FF), emap3), pl.BlockSpec((1, 1, 2 * D_FF), emap3),
                      pl.BlockSpec((1, D_FF, D_MODEL), emap3), pl.BlockSpec((1, 1, D_MODEL), emap3)],
            out_specs=pl.BlockSpec((MOE_TB, D_MODEL), lambda i, be, nu: (i, 0))),
        out_shape=jax.ShapeDtypeStruct(xs.shape, F32),
        compiler_params=_params(("arbitrary",)), name="moe_experts",
    )(block_e, n_used, xs, w1, b1, w2, b2)


def _combine_kernel(dest_ref, dnext_ref, route_ref, x1_ref, ys_ref, g_ref, b_ref, x2_ref, buf_ref, sems,
                    *, tm, alpha, n_tiles):
    step = pl.program_id(0)
    slot = step % 2

    def row_copy(s, t, kk, src_row):
        return pltpu.make_async_copy(ys_ref.at[pl.ds(src_row, 1)], buf_ref.at[s, kk, pl.ds(t, 1)], sems.at[s])

    def issue(dref, s):
        def body(t, c):
            for kk in range(TOP_K):
                row_copy(s, t, kk, dref[t * TOP_K + kk]).start(priority=kk % 2)
            return c
        lax.fori_loop(0, tm, body, 0)

    @pl.when(step == 0)
    def _():
        issue(dest_ref, 0)

    @pl.when(step + 1 < n_tiles)
    def _():
        issue(dnext_ref, 1 - slot)

    def drain(t, c):
        for kk in range(TOP_K):
            row_copy(slot, 0, kk, 0).wait()
        return c

    lax.fori_loop(0, tm, drain, 0)
    route = route_ref[...]
    f = jnp.zeros((tm, D_MODEL), F32)
    for kk in range(TOP_K):
        f = f + buf_ref[slot, kk] * route[:, ROUTE_GATE + kk:ROUTE_GATE + kk + 1]
    x2_ref[...] = _layer_norm(alpha * x1_ref[...] + f, g_ref[...], b_ref[...], LN_EPS)


def _combine_call(x1, route, dest, ys, ln_g, ln_b, alpha):
    n = x1.shape[0]
    tm = _pick(n, (256, 128))
    n_tiles = n // tm
    vec = pl.BlockSpec((1, D_MODEL), lambda i: (0, 0))
    return pl.pallas_call(
        functools.partial(_combine_kernel, tm=tm, alpha=alpha, n_tiles=n_tiles), grid=(n_tiles,),
        in_specs=[pl.BlockSpec((tm * TOP_K,), lambda i: (i,), memory_space=pltpu.SMEM),
                  pl.BlockSpec((tm * TOP_K,), lambda i: (jnp.minimum(i + 1, n_tiles - 1),), memory_space=pltpu.SMEM),
                  pl.BlockSpec((tm, LANES), lambda i: (i, 0)),
                  pl.BlockSpec((tm, D_MODEL), lambda i: (i, 0)),
                  pl.BlockSpec(memory_space=pl.ANY), vec, vec],
        out_specs=pl.BlockSpec((tm, D_MODEL), lambda i: (i, 0)),
        out_shape=jax.ShapeDtypeStruct((n, D_MODEL), F32),
        scratch_shapes=[pltpu.VMEM((2, TOP_K, tm, D_MODEL), F32), pltpu.SemaphoreType.DMA((2,))],
        compiler_params=_params(("arbitrary",)), name="moe_combine",
    )(dest, dest, route, x1, ys, ln_g.reshape(1, -1), ln_b.reshape(1, -1))


def _moe_ffn(x1, route, counts, w1, b1, w2, b2, ln_g, ln_b, alpha):
    n = x1.shape[0]
    dest, block_e, n_used, zero_rows, n_blocks = _moe_plan(route, counts, n)
    xs = _dispatch_call(x1, dest, zero_rows, n_blocks)
    ys = _expert_call(xs, block_e, n_used, w1, b1, w2, b2)
    return _combine_call(x1, route, dest, ys, ln_g, ln_b, alpha)


def kernel(x_prompt, x_sample, state_gdn_conv, state_gdn_S, state_cc_conv, state_rwkv_shift, state_rwkv_S, ln_in_g, ln_in_b, w_in, sgu_ln_g, sgu_ln_b, sgu_w, sgu_b, gdn_conv_w, gdn_A_log, gdn_dt_bias, gdn_norm_g, cc_dw_w, cc_dw_b, cc_ln_g, cc_ln_b, rw_mu, rw_w0, rw_w2, rw_a0, rw_a2, rw_g2, rw_k_k, rw_k_a, rw_r_k, rw_ln_g, rw_ln_b, w_out, ln_mix_g, ln_mix_b, router_w, router_b, moe_w1, moe_b1, moe_w2, moe_b2, ln_ffn_g, ln_ffn_b):
    bp, tp, _ = x_prompt.shape
    bs, ts, _ = x_sample.shape
    n_p, n_s = bp * tp, bs * ts
    depth = w_in.shape[0]
    alpha = (2 * depth) ** 0.25
    assert tp % SGU_CHUNK == 0 and SGU_CHUNK % ts == 0

    x = jnp.concatenate([x_prompt.reshape(n_p, D_MODEL), x_sample.reshape(n_s, D_MODEL)], axis=0)
    x = _ln_call(x, ln_in_g, ln_in_b)
    zeros = lambda *s: jnp.zeros(s, F32)
    outs_p, outs_s = [], []
    for l in range(depth):
        p_d, p_bg, p_b, p_a, p_c = _proj_call(x, _reorder_w_in(w_in[l], gdn_A_log.shape[1]))

        w_eff, b_eff = _sgu_weights(sgu_w[l], sgu_b[l], ts)
        y_a, v = _sgu_call(p_a, w_eff, b_eff, sgu_ln_g[l], sgu_ln_b[l], n_p)
        v_p = v[:n_p].reshape(bp, tp, W_MIX)[:, ((tp - 1) // SGU_CHUNK) * SGU_CHUNK:]
        v_s = v[n_p:].reshape(bs, ts, W_MIX)

        gdn_w = (gdn_conv_w[l], gdn_A_log[l], gdn_dt_bias[l], gdn_norm_g[l])
        yb_p, gbuf_p, gs_p = _gdn_call(p_b, p_bg, zeros(bp, GDN_CONV - 1, GDN_QKV),
                                       zeros(bp, N_HEADS, HEAD_DIM, HEAD_DIM), *gdn_w, 0, tp)
        yb_s, gbuf_s, gs_s = _gdn_call(p_b, p_bg, state_gdn_conv[l], state_gdn_S[l], *gdn_w, n_p, ts)

        cc_w = (cc_dw_w[l], cc_dw_b[l], cc_ln_g[l], cc_ln_b[l])
        yc_p, cbuf_p = _cc_call(p_c, zeros(bp, CC_WIDTH - 1, W_MIX), *cc_w, 0, tp)
        yc_s, cbuf_s = _cc_call(p_c, state_cc_conv[l], *cc_w, n_p, ts)

        rw_w = (rw_mu[l], rw_w0[l], rw_a0[l], _rwkv_lora_weights(rw_w2[l], rw_a2[l], rw_g2[l]),
                rw_k_k[l], rw_k_a[l], rw_r_k[l].reshape(-1), rw_ln_g[l], rw_ln_b[l])
        yd_p, rsh_p, rs_p = _rwkv_call(p_d, zeros(bp, COLS_D), zeros(bp, N_HEADS, HEAD_DIM, HEAD_DIM), *rw_w, 0, tp)
        yd_s, rsh_s, rs_s = _rwkv_call(p_d, state_rwkv_shift[l], state_rwkv_S[l], *rw_w, n_p, ts)

        x1, route, counts = _outproj_router_call((y_a, yb_p, yc_p, yd_p), (y_a, yb_s, yc_s, yd_s), x,
                                                 w_out[l].astype(BF16), ln_mix_g[l], ln_mix_b[l],
                                                 router_w[l], router_b[l], alpha)
        x = _moe_ffn(x1, route, counts, _w1_regroup_call(moe_w1[l]), _regroup_bias(moe_b1[l]),
                     moe_w2[l].astype(BF16), moe_b2[l][:, None, :], ln_ffn_g[l], ln_ffn_b[l], alpha)
        outs_p.append((v_p, gbuf_p, gs_p, cbuf_p, rsh_p, rs_p))
        outs_s.append((v_s, gbuf_s, gs_s, cbuf_s, rsh_s, rs_s))

    stack = lambda outs, i: jnp.stack([o[i] for o in outs])
    res = [x[:n_p].reshape(bp, tp, D_MODEL), x[n_p:].reshape(bs, ts, D_MODEL)]
    for i in range(6):
        res += [stack(outs_p, i), stack(outs_s, i)]
    return tuple(res)
```

```python
import functools
import math

import jax
import jax.numpy as jnp
from jax import lax
from jax.experimental import pallas as pl
from jax.experimental.pallas import tpu as pltpu

F32 = jnp.float32
BF16 = jnp.bfloat16
HI = lax.Precision.HIGHEST

D_MODEL = 1024
HEAD_DIM = 64
W_MIX = 256
N_HEADS = W_MIX // HEAD_DIM
SGU_CHUNK = 128
GDN_CONV = 4
GDN_CHUNK = 32
CC_WIDTH = 31
RWKV_CHUNK = 32
SEQ_GROUP = 4
SEQ_TBLK = 256
RWKV_PASSES = 1
GDN_PASSES = 1
LORA_W, LORA_A, LORA_G = 32, 32, 64
COLS_D = 3 * W_MIX + LORA_W + LORA_A + LORA_G
N_EXPERTS = 32
TOP_K = 4
D_FF = D_MODEL
SWIGLU_ALPHA = 1.702
SWIGLU_LIMIT = 7.0
LN_EPS = 1e-5
RMS_EPS = 1e-6
GN_EPS = 64e-5
LANES = 128
SUBLANES = 8
VMEM_LIMIT = 56 * 1024 * 1024
NEG_BIG = -1e30

P_D = COLS_D
P_BG = LANES
P_B = 4 * W_MIX
P_A = 2 * W_MIX
P_C = 2 * W_MIX
P_TOTAL = P_D + P_BG + P_B + P_A + P_C


def _pick(n, cands):
    for c in cands:
        if n % c == 0:
            return c
    raise ValueError(f"no tile in {cands} divides {n}")


def _params(sem):
    return pltpu.CompilerParams(dimension_semantics=sem, vmem_limit_bytes=VMEM_LIMIT)


def _layer_norm(x, g, b, eps):
    xc = x - jnp.mean(x, -1, keepdims=True)
    var = jnp.mean(xc * xc, -1, keepdims=True)
    return xc * lax.rsqrt(var + eps) * g + b


def _sigmoid(x):
    return 1.0 / (1.0 + jnp.exp(-x))


def _silu(x):
    return x * _sigmoid(x)


def _softplus(x):
    return jnp.maximum(x, 0.0) + jnp.log(1.0 + jnp.exp(-jnp.abs(x)))


def _dot(a, b):
    return jnp.dot(a, b, preferred_element_type=F32)


def _dot_hi(a, b):
    return jnp.dot(a, b, precision=HI, preferred_element_type=F32)


def _dot_nt_hi(a, b):
    return lax.dot_general(a, b, (((1,), (1,)), ((), ())), precision=HI, preferred_element_type=F32)


def _head_ones():
    r = lax.broadcasted_iota(jnp.int32, (W_MIX, W_MIX), 0) // HEAD_DIM
    c = lax.broadcasted_iota(jnp.int32, (W_MIX, W_MIX), 1) // HEAD_DIM
    return (r == c).astype(F32)


def _tri(n, strict):
    r = lax.broadcasted_iota(jnp.int32, (n, n), 0)
    c = lax.broadcasted_iota(jnp.int32, (n, n), 1)
    return (r > c) if strict else (r >= c)


def _neumann_inverse(x, n):
    eye = (lax.broadcasted_iota(jnp.int32, (n, n), 0) == lax.broadcasted_iota(jnp.int32, (n, n), 1)).astype(F32)
    acc = eye + x
    p = x
    k = 2
    while k < n:
        p = _dot_hi(p, p)
        acc = acc + _dot_hi(acc, p)
        k *= 2
    return acc


_NN = (((1,), (0,)), ((), ()))
_NT = (((1,), (1,)), ((), ()))
_TN = (((0,), (0,)), ((), ()))


def _split2(x):
    hi = x.astype(BF16)
    return hi, (x - hi.astype(F32)).astype(BF16)


def _split3(x):
    hi = x.astype(BF16)
    r = x - hi.astype(F32)
    mid = r.astype(BF16)
    return hi, mid, (r - mid.astype(F32)).astype(BF16)


def _mm(a, b, dn=_NN, passes=1):
    d = lambda x, y: lax.dot_general(x, y, dn, preferred_element_type=F32)
    if passes == 1:
        return d(a.astype(BF16), b.astype(BF16))
    a_hi, a_lo = _split2(a)
    b_hi, b_lo = _split2(b)
    return d(a_hi, b_hi) + (d(a_lo, b_hi) + d(a_hi, b_lo))


def _mm_exact_rhs(a, sel, dn=_NN):
    d = lambda x: lax.dot_general(x, sel, dn, preferred_element_type=F32)
    hi, mid, lo = _split3(a)
    return d(hi) + (d(mid) + d(lo))


def _mm_exact_lhs(sel, b, dn=_NN):
    d = lambda x: lax.dot_general(sel, x, dn, preferred_element_type=F32)
    hi, mid, lo = _split3(b)
    return d(hi) + (d(mid) + d(lo))


def _block_neumann_inverse(xs, block, passes):
    n = xs[0].shape[0]
    eye = (lax.broadcasted_iota(jnp.int32, (n, n), 0) == lax.broadcasted_iota(jnp.int32, (n, n), 1)).astype(F32)
    accs = [eye + x for x in xs]
    ps = list(xs)
    k = 2
    while k < block:
        ps = [_mm(p, p, passes=passes) for p in ps]
        accs = [acc + _mm(acc, p, passes=passes) for acc, p in zip(accs, ps)]
        k *= 2
    return accs


def _each(fn, *lists):
    return [fn(*args) for args in zip(*lists)]


def _stack_masked(x, c):
    lane_head = lax.broadcasted_iota(jnp.int32, (c, W_MIX), 1) // HEAD_DIM
    return jnp.concatenate([jnp.where(lane_head == h, x, 0.0) for h in range(N_HEADS)], axis=0)


def _stack_heads(x):
    return jnp.concatenate([x[:, h * HEAD_DIM:(h + 1) * HEAD_DIM] for h in range(N_HEADS)], axis=0)


def _unstack_heads(x, c):
    return jnp.concatenate([x[h * c:(h + 1) * c] for h in range(N_HEADS)], axis=1)


def _seq_block(i, tb, *, j, ns, nt, blk0):
    return (blk0 + (i * ns + j) * nt + tb, 0)


def _block_tri(c, strict):
    n = N_HEADS * c
    r = lax.broadcasted_iota(jnp.int32, (n, n), 0)
    q = lax.broadcasted_iota(jnp.int32, (n, n), 1)
    same = (r // c) == (q // c)
    return same & ((r > q) if strict else (r >= q))


def _ln_kernel(x_ref, g_ref, b_ref, o_ref):
    o_ref[...] = _layer_norm(x_ref[...], g_ref[...], b_ref[...], LN_EPS)


def _ln_call(x, g, b):
    n = x.shape[0]
    tm = _pick(n, (1024, 512, 256, 128))
    return pl.pallas_call(
        _ln_kernel, grid=(n // tm,),
        in_specs=[pl.BlockSpec((tm, D_MODEL), lambda i: (i, 0)),
                  pl.BlockSpec((1, D_MODEL), lambda i: (0, 0)),
                  pl.BlockSpec((1, D_MODEL), lambda i: (0, 0))],
        out_specs=pl.BlockSpec((tm, D_MODEL), lambda i: (i, 0)),
        out_shape=jax.ShapeDtypeStruct((n, D_MODEL), F32),
        compiler_params=_params(("parallel",)), name="ln_in",
    )(x, g.reshape(1, -1), b.reshape(1, -1))


def _proj_kernel(x_ref, w_ref, pd_ref, pbg_ref, pb_ref, pa_ref, pc_ref):
    p = _dot(x_ref[...].astype(BF16), w_ref[...])
    o = 0
    for ref, w in ((pd_ref, P_D), (pbg_ref, P_BG), (pb_ref, P_B), (pa_ref, P_A), (pc_ref, P_C)):
        ref[...] = p[:, o:o + w]
        o += w


def _proj_call(x, w_cat):
    n = x.shape[0]
    tm = _pick(n, (512, 256, 128))
    widths = (P_D, P_BG, P_B, P_A, P_C)
    return pl.pallas_call(
        _proj_kernel, grid=(n // tm,),
        in_specs=[pl.BlockSpec((tm, D_MODEL), lambda i: (i, 0)),
                  pl.BlockSpec((D_MODEL, P_TOTAL), lambda i: (0, 0))],
        out_specs=[pl.BlockSpec((tm, w), lambda i: (i, 0)) for w in widths],
        out_shape=[jax.ShapeDtypeStruct((n, w), F32) for w in widths],
        compiler_params=_params(("parallel",)), name="proj_in",
    )(x, w_cat)


def _reorder_w_in(w_in, h_b):
    cols_a = 2 * W_MIX
    cols_b = 3 * W_MIX + 2 * h_b + W_MIX
    o1, o2 = cols_a, cols_a + cols_b
    o3 = o2 + 2 * W_MIX
    wa, wb, wc, wd = w_in[:, :o1], w_in[:, o1:o2], w_in[:, o2:o3], w_in[:, o3:]
    qkv, bg, z = wb[:, :3 * W_MIX], wb[:, 3 * W_MIX:3 * W_MIX + 2 * h_b], wb[:, 3 * W_MIX + 2 * h_b:]
    bg = jnp.pad(bg, ((0, 0), (0, P_BG - 2 * h_b)))
    return jnp.concatenate([wd, bg, qkv, z, wa, wc], axis=1).astype(BF16)


def _sgu_kernel(p_ref, w_ref, b_ref, g_ref, beta_ref, y_ref, v_ref, *, n_chunks):
    w = w_ref[0]
    bias = b_ref[0]
    lane_head = lax.broadcasted_iota(jnp.int32, (SGU_CHUNK, W_MIX), 1) // HEAD_DIM
    for c in range(n_chunks):
        rows = pl.ds(c * SGU_CHUNK, SGU_CHUNK)
        x = p_ref[rows, :]
        h = 0.5 * x * (1.0 + lax.erf(x * (1.0 / math.sqrt(2.0))))
        u = h[:, :W_MIX]
        v = _layer_norm(h[:, W_MIX:], g_ref[...], beta_ref[...], LN_EPS)
        v_ref[rows, :] = v
        vb = jnp.concatenate([jnp.where(lane_head == hh, v, 0.0) for hh in range(N_HEADS)], axis=0)
        s = _dot(w, vb.astype(BF16)) + bias
        y_ref[rows, :] = u * s


def _sgu_weights(sgu_w, sgu_b, t_s):
    causal = jnp.tril(jnp.ones((SGU_CHUNK, SGU_CHUNK), bool))
    wp = jnp.where(causal, sgu_w, 0.0)
    reps = SGU_CHUNK // t_s
    ws = jnp.stack([jnp.kron(jnp.eye(reps, dtype=F32), wp[h, :t_s, :t_s]) for h in range(N_HEADS)])
    cat = lambda w: jnp.concatenate([w[h] for h in range(N_HEADS)], axis=1)
    w_eff = jnp.stack([cat(wp), cat(ws)]).astype(BF16)
    bp = jnp.repeat(sgu_b.T, HEAD_DIM, axis=1)
    bs = jnp.tile(bp[:t_s], (reps, 1))
    return w_eff, jnp.stack([bp, bs])


def _sgu_call(p_a, w_eff, b_eff, ln_g, ln_b, n_prompt_rows):
    n = p_a.shape[0]
    tb = _pick(math.gcd(n_prompt_rows, n - n_prompt_rows), (1024, 512, 256, 128))
    n_prompt_tiles = n_prompt_rows // tb
    grp = lambda i: jnp.minimum(i // n_prompt_tiles, 1)
    return pl.pallas_call(
        functools.partial(_sgu_kernel, n_chunks=tb // SGU_CHUNK), grid=(n // tb,),
        in_specs=[pl.BlockSpec((tb, P_A), lambda i: (i, 0)),
                  pl.BlockSpec((1, SGU_CHUNK, N_HEADS * SGU_CHUNK), lambda i: (grp(i), 0, 0)),
                  pl.BlockSpec((1, SGU_CHUNK, W_MIX), lambda i: (grp(i), 0, 0)),
                  pl.BlockSpec((1, W_MIX), lambda i: (0, 0)),
                  pl.BlockSpec((1, W_MIX), lambda i: (0, 0))],
        out_specs=[pl.BlockSpec((tb, W_MIX), lambda i: (i, 0))] * 2,
        out_shape=[jax.ShapeDtypeStruct((n, W_MIX), F32)] * 2,
        compiler_params=_params(("parallel",)), name="sgu",
    )(p_a, w_eff, b_eff, ln_g.reshape(1, -1), ln_b.reshape(1, -1))


CC_HDR = 32


def _cc_kernel(p_ref, buf_ref, w_ref, wb_ref, g_ref, b_ref, y_ref, nb_ref, xp_ref, *, sb, t, tt):
    hist = CC_WIDTH - 1
    ones = _head_ones()
    w = w_ref[...]
    for s in range(sb):
        x = p_ref[pl.ds(s * t, t), :]
        xp_ref[pl.ds(CC_HDR - hist, hist), :] = buf_ref[s]
        xp_ref[pl.ds(CC_HDR, t), :] = x[:, :W_MIX] * _sigmoid(x[:, W_MIX:])
        nb_ref[s] = xp_ref[pl.ds(t + CC_HDR - hist, hist), :]

        def tile(i, carry):
            base = pl.multiple_of(i * tt, SUBLANES)
            win = xp_ref[pl.ds(base, tt + CC_HDR), :]
            acc = jnp.zeros((tt, W_MIX), F32)
            shifted = [win[b:] for b in range(SUBLANES)]
            for j in range(CC_WIDTH):
                o = j + CC_HDR - hist
                a8 = (o // SUBLANES) * SUBLANES
                acc = acc + shifted[o % SUBLANES][a8:a8 + tt] * w[j:j + 1]
            hh = acc + wb_ref[...]
            mean = _dot_hi(hh, ones) * (1.0 / HEAD_DIM)
            xc = hh - mean
            var = _dot_hi(xc * xc, ones) * (1.0 / HEAD_DIM)
            yy = xc * lax.rsqrt(var + LN_EPS) * g_ref[...] + b_ref[...]
            y_ref[pl.ds(pl.multiple_of(s * t + base, SUBLANES), tt), :] = _silu(yy)
            return carry

        lax.fori_loop(0, t // tt, tile, 0)


def _cc_call(p_c, buf, w, wb, g, b, row0, t):
    nseq = buf.shape[0]
    sb = 1 if t >= 256 else _pick(nseq, (16, 8, 4, 2, 1))
    tt = min(t, 256)
    rows = sb * t
    blk0 = row0 // rows
    assert row0 % rows == 0 and t % tt == 0
    kern = functools.partial(_cc_kernel, sb=sb, t=t, tt=tt)
    in_specs = [pl.BlockSpec((rows, P_C), lambda i: (blk0 + i, 0)),
                pl.BlockSpec((sb, CC_WIDTH - 1, W_MIX), lambda i: (i, 0, 0)),
                pl.BlockSpec((CC_WIDTH, W_MIX), lambda i: (0, 0)),
                pl.BlockSpec((1, W_MIX), lambda i: (0, 0)),
                pl.BlockSpec((1, W_MIX), lambda i: (0, 0)),
                pl.BlockSpec((1, W_MIX), lambda i: (0, 0))]
    args = [p_c, buf, w, wb.reshape(1, -1), g.reshape(1, -1), b.reshape(1, -1)]
    return pl.pallas_call(
        kern, grid=(nseq // sb,), in_specs=in_specs,
        out_specs=[pl.BlockSpec((rows, W_MIX), lambda i: (i, 0)),
                   pl.BlockSpec((sb, CC_WIDTH - 1, W_MIX), lambda i: (i, 0, 0))],
        out_shape=[jax.ShapeDtypeStruct((nseq * t, W_MIX), F32),
                   jax.ShapeDtypeStruct((nseq, CC_WIDTH - 1, W_MIX), F32)],
        scratch_shapes=[pltpu.VMEM((t + CC_HDR, W_MIX), F32)],
        compiler_params=_params(("arbitrary",)), name="cc",
    )(*args)


GDN_HDR = 8
GDN_QKV = 3 * W_MIX


def _dot_tn_hi(a, b):
    return lax.dot_general(a, b, (((0,), (0,)), ((), ())), precision=HI, preferred_element_type=F32)


def _lane_expand(src_lane0):
    r = lax.broadcasted_iota(jnp.int32, (LANES, W_MIX), 0)
    c = lax.broadcasted_iota(jnp.int32, (LANES, W_MIX), 1) // HEAD_DIM
    return (r == c + src_lane0).astype(F32)


def _gdn_kernel(*refs, ns, tblk, c):
    p_refs, bg_refs = refs[:ns], refs[ns:2 * ns]
    buf_ref, s0_ref, cw_ref, alog_ref, dt_ref, ng_ref, y_ref, nb_ref, s_ref, hdr_ref = refs[2 * ns:]
    hist = GDN_CONV - 1
    hc = N_HEADS * c

    @pl.when(pl.program_id(1) == 0)
    def _():
        nb_ref[...] = buf_ref[...]
        s_ref[...] = s0_ref[...]
        hdr_ref[...] = jnp.zeros_like(hdr_ref)

    ones = _head_ones().astype(BF16)
    e_beta = _lane_expand(0).astype(BF16)
    e_g = _lane_expand(N_HEADS).astype(BF16)
    tri_ones = _tri(c, False).astype(BF16)
    strict_bd = _block_tri(c, True)
    incl_bd = _block_tri(c, False)
    eye_hc = (lax.broadcasted_iota(jnp.int32, (hc, hc), 0) == lax.broadcasted_iota(jnp.int32, (hc, hc), 1))
    eye_w = (lax.broadcasted_iota(jnp.int32, (W_MIX, W_MIX), 0) == lax.broadcasted_iota(jnp.int32, (W_MIX, W_MIX), 1))
    first_lane = (lax.broadcasted_iota(jnp.int32, (W_MIX, hc), 0) % HEAD_DIM == 0).astype(BF16)
    ones_hc = jnp.ones((hc, hc), BF16)
    ones_wv = jnp.ones((W_MIX, HEAD_DIM), BF16)
    cw = cw_ref[...]
    neg_a = -jnp.exp(alog_ref[...])
    mm = functools.partial(_mm, passes=GDN_PASSES)

    def chunk(n, carries):
        rows = pl.ds(pl.multiple_of(n * c, SUBLANES), c)
        sm = functools.partial(_stack_masked, c=c)
        cat = jnp.concatenate
        tails = [cr[0] for cr in carries]
        sts = [cr[1] for cr in carries]
        wins = [cat([tails[j], p_refs[j][rows, :GDN_QKV]], axis=0) for j in range(ns)]

        def conv_act(win):
            conv = jnp.zeros((c, GDN_QKV), F32)
            for tap in range(GDN_CONV):
                o = tap + GDN_HDR - hist
                conv = conv + win[o:o + c] * cw[tap:tap + 1]
            return _silu(conv)

        acts = _each(conv_act, wins)
        qs = [a[:, :W_MIX] for a in acts]
        ks = [a[:, W_MIX:2 * W_MIX] for a in acts]
        vs = [a[:, 2 * W_MIX:] for a in acts]
        sqs = _each(lambda q, k: _mm_exact_rhs(cat([q * q, k * k], axis=0), ones), qs, ks)
        qs = _each(lambda q, sq: q * lax.rsqrt(sq[:c] + 1e-6) * (HEAD_DIM ** -0.5), qs, sqs)
        ks = _each(lambda k, sq: k * lax.rsqrt(sq[c:] + 1e-6), ks, sqs)
        bgs = [bg_refs[j][rows, :] for j in range(ns)]
        betas = _each(lambda bg: _mm_exact_rhs(_sigmoid(bg), e_beta), bgs)
        gsums = _each(lambda bg: _mm_exact_lhs(tri_ones, neg_a * _softplus(bg + dt_ref[...])), bgs)
        gcums = _each(lambda gs: _mm_exact_rhs(gs, e_g), gsums)
        kbs = _each(lambda k, b: k * b, ks, betas)
        vbs = _each(lambda v, b: v * b, vs, betas)
        egcs = _each(jnp.exp, gcums)
        glasts = [gc[c - 1:c, :] for gc in gcums]
        kdecs = _each(lambda k, gl, gc: k * jnp.exp(gl - gc), ks, glasts, gcums)

        gcols = _each(lambda gc: _mm_exact_rhs(sm(gc), first_lane), gcums)
        grows = _each(lambda gcol: _mm_exact_lhs(ones_hc, jnp.where(eye_hc, gcol, 0.0)), gcols)
        decays = _each(lambda gcol, grow: jnp.where(incl_bd, jnp.exp(gcol - grow), 0.0), gcols, grows)
        prods = _each(lambda kb, q, k: mm(cat([sm(kb), sm(q)], axis=0), sm(k), _NT), kbs, qs, ks)
        lms = _each(lambda pr, dec: jnp.where(strict_bd, pr[:hc] * dec, 0.0), prods, decays)
        aqks = _each(lambda pr, dec: pr[hc:] * dec, prods, decays)
        tinvs = _block_neumann_inverse([-lm for lm in lms], c, GDN_PASSES)
        uws = _each(lambda ti, vb, kb, egc: mm(ti, cat([_stack_heads(vb), sm(kb * egc)], axis=1)),
                    tinvs, vbs, kbs, egcs)
        wss = _each(lambda uw, q, egc, st: mm(cat([uw[:, HEAD_DIM:], sm(q * egc)], axis=0), st),
                    uws, qs, egcs, sts)
        vnews = _each(lambda uw, ws: uw[:, :HEAD_DIM] - ws[:hc], uws, wss)
        outs = _each(lambda ws, aqk, vn: _unstack_heads(ws[hc:] + mm(aqk, vn), c), wss, aqks, vnews)
        grs = _each(lambda gl: _mm_exact_rhs(jnp.where(eye_w, jnp.exp(gl), 0.0), ones_wv), glasts)
        sts = _each(lambda st, gr, kd, vn: st * gr + mm(sm(kd), vn, _TN), sts, grs, kdecs, vnews)
        outs = _each(lambda o: o * lax.rsqrt(_mm_exact_rhs(o * o, ones) * (1.0 / HEAD_DIM) + RMS_EPS) * ng_ref[...],
                     outs)
        for j in range(ns):
            y_ref[j, rows, :] = outs[j] * _silu(p_refs[j][rows, GDN_QKV:])
        return tuple((wins[j][c:c + GDN_HDR], sts[j]) for j in range(ns))

    init = []
    for j in range(ns):
        hdr_ref[j, pl.ds(GDN_HDR - hist, hist), :] = nb_ref[j]
        init.append((hdr_ref[j], jnp.concatenate([s_ref[j, h] for h in range(N_HEADS)], axis=0)))
    fin = lax.fori_loop(0, tblk // c, chunk, tuple(init))
    for j in range(ns):
        tail, st = fin[j]
        nb_ref[j] = tail[GDN_HDR - hist:]
        for h in range(N_HEADS):
            s_ref[j, h] = st[h * HEAD_DIM:(h + 1) * HEAD_DIM]


def _gdn_call(p_b, p_bg, buf, s0, conv_w, a_log, dt_bias, norm_g, row0, t):
    nseq = buf.shape[0]
    h_b = a_log.shape[0]
    assert h_b == N_HEADS
    c = math.gcd(t, GDN_CHUNK)
    ns = _pick(nseq, (SEQ_GROUP, 2, 1))
    tblk = min(t, SEQ_TBLK)
    nt = t // tblk
    blk0 = row0 // tblk
    assert row0 % tblk == 0 and t % tblk == 0 and tblk % c == 0 and c % SUBLANES == 0
    lane_pad = lambda x: jnp.pad(x.reshape(1, -1), ((0, 0), (h_b, LANES - 2 * h_b)))
    seq_rows = lambda w: [pl.BlockSpec((tblk, w), functools.partial(_seq_block, j=j, ns=ns, nt=nt, blk0=blk0))
                          for j in range(ns)]
    kern = functools.partial(_gdn_kernel, ns=ns, tblk=tblk, c=c)
    in_specs = seq_rows(P_B) + seq_rows(P_BG) + [
        pl.BlockSpec((ns, GDN_CONV - 1, GDN_QKV), lambda i, tb: (i, 0, 0)),
        pl.BlockSpec((ns, N_HEADS, HEAD_DIM, HEAD_DIM), lambda i, tb: (i, 0, 0, 0)),
        pl.BlockSpec((GDN_CONV, GDN_QKV), lambda i, tb: (0, 0)),
        pl.BlockSpec((1, LANES), lambda i, tb: (0, 0)),
        pl.BlockSpec((1, LANES), lambda i, tb: (0, 0)),
        pl.BlockSpec((1, W_MIX), lambda i, tb: (0, 0))]
    args = [p_b] * ns + [p_bg] * ns + [buf, s0, conv_w, lane_pad(a_log), lane_pad(dt_bias),
                                       jnp.tile(norm_g, N_HEADS).reshape(1, -1)]
    y, nb, st = pl.pallas_call(
        kern, grid=(nseq // ns, nt), in_specs=in_specs,
        out_specs=[pl.BlockSpec((ns, tblk, W_MIX), lambda i, tb: (i, tb, 0)),
                   pl.BlockSpec((ns, GDN_CONV - 1, GDN_QKV), lambda i, tb: (i, 0, 0)),
                   pl.BlockSpec((ns, N_HEADS, HEAD_DIM, HEAD_DIM), lambda i, tb: (i, 0, 0, 0))],
        out_shape=[jax.ShapeDtypeStruct((nseq, t, W_MIX), F32),
                   jax.ShapeDtypeStruct((nseq, GDN_CONV - 1, GDN_QKV), F32),
                   jax.ShapeDtypeStruct((nseq, N_HEADS, HEAD_DIM, HEAD_DIM), F32)],
        scratch_shapes=[pltpu.VMEM((ns, GDN_HDR, GDN_QKV), F32)],
        compiler_params=_params(("arbitrary", "arbitrary")), name="gdn",
    )(*args)
    return y.reshape(nseq * t, W_MIX), nb, st


def _rwkv_kernel(*refs, ns, tblk, c):
    p_refs = refs[:ns]
    (sh_ref, s0_ref, mu_ref, w0_ref, a0_ref, lora_ref, kk_ref, ka_ref, rk_ref, g_ref, b_ref,
     y_ref, sho_ref, s_ref) = refs[ns:]
    hc = N_HEADS * c
    ones = _head_ones().astype(BF16)
    tri_ones = _tri(c, False).astype(BF16)
    strict_bd = _block_tri(c, True)
    incl_bd = _block_tri(c, False)
    lane = lax.broadcasted_iota(jnp.int32, (c, LANES), 1)
    row_id = lax.broadcasted_iota(jnp.int32, (c, COLS_D), 0)
    mm = functools.partial(_mm, passes=RWKV_PASSES)

    @pl.when(pl.program_id(1) == 0)
    def _():
        sho_ref[...] = sh_ref[...]
        s_ref[...] = s0_ref[...]

    def chunk(n, carries):
        rows = pl.ds(pl.multiple_of(n * c, SUBLANES), c)
        sm = functools.partial(_stack_masked, c=c)
        cat = jnp.concatenate
        prev_rows = [cr[0] for cr in carries]
        sts = [cr[1] for cr in carries]
        xs = [p_refs[j][rows, :] for j in range(ns)]
        xls = _each(lambda x, pr: x + (jnp.where(row_id == 0, pr, pltpu.roll(x, 1, 0)) - x) * mu_ref[...],
                    xs, prev_rows)
        rs = [xl[:, :W_MIX] for xl in xls]
        ks = [xl[:, W_MIX:2 * W_MIX] for xl in xls]
        vs = [xl[:, 2 * W_MIX:3 * W_MIX] for xl in xls]

        def lora_act(xl):
            lo = xl[:, 3 * W_MIX:]
            return jnp.where(lane < LORA_W, jnp.tanh(lo), jnp.where(lane < LORA_W + LORA_A, lo, _sigmoid(lo)))

        loras = _each(lambda xl: _dot_hi(lora_act(xl), lora_ref[...]), xls)
        lws = _each(lambda lr: -jnp.exp(-_softplus(-(w0_ref[...] + lr[:, :W_MIX])) - 0.5), loras)
        a_s = _each(lambda lr: _sigmoid(a0_ref[...] + lr[:, W_MIX:2 * W_MIX]), loras)
        gs = [lr[:, 2 * W_MIX:] for lr in loras]
        kkps = _each(lambda k: k * kk_ref[...], ks)
        k2s = _each(lambda k, a: k * (1.0 + (a - 1.0) * ka_ref[...]), ks, a_s)
        sums = _each(lambda kkp, r, k2: _mm_exact_rhs(cat([kkp * kkp, r * k2 * rk_ref[...]], axis=0), ones),
                     kkps, rs, k2s)
        kks = _each(lambda kkp, sm_: kkp * lax.rsqrt(sm_[:c] + 1e-6), kkps, sums)
        bonuses = [sm_[c:] for sm_ in sums]
        cums = _each(lambda lw: _mm_exact_lhs(tri_ones, lw), lws)
        invs = _each(lambda cum: jnp.exp(-cum), cums)
        a_hats = _each(lambda kk, cum, lw: -kk * jnp.exp(cum - lw), kks, cums, lws)
        b_hats = _each(lambda kk, a, inv: kk * a * inv, kks, a_s, invs)
        c_hats = _each(lambda k2, inv: k2 * inv, k2s, invs)
        q_hats = _each(lambda r, cum: r * jnp.exp(cum), rs, cums)
        gam_cs = [jnp.exp(cum[c - 1:c, :]) for cum in cums]

        xaqs = _each(lambda ah, qh: cat([sm(ah), sm(qh)], axis=0), a_hats, q_hats)
        bcss = _each(lambda bh, ch: cat([sm(bh), sm(ch)], axis=0), b_hats, c_hats)
        prods = _each(lambda xaq, bcs: mm(xaq, bcs, _NT), xaqs, bcss)
        a_ms = [jnp.where(strict_bd, pr[:hc, :hc], 0.0) for pr in prods]
        b_ms = [jnp.where(strict_bd, pr[:hc, hc:], 0.0) for pr in prods]
        p_qs = [cat([jnp.where(incl_bd, pr[hc:, :hc], 0.0), jnp.where(incl_bd, pr[hc:, hc:], 0.0)], axis=1)
                for pr in prods]
        tinvs = _block_neumann_inverse(a_ms, c, RWKV_PASSES)
        vss = _each(_stack_heads, vs)
        zos = _each(lambda xaq, st: mm(xaq, st, _NT), xaqs, sts)
        bvs = _each(lambda bm, v_: mm(bm, v_), b_ms, vss)
        zs = _each(lambda ti, zo, bv: mm(ti, zo[:hc] + bv), tinvs, zos, bvs)
        zvs = _each(lambda z, v_: cat([z, v_], axis=0), zs, vss)
        outs = _each(lambda zo, pq, zv: zo[hc:] + mm(pq, zv), zos, p_qs, zvs)
        sts = _each(lambda st, zv, bcs, gc: (st + mm(zv, bcs, _TN)) * gc, sts, zvs, bcss, gam_cs)

        ys = _each(lambda o: _unstack_heads(o, c), outs)
        ycs = _each(lambda y: y - _mm_exact_rhs(y, ones) * (1.0 / HEAD_DIM), ys)
        yns = _each(lambda yc: yc * lax.rsqrt(_mm_exact_rhs(yc * yc, ones) * (1.0 / HEAD_DIM) + GN_EPS)
                    * g_ref[...] + b_ref[...], ycs)
        for j in range(ns):
            y_ref[j, rows, :] = (yns[j] + bonuses[j] * vs[j]) * gs[j]
        return tuple((xs[j][c - 1:c, :], sts[j]) for j in range(ns))

    init = tuple((sho_ref[j], jnp.concatenate([s_ref[j, h] for h in range(N_HEADS)], axis=1)) for j in range(ns))
    fin = lax.fori_loop(0, tblk // c, chunk, init)
    for j in range(ns):
        last_row, st = fin[j]
        sho_ref[j] = last_row
        for h in range(N_HEADS):
            s_ref[j, h] = st[:, h * HEAD_DIM:(h + 1) * HEAD_DIM]


def _rwkv_lora_weights(w2, a2, g2):
    m = jnp.zeros((LANES, 3 * W_MIX), F32)
    m = m.at[:LORA_W, :W_MIX].set(w2)
    m = m.at[LORA_W:LORA_W + LORA_A, W_MIX:2 * W_MIX].set(a2)
    return m.at[LORA_W + LORA_A:, 2 * W_MIX:].set(g2)


def _rwkv_call(p_d, shift, s0, mu, w0, a0, lora_w, k_k, k_a, r_k, ln_g, ln_b, row0, t):
    nseq = shift.shape[0]
    c = math.gcd(t, RWKV_CHUNK)
    ns = _pick(nseq, (SEQ_GROUP, 2, 1))
    tblk = min(t, SEQ_TBLK)
    nt = t // tblk
    blk0 = row0 // tblk
    assert row0 % tblk == 0 and t % tblk == 0 and tblk % c == 0 and c % SUBLANES == 0
    row = lambda x: x.reshape(1, -1)
    vec = lambda w: pl.BlockSpec((1, w), lambda i, tb: (0, 0))
    kern = functools.partial(_rwkv_kernel, ns=ns, tblk=tblk, c=c)
    in_specs = [pl.BlockSpec((tblk, P_D), functools.partial(_seq_block, j=j, ns=ns, nt=nt, blk0=blk0))
                for j in range(ns)] + [
        pl.BlockSpec((ns, 1, COLS_D), lambda i, tb: (i, 0, 0)),
        pl.BlockSpec((ns, N_HEADS, HEAD_DIM, HEAD_DIM), lambda i, tb: (i, 0, 0, 0)),
        vec(COLS_D), vec(W_MIX), vec(W_MIX),
        pl.BlockSpec((LANES, 3 * W_MIX), lambda i, tb: (0, 0)),
        vec(W_MIX), vec(W_MIX), vec(W_MIX), vec(W_MIX), vec(W_MIX)]
    args = [p_d] * ns + [shift.reshape(nseq, 1, COLS_D), s0, row(mu), row(w0), row(a0), lora_w,
                         row(k_k), row(k_a), row(r_k), row(ln_g), row(ln_b)]
    y, sh, st = pl.pallas_call(
        kern, grid=(nseq // ns, nt), in_specs=in_specs,
        out_specs=[pl.BlockSpec((ns, tblk, W_MIX), lambda i, tb: (i, tb, 0)),
                   pl.BlockSpec((ns, 1, COLS_D), lambda i, tb: (i, 0, 0)),
                   pl.BlockSpec((ns, N_HEADS, HEAD_DIM, HEAD_DIM), lambda i, tb: (i, 0, 0, 0))],
        out_shape=[jax.ShapeDtypeStruct((nseq, t, W_MIX), F32),
                   jax.ShapeDtypeStruct((nseq, 1, COLS_D), F32),
                   jax.ShapeDtypeStruct((nseq, N_HEADS, HEAD_DIM, HEAD_DIM), F32)],
        compiler_params=_params(("arbitrary", "arbitrary")), name="rwkv",
    )(*args)
    return y.reshape(nseq * t, W_MIX), sh.reshape(nseq, COLS_D), st


ROUTE_IDX, ROUTE_GATE, ROUTE_RANK = 0, TOP_K, 2 * TOP_K


def _outproj_router_kernel(*refs, tm, alpha, n_prompt_tiles):
    yp_refs, ys_refs = refs[0:4], refs[4:8]
    x_ref, wo_ref, g_ref, b_ref, rw_ref, rb_ref, x1_ref, route_ref, cnt_ref, tri_ref = refs[8:]

    @pl.when(pl.program_id(0) == 0)
    def _():
        cnt_ref[...] = jnp.zeros_like(cnt_ref)
        tri_ref[...] = _tri(tm, True).astype(BF16)

    is_prompt = pl.program_id(0) < n_prompt_tiles
    mix = jnp.zeros((tm, D_MODEL), F32)
    for i in range(4):
        y = jnp.where(is_prompt, yp_refs[i][...], ys_refs[i][...])
        mix = mix + _dot(y.astype(BF16), wo_ref[pl.ds(i * W_MIX, W_MIX), :])
    x1 = _layer_norm(alpha * x_ref[...] + mix, g_ref[...], b_ref[...], LN_EPS)
    x1_ref[...] = x1

    logits = _mm(x1, rw_ref[...], passes=3) + rb_ref[...]
    lane = lax.broadcasted_iota(jnp.int32, (tm, LANES), 1)
    lane_f = lane.astype(F32)
    work = logits
    vals, hots, ids = [], [], []
    for _ in range(TOP_K):
        m = jnp.max(work, axis=-1, keepdims=True)
        idx = jnp.min(jnp.where(work == m, lane_f, float(LANES)), axis=-1, keepdims=True)
        hot = lane_f == idx
        vals.append(m)
        hots.append(hot)
        ids.append(idx)
        work = jnp.where(hot, -jnp.inf, work)
    exps = [jnp.exp(v - vals[0]) for v in vals]
    denom = exps[0] + exps[1] + exps[2] + exps[3]

    any_hot = jnp.zeros((tm, LANES), F32)
    for hot in hots:
        any_hot = any_hot + hot.astype(F32)
    before = _dot(tri_ref[...], any_hot.astype(BF16)) + cnt_ref[...]
    cnt_ref[...] = cnt_ref[...] + jnp.sum(any_hot, axis=0, keepdims=True)

    route = jnp.zeros((tm, LANES), F32)
    for kk in range(TOP_K):
        rank = jnp.sum(jnp.where(hots[kk], before, 0.0), axis=-1, keepdims=True)
        route = jnp.where(lane == ROUTE_IDX + kk, ids[kk], route)
        route = jnp.where(lane == ROUTE_GATE + kk, exps[kk] / denom, route)
        route = jnp.where(lane == ROUTE_RANK + kk, rank, route)
    route_ref[...] = route


def _outproj_router_call(ys_prompt, ys_sample, x, w_out, ln_g, ln_b, router_w, router_b, alpha):
    n = x.shape[0]
    n_p, n_s = ys_prompt[-1].shape[0], ys_sample[-1].shape[0]
    tm = _pick(math.gcd(n_p, n_s), (256, 128))
    npt = n_p // tm
    pmap = lambda y: (lambda i: (i, 0)) if y.shape[0] == n else (lambda i: (jnp.minimum(i, npt - 1), 0))
    smap = lambda y: (lambda i: (i, 0)) if y.shape[0] == n else (lambda i: (jnp.maximum(i - npt, 0), 0))
    row = lambda v: v.reshape(1, -1)
    vec = lambda w: pl.BlockSpec((1, w), lambda i: (0, 0))
    rw = jnp.pad(router_w, ((0, 0), (0, LANES - N_EXPERTS)))
    rb = jnp.pad(router_b, (0, LANES - N_EXPERTS), constant_values=NEG_BIG)
    return pl.pallas_call(
        functools.partial(_outproj_router_kernel, tm=tm, alpha=alpha, n_prompt_tiles=npt), grid=(n // tm,),
        in_specs=[pl.BlockSpec((tm, W_MIX), pmap(y)) for y in ys_prompt] + [
            pl.BlockSpec((tm, W_MIX), smap(y)) for y in ys_sample] + [
            pl.BlockSpec((tm, D_MODEL), lambda i: (i, 0)),
            pl.BlockSpec((D_MODEL, D_MODEL), lambda i: (0, 0)),
            vec(D_MODEL), vec(D_MODEL),
            pl.BlockSpec((D_MODEL, LANES), lambda i: (0, 0)), vec(LANES)],
        out_specs=[pl.BlockSpec((tm, D_MODEL), lambda i: (i, 0)),
                   pl.BlockSpec((tm, LANES), lambda i: (i, 0)),
                   pl.BlockSpec((1, LANES), lambda i: (0, 0))],
        out_shape=[jax.ShapeDtypeStruct((n, D_MODEL), F32),
                   jax.ShapeDtypeStruct((n, LANES), F32),
                   jax.ShapeDtypeStruct((1, LANES), F32)],
        scratch_shapes=[pltpu.VMEM((tm, tm), BF16)],
        compiler_params=_params(("arbitrary",)), name="outproj_router",
    )(*ys_prompt, *ys_sample, x, w_out, row(ln_g), row(ln_b), rw, row(rb))


MOE_TB = 256


def _moe_plan(route, counts, n):
    e_idx = route[:, ROUTE_IDX:ROUTE_IDX + TOP_K].astype(jnp.int32)
    rank = route[:, ROUTE_RANK:ROUTE_RANK + TOP_K].astype(jnp.int32)
    cnt = counts[0, :N_EXPERTS].astype(jnp.int32)
    padded = (cnt + MOE_TB - 1) // MOE_TB * MOE_TB
    pad_end = jnp.cumsum(padded)
    pad_start = pad_end - padded
    dest = (pad_start[e_idx] + rank).reshape(n * TOP_K)
    n_blocks = -(-n * TOP_K // MOE_TB) + N_EXPERTS
    n_used = pad_end[-1] // MOE_TB
    blk = jnp.minimum(jnp.arange(n_blocks), n_used - 1) * MOE_TB
    block_e = jnp.minimum(jnp.sum(pad_end[None, :] <= blk[:, None], axis=1), N_EXPERTS - 1).astype(jnp.int32)
    last_block_row = jnp.where(padded > 0, pad_end - MOE_TB, -1)
    tail = n_used + jnp.arange(N_EXPERTS)
    tail_row = jnp.where(tail < n_blocks, tail * MOE_TB, -1)
    zero_rows = jnp.concatenate([last_block_row, tail_row]).astype(jnp.int32)
    return dest, block_e, n_used.reshape(1).astype(jnp.int32), zero_rows, n_blocks


ROW_TILE = D_MODEL // LANES


def _store_row_tiles(ref, x):
    for s in range(ROW_TILE):
        ref[pl.ds(s, x.shape[0], stride=ROW_TILE), :] = x[:, s * LANES:(s + 1) * LANES]


def _load_row_tiles(ref, rows):
    return jnp.concatenate([ref[pl.ds(s, rows, stride=ROW_TILE), :] for s in range(ROW_TILE)], axis=1)


def _tile_rows(row):
    return pl.ds(pl.multiple_of(row * ROW_TILE, ROW_TILE), ROW_TILE)


def _dispatch_kernel(zrow_ref, dest_ref, x_ref, xs_ref, zbuf_ref, xbuf_ref, zsem, sems, *, tm, n_tiles):
    step = pl.program_id(0)
    slot = step % 2

    def zero_copy(e):
        row = pl.multiple_of(jnp.maximum(zrow_ref[e], 0) * ROW_TILE, MOE_TB * ROW_TILE)
        return pltpu.make_async_copy(zbuf_ref, xs_ref.at[pl.ds(row, MOE_TB * ROW_TILE)], zsem)

    @pl.when(pl.program_id(0) == 0)
    def _():
        zbuf_ref[...] = jnp.zeros_like(zbuf_ref)
        for e in range(2 * N_EXPERTS):
            @pl.when(zrow_ref[e] >= 0)
            def _():
                zero_copy(e).start()
        for e in range(2 * N_EXPERTS):
            @pl.when(zrow_ref[e] >= 0)
            def _():
                zero_copy(e).wait()

    def row_copy(s, t, dst_row):
        return pltpu.make_async_copy(xbuf_ref.at[s, _tile_rows(t)], xs_ref.at[_tile_rows(dst_row)], sems.at[s])

    def issue(t, c):
        for kk in range(TOP_K):
            row_copy(slot, t, dest_ref[t * TOP_K + kk]).start(priority=kk % 2)
        return c

    def drain(s):
        def body(t, c):
            for kk in range(TOP_K):
                row_copy(s, 0, 0).wait()
            return c
        lax.fori_loop(0, tm, body, 0)

    _store_row_tiles(xbuf_ref.at[slot], x_ref[...])
    lax.fori_loop(0, tm, issue, 0)

    @pl.when(step > 0)
    def _():
        drain(1 - slot)

    @pl.when(step == n_tiles - 1)
    def _():
        drain(slot)


def _dispatch_call(x1, dest, last_block_row, n_blocks):
    n = x1.shape[0]
    tm = _pick(n, (256, 128))
    return pl.pallas_call(
        functools.partial(_dispatch_kernel, tm=tm, n_tiles=n // tm),
        grid_spec=pltpu.PrefetchScalarGridSpec(
            num_scalar_prefetch=1, grid=(n // tm,),
            in_specs=[pl.BlockSpec((tm * TOP_K,), lambda i, z: (i,), memory_space=pltpu.SMEM),
                      pl.BlockSpec((tm, D_MODEL), lambda i, z: (i, 0))],
            out_specs=pl.BlockSpec(memory_space=pl.ANY),
            scratch_shapes=[pltpu.VMEM((MOE_TB * ROW_TILE, LANES), F32), pltpu.VMEM((2, tm * ROW_TILE, LANES), F32),
                            pltpu.SemaphoreType.DMA(()), pltpu.SemaphoreType.DMA((2,))]),
        out_shape=jax.ShapeDtypeStruct((n_blocks * MOE_TB * ROW_TILE, LANES), F32),
        compiler_params=_params(("arbitrary",)), name="moe_dispatch",
    )(last_block_row, dest, x1)


PAIR_GROUP = 2 * LANES


def _regroup_bias(b1):
    e, f2 = b1.shape
    return b1.reshape(e, f2 // PAIR_GROUP, LANES, 2).swapaxes(2, 3).reshape(e, 1, f2)


def _expert_kernel(be_ref, nu_ref, x_ref, w1_ref, b1_ref, w2_ref, b2_ref, y_ref, w1s_ref, w2s_ref):
    step = pl.program_id(0)
    live = step < nu_ref[0]
    new_expert = (step == 0) | (be_ref[step] != be_ref[jnp.maximum(step - 1, 0)])

    @pl.when(live & new_expert)
    def _():
        r = lax.broadcasted_iota(jnp.int32, (PAIR_GROUP, PAIR_GROUP), 0)
        c = lax.broadcasted_iota(jnp.int32, (PAIR_GROUP, PAIR_GROUP), 1)
        perm = (r == jnp.where(c < LANES, 2 * c, 2 * (c - LANES) + 1)).astype(BF16)
        for g in range(2 * D_FF // PAIR_GROUP):
            cols = pl.ds(g * PAIR_GROUP, PAIR_GROUP)
            w1s_ref[:, cols] = _dot(w1_ref[0, :, cols].astype(BF16), perm).astype(BF16)
        w2s_ref[...] = w2_ref[0].astype(BF16)

    @pl.when(live)
    def _():
        x = _load_row_tiles(x_ref, MOE_TB)
        h = _dot(x.astype(BF16), w1s_ref[...]) + b1_ref[0]
        acts = []
        for g in range(2 * D_FF // PAIR_GROUP):
            hg = jnp.minimum(h[:, g * PAIR_GROUP:g * PAIR_GROUP + LANES], SWIGLU_LIMIT)
            hl = jnp.clip(h[:, g * PAIR_GROUP + LANES:(g + 1) * PAIR_GROUP], -SWIGLU_LIMIT, SWIGLU_LIMIT)
            acts.append((hg * _sigmoid(SWIGLU_ALPHA * hg) * (hl + 1.0)).astype(BF16))
        _store_row_tiles(y_ref, _dot(jnp.concatenate(acts, axis=1), w2s_ref[...]) + b2_ref[0])

    @pl.when(jnp.logical_not(live))
    def _():
        y_ref[...] = jnp.zeros_like(y_ref)


def _expert_call(xs, block_e, n_used, w1, b1, w2, b2, e0):
    n_blocks = xs.shape[0] // (MOE_TB * ROW_TILE)
    xmap = lambda i, be, nu: (jnp.minimum(i, nu[0] - 1), 0)
    emap3 = lambda i, be, nu: (e0 + be[i], 0, 0)
    return pl.pallas_call(
        _expert_kernel,
        grid_spec=pltpu.PrefetchScalarGridSpec(
            num_scalar_prefetch=2, grid=(n_blocks,),
            in_specs=[pl.BlockSpec((MOE_TB * ROW_TILE, LANES), xmap),
                      pl.BlockSpec((1, D_MODEL, 2 * D_FF), emap3), pl.BlockSpec((1, 1, 2 * D_FF), emap3),
                      pl.BlockSpec((1, D_FF, D_MODEL), emap3), pl.BlockSpec((1, 1, D_MODEL), emap3)],
            out_specs=pl.BlockSpec((MOE_TB * ROW_TILE, LANES), lambda i, be, nu: (i, 0)),
            scratch_shapes=[pltpu.VMEM((D_MODEL, 2 * D_FF), BF16), pltpu.VMEM((D_FF, D_MODEL), BF16)]),
        out_shape=jax.ShapeDtypeStruct(xs.shape, F32),
        compiler_params=_params(("arbitrary",)), name="moe_experts",
    )(block_e, n_used, xs, w1, b1, w2, b2)


def _combine_kernel(dest_ref, dnext_ref, route_ref, x1_ref, ys_ref, g_ref, b_ref, x2_ref, buf_ref, sems,
                    *, tm, alpha, n_tiles):
    step = pl.program_id(0)
    slot = step % 2

    def row_copy(s, t, kk, src_row):
        return pltpu.make_async_copy(ys_ref.at[_tile_rows(src_row)], buf_ref.at[s, kk, _tile_rows(t)], sems.at[s])

    def issue(dref, s):
        def body(t, c):
            for kk in range(TOP_K):
                row_copy(s, t, kk, dref[t * TOP_K + kk]).start(priority=kk % 2)
            return c
        lax.fori_loop(0, tm, body, 0)

    @pl.when(step == 0)
    def _():
        issue(dest_ref, 0)

    @pl.when(step + 1 < n_tiles)
    def _():
        issue(dnext_ref, 1 - slot)

    def drain(t, c):
        for kk in range(TOP_K):
            row_copy(slot, 0, kk, 0).wait()
        return c

    lax.fori_loop(0, tm, drain, 0)
    route = route_ref[...]
    f = jnp.zeros((tm, D_MODEL), F32)
    for kk in range(TOP_K):
        f = f + _load_row_tiles(buf_ref.at[slot, kk], tm) * route[:, ROUTE_GATE + kk:ROUTE_GATE + kk + 1]
    x2_ref[...] = _layer_norm(alpha * x1_ref[...] + f, g_ref[...], b_ref[...], LN_EPS)


def _combine_call(x1, route, dest, ys, ln_g, ln_b, alpha):
    n = x1.shape[0]
    tm = _pick(n, (256, 128))
    n_tiles = n // tm
    vec = pl.BlockSpec((1, D_MODEL), lambda i: (0, 0))
    return pl.pallas_call(
        functools.partial(_combine_kernel, tm=tm, alpha=alpha, n_tiles=n_tiles), grid=(n_tiles,),
        in_specs=[pl.BlockSpec((tm * TOP_K,), lambda i: (i,), memory_space=pltpu.SMEM),
                  pl.BlockSpec((tm * TOP_K,), lambda i: (jnp.minimum(i + 1, n_tiles - 1),), memory_space=pltpu.SMEM),
                  pl.BlockSpec((tm, LANES), lambda i: (i, 0)),
                  pl.BlockSpec((tm, D_MODEL), lambda i: (i, 0)),
                  pl.BlockSpec(memory_space=pl.ANY), vec, vec],
        out_specs=pl.BlockSpec((tm, D_MODEL), lambda i: (i, 0)),
        out_shape=jax.ShapeDtypeStruct((n, D_MODEL), F32),
        scratch_shapes=[pltpu.VMEM((2, TOP_K, tm * ROW_TILE, LANES), F32), pltpu.SemaphoreType.DMA((2,))],
        compiler_params=_params(("arbitrary",)), name="moe_combine",
    )(dest, dest, route, x1, ys, ln_g.reshape(1, -1), ln_b.reshape(1, -1))


def _moe_ffn(x1, route, counts, w1, b1, w2, b2, e0, ln_g, ln_b, alpha):
    n = x1.shape[0]
    dest, block_e, n_used, zero_rows, n_blocks = _moe_plan(route, counts, n)
    xs = _dispatch_call(x1, dest, zero_rows, n_blocks)
    ys = _expert_call(xs, block_e, n_used, w1, b1, w2, b2, e0)
    return _combine_call(x1, route, dest, ys, ln_g, ln_b, alpha)


def kernel(x_prompt, x_sample, state_gdn_conv, state_gdn_S, state_cc_conv, state_rwkv_shift, state_rwkv_S, ln_in_g, ln_in_b, w_in, sgu_ln_g, sgu_ln_b, sgu_w, sgu_b, gdn_conv_w, gdn_A_log, gdn_dt_bias, gdn_norm_g, cc_dw_w, cc_dw_b, cc_ln_g, cc_ln_b, rw_mu, rw_w0, rw_w2, rw_a0, rw_a2, rw_g2, rw_k_k, rw_k_a, rw_r_k, rw_ln_g, rw_ln_b, w_out, ln_mix_g, ln_mix_b, router_w, router_b, moe_w1, moe_b1, moe_w2, moe_b2, ln_ffn_g, ln_ffn_b):
    bp, tp, _ = x_prompt.shape
    bs, ts, _ = x_sample.shape
    n_p, n_s = bp * tp, bs * ts
    depth = w_in.shape[0]
    alpha = (2 * depth) ** 0.25
    assert tp % SGU_CHUNK == 0 and SGU_CHUNK % ts == 0

    x = jnp.concatenate([x_prompt.reshape(n_p, D_MODEL), x_sample.reshape(n_s, D_MODEL)], axis=0)
    x = _ln_call(x, ln_in_g, ln_in_b)
    zeros = lambda *s: jnp.zeros(s, F32)
    n_exp = moe_w1.shape[1]
    w1_all = moe_w1.reshape(depth * n_exp, D_MODEL, 2 * D_FF)
    w2_all = moe_w2.reshape(depth * n_exp, D_FF, D_MODEL)
    b1_all = _regroup_bias(moe_b1.reshape(depth * n_exp, 2 * D_FF))
    b2_all = moe_b2.reshape(depth * n_exp, 1, D_MODEL)
    outs_p, outs_s = [], []
    for l in range(depth):
        p_d, p_bg, p_b, p_a, p_c = _proj_call(x, _reorder_w_in(w_in[l], gdn_A_log.shape[1]))

        w_eff, b_eff = _sgu_weights(sgu_w[l], sgu_b[l], ts)
        y_a, v = _sgu_call(p_a, w_eff, b_eff, sgu_ln_g[l], sgu_ln_b[l], n_p)
        v_p = v[:n_p].reshape(bp, tp, W_MIX)[:, ((tp - 1) // SGU_CHUNK) * SGU_CHUNK:]
        v_s = v[n_p:].reshape(bs, ts, W_MIX)

        gdn_w = (gdn_conv_w[l], gdn_A_log[l], gdn_dt_bias[l], gdn_norm_g[l])
        yb_p, gbuf_p, gs_p = _gdn_call(p_b, p_bg, zeros(bp, GDN_CONV - 1, GDN_QKV),
                                       zeros(bp, N_HEADS, HEAD_DIM, HEAD_DIM), *gdn_w, 0, tp)
        yb_s, gbuf_s, gs_s = _gdn_call(p_b, p_bg, state_gdn_conv[l], state_gdn_S[l], *gdn_w, n_p, ts)

        cc_w = (cc_dw_w[l], cc_dw_b[l], cc_ln_g[l], cc_ln_b[l])
        yc_p, cbuf_p = _cc_call(p_c, zeros(bp, CC_WIDTH - 1, W_MIX), *cc_w, 0, tp)
        yc_s, cbuf_s = _cc_call(p_c, state_cc_conv[l], *cc_w, n_p, ts)

        rw_w = (rw_mu[l], rw_w0[l], rw_a0[l], _rwkv_lora_weights(rw_w2[l], rw_a2[l], rw_g2[l]),
                rw_k_k[l], rw_k_a[l], rw_r_k[l].reshape(-1), rw_ln_g[l], rw_ln_b[l])
        yd_p, rsh_p, rs_p = _rwkv_call(p_d, zeros(bp, COLS_D), zeros(bp, N_HEADS, HEAD_DIM, HEAD_DIM), *rw_w, 0, tp)
        yd_s, rsh_s, rs_s = _rwkv_call(p_d, state_rwkv_shift[l], state_rwkv_S[l], *rw_w, n_p, ts)

        x1, route, counts = _outproj_router_call((y_a, yb_p, yc_p, yd_p), (y_a, yb_s, yc_s, yd_s), x,
                                                 w_out[l].astype(BF16), ln_mix_g[l], ln_mix_b[l],
                                                 router_w[l], router_b[l], alpha)
        x = _moe_ffn(x1, route, counts, w1_all, b1_all, w2_all, b2_all, l * n_exp,
                     ln_ffn_g[l], ln_ffn_b[l], alpha)
        outs_p.append((v_p, gbuf_p, gs_p, cbuf_p, rsh_p, rs_p))
        outs_s.append((v_s, gbuf_s, gs_s, cbuf_s, rsh_s, rs_s))

    stack = lambda outs, i: jnp.stack([o[i] for o in outs])
    res = [x[:n_p].reshape(bp, tp, D_MODEL), x[n_p:].reshape(bs, ts, D_MODEL)]
    for i in range(6):
        res += [stack(outs_p, i), stack(outs_s, i)]
    return tuple(res)
```

```python
import functools
import math

import jax
import jax.numpy as jnp
from jax import lax
from jax.experimental import pallas as pl
from jax.experimental.pallas import tpu as pltpu

F32 = jnp.float32
BF16 = jnp.bfloat16
HI = lax.Precision.HIGHEST

D_MODEL = 1024
HEAD_DIM = 64
W_MIX = 256
N_HEADS = W_MIX // HEAD_DIM
SGU_CHUNK = 128
GDN_CONV = 4
GDN_CHUNK = 32
CC_WIDTH = 31
RWKV_CHUNK = 32
SEQ_GROUP = 4
SEQ_TBLK = 256
RWKV_PASSES = 1
GDN_PASSES = 1
LORA_W, LORA_A, LORA_G = 32, 32, 64
COLS_D = 3 * W_MIX + LORA_W + LORA_A + LORA_G
N_EXPERTS = 32
TOP_K = 4
D_FF = D_MODEL
SWIGLU_ALPHA = 1.702
SWIGLU_LIMIT = 7.0
LN_EPS = 1e-5
RMS_EPS = 1e-6
GN_EPS = 64e-5
LANES = 128
SUBLANES = 8
VMEM_LIMIT = 56 * 1024 * 1024
NEG_BIG = -1e30

P_D = COLS_D
P_BG = LANES
P_B = 4 * W_MIX
P_A = 2 * W_MIX
P_C = 2 * W_MIX
P_TOTAL = P_D + P_BG + P_B + P_A + P_C


def _pick(n, cands):
    for c in cands:
        if n % c == 0:
            return c
    raise ValueError(f"no tile in {cands} divides {n}")


def _params(sem):
    return pltpu.CompilerParams(dimension_semantics=sem, vmem_limit_bytes=VMEM_LIMIT)


def _layer_norm(x, g, b, eps):
    xc = x - jnp.mean(x, -1, keepdims=True)
    var = jnp.mean(xc * xc, -1, keepdims=True)
    return xc * lax.rsqrt(var + eps) * g + b


def _sigmoid(x):
    return 1.0 / (1.0 + jnp.exp(-x))


def _silu(x):
    return x * _sigmoid(x)


def _softplus(x):
    return jnp.maximum(x, 0.0) + jnp.log(1.0 + jnp.exp(-jnp.abs(x)))


def _dot(a, b):
    return jnp.dot(a, b, preferred_element_type=F32)


def _dot_hi(a, b):
    return jnp.dot(a, b, precision=HI, preferred_element_type=F32)


def _dot_nt_hi(a, b):
    return lax.dot_general(a, b, (((1,), (1,)), ((), ())), precision=HI, preferred_element_type=F32)


def _head_ones():
    r = lax.broadcasted_iota(jnp.int32, (W_MIX, W_MIX), 0) // HEAD_DIM
    c = lax.broadcasted_iota(jnp.int32, (W_MIX, W_MIX), 1) // HEAD_DIM
    return (r == c).astype(F32)


def _tri(n, strict):
    r = lax.broadcasted_iota(jnp.int32, (n, n), 0)
    c = lax.broadcasted_iota(jnp.int32, (n, n), 1)
    return (r > c) if strict else (r >= c)


def _neumann_inverse(x, n):
    eye = (lax.broadcasted_iota(jnp.int32, (n, n), 0) == lax.broadcasted_iota(jnp.int32, (n, n), 1)).astype(F32)
    acc = eye + x
    p = x
    k = 2
    while k < n:
        p = _dot_hi(p, p)
        acc = acc + _dot_hi(acc, p)
        k *= 2
    return acc


_NN = (((1,), (0,)), ((), ()))
_NT = (((1,), (1,)), ((), ()))
_TN = (((0,), (0,)), ((), ()))


def _split2(x):
    hi = x.astype(BF16)
    return hi, (x - hi.astype(F32)).astype(BF16)


def _split3(x):
    hi = x.astype(BF16)
    r = x - hi.astype(F32)
    mid = r.astype(BF16)
    return hi, mid, (r - mid.astype(F32)).astype(BF16)


def _mm(a, b, dn=_NN, passes=1):
    d = lambda x, y: lax.dot_general(x, y, dn, preferred_element_type=F32)
    if passes == 1:
        return d(a.astype(BF16), b.astype(BF16))
    a_hi, a_lo = _split2(a)
    b_hi, b_lo = _split2(b)
    return d(a_hi, b_hi) + (d(a_lo, b_hi) + d(a_hi, b_lo))


def _mm_exact_rhs(a, sel, dn=_NN):
    d = lambda x: lax.dot_general(x, sel, dn, preferred_element_type=F32)
    hi, mid, lo = _split3(a)
    return d(hi) + (d(mid) + d(lo))


def _mm_exact_lhs(sel, b, dn=_NN):
    d = lambda x: lax.dot_general(sel, x, dn, preferred_element_type=F32)
    hi, mid, lo = _split3(b)
    return d(hi) + (d(mid) + d(lo))


def _block_neumann_inverse(xs, block, passes):
    n = xs[0].shape[0]
    eye = (lax.broadcasted_iota(jnp.int32, (n, n), 0) == lax.broadcasted_iota(jnp.int32, (n, n), 1)).astype(F32)
    accs = [eye + x for x in xs]
    ps = list(xs)
    k = 2
    while k < block:
        ps = [_mm(p, p, passes=passes) for p in ps]
        accs = [acc + _mm(acc, p, passes=passes) for acc, p in zip(accs, ps)]
        k *= 2
    return accs


def _each(fn, *lists):
    return [fn(*args) for args in zip(*lists)]


def _stack_masked(x, c):
    lane_head = lax.broadcasted_iota(jnp.int32, (c, W_MIX), 1) // HEAD_DIM
    return jnp.concatenate([jnp.where(lane_head == h, x, 0.0) for h in range(N_HEADS)], axis=0)


def _stack_heads(x):
    return jnp.concatenate([x[:, h * HEAD_DIM:(h + 1) * HEAD_DIM] for h in range(N_HEADS)], axis=0)


def _unstack_heads(x, c):
    return jnp.concatenate([x[h * c:(h + 1) * c] for h in range(N_HEADS)], axis=1)


def _seq_block(i, tb, *, j, ns, nt, blk0):
    return (blk0 + (i * ns + j) * nt + tb, 0)


def _block_tri(c, strict):
    n = N_HEADS * c
    r = lax.broadcasted_iota(jnp.int32, (n, n), 0)
    q = lax.broadcasted_iota(jnp.int32, (n, n), 1)
    same = (r // c) == (q // c)
    return same & ((r > q) if strict else (r >= q))


def _ln_kernel(x_ref, g_ref, b_ref, o_ref):
    o_ref[...] = _layer_norm(x_ref[...], g_ref[...], b_ref[...], LN_EPS)


def _ln_call(x, g, b):
    n = x.shape[0]
    tm = _pick(n, (1024, 512, 256, 128))
    return pl.pallas_call(
        _ln_kernel, grid=(n // tm,),
        in_specs=[pl.BlockSpec((tm, D_MODEL), lambda i: (i, 0)),
                  pl.BlockSpec((1, D_MODEL), lambda i: (0, 0)),
                  pl.BlockSpec((1, D_MODEL), lambda i: (0, 0))],
        out_specs=pl.BlockSpec((tm, D_MODEL), lambda i: (i, 0)),
        out_shape=jax.ShapeDtypeStruct((n, D_MODEL), F32),
        compiler_params=_params(("parallel",)), name="ln_in",
    )(x, g.reshape(1, -1), b.reshape(1, -1))


def _proj_kernel(x_ref, w_ref, pd_ref, pbg_ref, pb_ref, pa_ref, pc_ref):
    p = _dot(x_ref[...].astype(BF16), w_ref[...])
    o = 0
    for ref, w in ((pd_ref, P_D), (pbg_ref, P_BG), (pb_ref, P_B), (pa_ref, P_A), (pc_ref, P_C)):
        ref[...] = p[:, o:o + w]
        o += w


def _proj_call(x, w_cat):
    n = x.shape[0]
    tm = _pick(n, (512, 256, 128))
    widths = (P_D, P_BG, P_B, P_A, P_C)
    return pl.pallas_call(
        _proj_kernel, grid=(n // tm,),
        in_specs=[pl.BlockSpec((tm, D_MODEL), lambda i: (i, 0)),
                  pl.BlockSpec((D_MODEL, P_TOTAL), lambda i: (0, 0))],
        out_specs=[pl.BlockSpec((tm, w), lambda i: (i, 0)) for w in widths],
        out_shape=[jax.ShapeDtypeStruct((n, w), F32) for w in widths],
        compiler_params=_params(("parallel",)), name="proj_in",
    )(x, w_cat)


def _reorder_w_in(w_in, h_b):
    cols_a = 2 * W_MIX
    cols_b = 3 * W_MIX + 2 * h_b + W_MIX
    o1, o2 = cols_a, cols_a + cols_b
    o3 = o2 + 2 * W_MIX
    wa, wb, wc, wd = w_in[:, :o1], w_in[:, o1:o2], w_in[:, o2:o3], w_in[:, o3:]
    qkv, bg, z = wb[:, :3 * W_MIX], wb[:, 3 * W_MIX:3 * W_MIX + 2 * h_b], wb[:, 3 * W_MIX + 2 * h_b:]
    bg = jnp.pad(bg, ((0, 0), (0, P_BG - 2 * h_b)))
    return jnp.concatenate([wd, bg, qkv, z, wa, wc], axis=1).astype(BF16)


def _sgu_kernel(p_ref, w_ref, b_ref, g_ref, beta_ref, y_ref, v_ref, *, n_chunks):
    w = w_ref[0]
    bias = b_ref[0]
    lane_head = lax.broadcasted_iota(jnp.int32, (SGU_CHUNK, W_MIX), 1) // HEAD_DIM
    for c in range(n_chunks):
        rows = pl.ds(c * SGU_CHUNK, SGU_CHUNK)
        x = p_ref[rows, :]
        h = 0.5 * x * (1.0 + lax.erf(x * (1.0 / math.sqrt(2.0))))
        u = h[:, :W_MIX]
        v = _layer_norm(h[:, W_MIX:], g_ref[...], beta_ref[...], LN_EPS)
        v_ref[rows, :] = v
        vb = jnp.concatenate([jnp.where(lane_head == hh, v, 0.0) for hh in range(N_HEADS)], axis=0)
        s = _dot(w, vb.astype(BF16)) + bias
        y_ref[rows, :] = u * s


def _sgu_weights(sgu_w, sgu_b, t_s):
    causal = jnp.tril(jnp.ones((SGU_CHUNK, SGU_CHUNK), bool))
    wp = jnp.where(causal, sgu_w, 0.0)
    reps = SGU_CHUNK // t_s
    ws = jnp.stack([jnp.kron(jnp.eye(reps, dtype=F32), wp[h, :t_s, :t_s]) for h in range(N_HEADS)])
    cat = lambda w: jnp.concatenate([w[h] for h in range(N_HEADS)], axis=1)
    w_eff = jnp.stack([cat(wp), cat(ws)]).astype(BF16)
    bp = jnp.repeat(sgu_b.T, HEAD_DIM, axis=1)
    bs = jnp.tile(bp[:t_s], (reps, 1))
    return w_eff, jnp.stack([bp, bs])


def _sgu_call(p_a, w_eff, b_eff, ln_g, ln_b, n_prompt_rows):
    n = p_a.shape[0]
    tb = _pick(math.gcd(n_prompt_rows, n - n_prompt_rows), (1024, 512, 256, 128))
    n_prompt_tiles = n_prompt_rows // tb
    grp = lambda i: jnp.minimum(i // n_prompt_tiles, 1)
    return pl.pallas_call(
        functools.partial(_sgu_kernel, n_chunks=tb // SGU_CHUNK), grid=(n // tb,),
        in_specs=[pl.BlockSpec((tb, P_A), lambda i: (i, 0)),
                  pl.BlockSpec((1, SGU_CHUNK, N_HEADS * SGU_CHUNK), lambda i: (grp(i), 0, 0)),
                  pl.BlockSpec((1, SGU_CHUNK, W_MIX), lambda i: (grp(i), 0, 0)),
                  pl.BlockSpec((1, W_MIX), lambda i: (0, 0)),
                  pl.BlockSpec((1, W_MIX), lambda i: (0, 0))],
        out_specs=[pl.BlockSpec((tb, W_MIX), lambda i: (i, 0))] * 2,
        out_shape=[jax.ShapeDtypeStruct((n, W_MIX), F32)] * 2,
        compiler_params=_params(("parallel",)), name="sgu",
    )(p_a, w_eff, b_eff, ln_g.reshape(1, -1), ln_b.reshape(1, -1))


CC_HDR = 32


def _cc_kernel(p_ref, buf_ref, w_ref, wb_ref, g_ref, b_ref, y_ref, nb_ref, xp_ref, *, sb, t, tt):
    hist = CC_WIDTH - 1
    ones = _head_ones()
    w = w_ref[...]
    for s in range(sb):
        x = p_ref[pl.ds(s * t, t), :]
        xp_ref[pl.ds(CC_HDR - hist, hist), :] = buf_ref[s]
        xp_ref[pl.ds(CC_HDR, t), :] = x[:, :W_MIX] * _sigmoid(x[:, W_MIX:])
        nb_ref[s] = xp_ref[pl.ds(t + CC_HDR - hist, hist), :]

        def tile(i, carry):
            base = pl.multiple_of(i * tt, SUBLANES)
            win = xp_ref[pl.ds(base, tt + CC_HDR), :]
            acc = jnp.zeros((tt, W_MIX), F32)
            shifted = [win[b:] for b in range(SUBLANES)]
            for j in range(CC_WIDTH):
                o = j + CC_HDR - hist
                a8 = (o // SUBLANES) * SUBLANES
                acc = acc + shifted[o % SUBLANES][a8:a8 + tt] * w[j:j + 1]
            hh = acc + wb_ref[...]
            mean = _dot_hi(hh, ones) * (1.0 / HEAD_DIM)
            xc = hh - mean
            var = _dot_hi(xc * xc, ones) * (1.0 / HEAD_DIM)
            yy = xc * lax.rsqrt(var + LN_EPS) * g_ref[...] + b_ref[...]
            y_ref[pl.ds(pl.multiple_of(s * t + base, SUBLANES), tt), :] = _silu(yy)
            return carry

        lax.fori_loop(0, t // tt, tile, 0)


def _cc_call(p_c, buf, w, wb, g, b, row0, t):
    nseq = buf.shape[0]
    sb = 1 if t >= 256 else _pick(nseq, (16, 8, 4, 2, 1))
    tt = min(t, 256)
    rows = sb * t
    blk0 = row0 // rows
    assert row0 % rows == 0 and t % tt == 0
    kern = functools.partial(_cc_kernel, sb=sb, t=t, tt=tt)
    in_specs = [pl.BlockSpec((rows, P_C), lambda i: (blk0 + i, 0)),
                pl.BlockSpec((sb, CC_WIDTH - 1, W_MIX), lambda i: (i, 0, 0)),
                pl.BlockSpec((CC_WIDTH, W_MIX), lambda i: (0, 0)),
                pl.BlockSpec((1, W_MIX), lambda i: (0, 0)),
                pl.BlockSpec((1, W_MIX), lambda i: (0, 0)),
                pl.BlockSpec((1, W_MIX), lambda i: (0, 0))]
    args = [p_c, buf, w, wb.reshape(1, -1), g.reshape(1, -1), b.reshape(1, -1)]
    return pl.pallas_call(
        kern, grid=(nseq // sb,), in_specs=in_specs,
        out_specs=[pl.BlockSpec((rows, W_MIX), lambda i: (i, 0)),
                   pl.BlockSpec((sb, CC_WIDTH - 1, W_MIX), lambda i: (i, 0, 0))],
        out_shape=[jax.ShapeDtypeStruct((nseq * t, W_MIX), F32),
                   jax.ShapeDtypeStruct((nseq, CC_WIDTH - 1, W_MIX), F32)],
        scratch_shapes=[pltpu.VMEM((t + CC_HDR, W_MIX), F32)],
        compiler_params=_params(("arbitrary",)), name="cc",
    )(*args)


GDN_HDR = 8
GDN_QKV = 3 * W_MIX


def _dot_tn_hi(a, b):
    return lax.dot_general(a, b, (((0,), (0,)), ((), ())), precision=HI, preferred_element_type=F32)


def _lane_expand(src_lane0):
    r = lax.broadcasted_iota(jnp.int32, (LANES, W_MIX), 0)
    c = lax.broadcasted_iota(jnp.int32, (LANES, W_MIX), 1) // HEAD_DIM
    return (r == c + src_lane0).astype(F32)


def _gdn_kernel(*refs, ns, tblk, c):
    p_refs, bg_refs = refs[:ns], refs[ns:2 * ns]
    buf_ref, s0_ref, cw_ref, alog_ref, dt_ref, ng_ref, y_ref, nb_ref, s_ref, hdr_ref = refs[2 * ns:]
    hist = GDN_CONV - 1
    hc = N_HEADS * c

    @pl.when(pl.program_id(1) == 0)
    def _():
        nb_ref[...] = buf_ref[...]
        s_ref[...] = s0_ref[...]
        hdr_ref[...] = jnp.zeros_like(hdr_ref)

    ones = _head_ones().astype(BF16)
    e_beta = _lane_expand(0).astype(BF16)
    e_g = _lane_expand(N_HEADS).astype(BF16)
    tri_ones = _tri(c, False).astype(BF16)
    strict_bd = _block_tri(c, True)
    incl_bd = _block_tri(c, False)
    eye_hc = (lax.broadcasted_iota(jnp.int32, (hc, hc), 0) == lax.broadcasted_iota(jnp.int32, (hc, hc), 1))
    eye_w = (lax.broadcasted_iota(jnp.int32, (W_MIX, W_MIX), 0) == lax.broadcasted_iota(jnp.int32, (W_MIX, W_MIX), 1))
    first_lane = (lax.broadcasted_iota(jnp.int32, (W_MIX, hc), 0) % HEAD_DIM == 0).astype(BF16)
    ones_hc = jnp.ones((hc, hc), BF16)
    ones_wv = jnp.ones((W_MIX, HEAD_DIM), BF16)
    cw = cw_ref[...]
    neg_a = -jnp.exp(alog_ref[...])
    mm = functools.partial(_mm, passes=GDN_PASSES)

    def chunk(n, carries):
        rows = pl.ds(pl.multiple_of(n * c, SUBLANES), c)
        sm = functools.partial(_stack_masked, c=c)
        cat = jnp.concatenate
        tails = [cr[0] for cr in carries]
        sts = [cr[1] for cr in carries]
        wins = [cat([tails[j], p_refs[j][rows, :GDN_QKV]], axis=0) for j in range(ns)]

        def conv_act(win):
            conv = jnp.zeros((c, GDN_QKV), F32)
            for tap in range(GDN_CONV):
                o = tap + GDN_HDR - hist
                conv = conv + win[o:o + c] * cw[tap:tap + 1]
            return _silu(conv)

        acts = _each(conv_act, wins)
        qs = [a[:, :W_MIX] for a in acts]
        ks = [a[:, W_MIX:2 * W_MIX] for a in acts]
        vs = [a[:, 2 * W_MIX:] for a in acts]
        sqs = _each(lambda q, k: _mm_exact_rhs(cat([q * q, k * k], axis=0), ones), qs, ks)
        qs = _each(lambda q, sq: q * lax.rsqrt(sq[:c] + 1e-6) * (HEAD_DIM ** -0.5), qs, sqs)
        ks = _each(lambda k, sq: k * lax.rsqrt(sq[c:] + 1e-6), ks, sqs)
        bgs = [bg_refs[j][rows, :] for j in range(ns)]
        betas = _each(lambda bg: _mm_exact_rhs(_sigmoid(bg), e_beta), bgs)
        gsums = _each(lambda bg: _mm_exact_lhs(tri_ones, neg_a * _softplus(bg + dt_ref[...])), bgs)
        gcums = _each(lambda gs: _mm_exact_rhs(gs, e_g), gsums)
        kbs = _each(lambda k, b: k * b, ks, betas)
        vbs = _each(lambda v, b: v * b, vs, betas)
        egcs = _each(jnp.exp, gcums)
        glasts = [gc[c - 1:c, :] for gc in gcums]
        kdecs = _each(lambda k, gl, gc: k * jnp.exp(gl - gc), ks, glasts, gcums)

        gcols = _each(lambda gc: _mm_exact_rhs(sm(gc), first_lane), gcums)
        grows = _each(lambda gcol: _mm_exact_lhs(ones_hc, jnp.where(eye_hc, gcol, 0.0)), gcols)
        decays = _each(lambda gcol, grow: jnp.where(incl_bd, jnp.exp(gcol - grow), 0.0), gcols, grows)
        prods = _each(lambda kb, q, k: mm(cat([sm(kb), sm(q)], axis=0), sm(k), _NT), kbs, qs, ks)
        lms = _each(lambda pr, dec: jnp.where(strict_bd, pr[:hc] * dec, 0.0), prods, decays)
        aqks = _each(lambda pr, dec: pr[hc:] * dec, prods, decays)
        tinvs = _block_neumann_inverse([-lm for lm in lms], c, GDN_PASSES)
        uws = _each(lambda ti, vb, kb, egc: mm(ti, cat([_stack_heads(vb), sm(kb * egc)], axis=1)),
                    tinvs, vbs, kbs, egcs)
        wss = _each(lambda uw, q, egc, st: mm(cat([uw[:, HEAD_DIM:], sm(q * egc)], axis=0), st),
                    uws, qs, egcs, sts)
        vnews = _each(lambda uw, ws: uw[:, :HEAD_DIM] - ws[:hc], uws, wss)
        outs = _each(lambda ws, aqk, vn: _unstack_heads(ws[hc:] + mm(aqk, vn), c), wss, aqks, vnews)
        grs = _each(lambda gl: _mm_exact_rhs(jnp.where(eye_w, jnp.exp(gl), 0.0), ones_wv), glasts)
        sts = _each(lambda st, gr, kd, vn: st * gr + mm(sm(kd), vn, _TN), sts, grs, kdecs, vnews)
        outs = _each(lambda o: o * lax.rsqrt(_mm_exact_rhs(o * o, ones) * (1.0 / HEAD_DIM) + RMS_EPS) * ng_ref[...],
                     outs)
        for j in range(ns):
            y_ref[j, rows, :] = outs[j] * _silu(p_refs[j][rows, GDN_QKV:])
        return tuple((wins[j][c:c + GDN_HDR], sts[j]) for j in range(ns))

    init = []
    for j in range(ns):
        hdr_ref[j, pl.ds(GDN_HDR - hist, hist), :] = nb_ref[j]
        init.append((hdr_ref[j], jnp.concatenate([s_ref[j, h] for h in range(N_HEADS)], axis=0)))
    fin = lax.fori_loop(0, tblk // c, chunk, tuple(init))
    for j in range(ns):
        tail, st = fin[j]
        nb_ref[j] = tail[GDN_HDR - hist:]
        for h in range(N_HEADS):
            s_ref[j, h] = st[h * HEAD_DIM:(h + 1) * HEAD_DIM]


def _gdn_call(p_b, p_bg, buf, s0, conv_w, a_log, dt_bias, norm_g, row0, t):
    nseq = buf.shape[0]
    h_b = a_log.shape[0]
    assert h_b == N_HEADS
    c = math.gcd(t, GDN_CHUNK)
    ns = _pick(nseq, (SEQ_GROUP, 2, 1))
    tblk = min(t, SEQ_TBLK)
    nt = t // tblk
    blk0 = row0 // tblk
    assert row0 % tblk == 0 and t % tblk == 0 and tblk % c == 0 and c % SUBLANES == 0
    lane_pad = lambda x: jnp.pad(x.reshape(1, -1), ((0, 0), (h_b, LANES - 2 * h_b)))
    seq_rows = lambda w: [pl.BlockSpec((tblk, w), functools.partial(_seq_block, j=j, ns=ns, nt=nt, blk0=blk0))
                          for j in range(ns)]
    kern = functools.partial(_gdn_kernel, ns=ns, tblk=tblk, c=c)
    in_specs = seq_rows(P_B) + seq_rows(P_BG) + [
        pl.BlockSpec((ns, GDN_CONV - 1, GDN_QKV), lambda i, tb: (i, 0, 0)),
        pl.BlockSpec((ns, N_HEADS, HEAD_DIM, HEAD_DIM), lambda i, tb: (i, 0, 0, 0)),
        pl.BlockSpec((GDN_CONV, GDN_QKV), lambda i, tb: (0, 0)),
        pl.BlockSpec((1, LANES), lambda i, tb: (0, 0)),
        pl.BlockSpec((1, LANES), lambda i, tb: (0, 0)),
        pl.BlockSpec((1, W_MIX), lambda i, tb: (0, 0))]
    args = [p_b] * ns + [p_bg] * ns + [buf, s0, conv_w, lane_pad(a_log), lane_pad(dt_bias),
                                       jnp.tile(norm_g, N_HEADS).reshape(1, -1)]
    y, nb, st = pl.pallas_call(
        kern, grid=(nseq // ns, nt), in_specs=in_specs,
        out_specs=[pl.BlockSpec((ns, tblk, W_MIX), lambda i, tb: (i, tb, 0)),
                   pl.BlockSpec((ns, GDN_CONV - 1, GDN_QKV), lambda i, tb: (i, 0, 0)),
                   pl.BlockSpec((ns, N_HEADS, HEAD_DIM, HEAD_DIM), lambda i, tb: (i, 0, 0, 0))],
        out_shape=[jax.ShapeDtypeStruct((nseq, t, W_MIX), F32),
                   jax.ShapeDtypeStruct((nseq, GDN_CONV - 1, GDN_QKV), F32),
                   jax.ShapeDtypeStruct((nseq, N_HEADS, HEAD_DIM, HEAD_DIM), F32)],
        scratch_shapes=[pltpu.VMEM((ns, GDN_HDR, GDN_QKV), F32)],
        compiler_params=_params(("arbitrary", "arbitrary")), name="gdn",
    )(*args)
    return y.reshape(nseq * t, W_MIX), nb, st


def _rwkv_kernel(*refs, ns, tblk, c):
    p_refs = refs[:ns]
    (sh_ref, s0_ref, mu_ref, w0_ref, a0_ref, lora_ref, kk_ref, ka_ref, rk_ref, g_ref, b_ref,
     y_ref, sho_ref, s_ref) = refs[ns:]
    hc = N_HEADS * c
    ones = _head_ones().astype(BF16)
    tri_ones = _tri(c, False).astype(BF16)
    strict_bd = _block_tri(c, True)
    incl_bd = _block_tri(c, False)
    lane = lax.broadcasted_iota(jnp.int32, (c, LANES), 1)
    row_id = lax.broadcasted_iota(jnp.int32, (c, COLS_D), 0)
    mm = functools.partial(_mm, passes=RWKV_PASSES)

    @pl.when(pl.program_id(1) == 0)
    def _():
        sho_ref[...] = sh_ref[...]
        s_ref[...] = s0_ref[...]

    def chunk(n, carries):
        rows = pl.ds(pl.multiple_of(n * c, SUBLANES), c)
        sm = functools.partial(_stack_masked, c=c)
        cat = jnp.concatenate
        prev_rows = [cr[0] for cr in carries]
        sts = [cr[1] for cr in carries]
        xs = [p_refs[j][rows, :] for j in range(ns)]
        xls = _each(lambda x, pr: x + (jnp.where(row_id == 0, pr, pltpu.roll(x, 1, 0)) - x) * mu_ref[...],
                    xs, prev_rows)
        rs = [xl[:, :W_MIX] for xl in xls]
        ks = [xl[:, W_MIX:2 * W_MIX] for xl in xls]
        vs = [xl[:, 2 * W_MIX:3 * W_MIX] for xl in xls]

        def lora_act(xl):
            lo = xl[:, 3 * W_MIX:]
            return jnp.where(lane < LORA_W, jnp.tanh(lo), jnp.where(lane < LORA_W + LORA_A, lo, _sigmoid(lo)))

        loras = _each(lambda xl: _dot_hi(lora_act(xl), lora_ref[...]), xls)
        lws = _each(lambda lr: -jnp.exp(-_softplus(-(w0_ref[...] + lr[:, :W_MIX])) - 0.5), loras)
        a_s = _each(lambda lr: _sigmoid(a0_ref[...] + lr[:, W_MIX:2 * W_MIX]), loras)
        gs = [lr[:, 2 * W_MIX:] for lr in loras]
        kkps = _each(lambda k: k * kk_ref[...], ks)
        k2s = _each(lambda k, a: k * (1.0 + (a - 1.0) * ka_ref[...]), ks, a_s)
        sums = _each(lambda kkp, r, k2: _mm_exact_rhs(cat([kkp * kkp, r * k2 * rk_ref[...]], axis=0), ones),
                     kkps, rs, k2s)
        kks = _each(lambda kkp, sm_: kkp * lax.rsqrt(sm_[:c] + 1e-6), kkps, sums)
        bonuses = [sm_[c:] for sm_ in sums]
        cums = _each(lambda lw: _mm_exact_lhs(tri_ones, lw), lws)
        invs = _each(lambda cum: jnp.exp(-cum), cums)
        a_hats = _each(lambda kk, cum, lw: -kk * jnp.exp(cum - lw), kks, cums, lws)
        b_hats = _each(lambda kk, a, inv: kk * a * inv, kks, a_s, invs)
        c_hats = _each(lambda k2, inv: k2 * inv, k2s, invs)
        q_hats = _each(lambda r, cum: r * jnp.exp(cum), rs, cums)
        gam_cs = [jnp.exp(cum[c - 1:c, :]) for cum in cums]

        xaqs = _each(lambda ah, qh: cat([sm(ah), sm(qh)], axis=0), a_hats, q_hats)
        bcss = _each(lambda bh, ch: cat([sm(bh), sm(ch)], axis=0), b_hats, c_hats)
        prods = _each(lambda xaq, bcs: mm(xaq, bcs, _NT), xaqs, bcss)
        a_ms = [jnp.where(strict_bd, pr[:hc, :hc], 0.0) for pr in prods]
        b_ms = [jnp.where(strict_bd, pr[:hc, hc:], 0.0) for pr in prods]
        p_qs = [cat([jnp.where(incl_bd, pr[hc:, :hc], 0.0), jnp.where(incl_bd, pr[hc:, hc:], 0.0)], axis=1)
                for pr in prods]
        tinvs = _block_neumann_inverse(a_ms, c, RWKV_PASSES)
        vss = _each(_stack_heads, vs)
        zos = _each(lambda xaq, st: mm(xaq, st, _NT), xaqs, sts)
        bvs = _each(lambda bm, v_: mm(bm, v_), b_ms, vss)
        zs = _each(lambda ti, zo, bv: mm(ti, zo[:hc] + bv), tinvs, zos, bvs)
        zvs = _each(lambda z, v_: cat([z, v_], axis=0), zs, vss)
        outs = _each(lambda zo, pq, zv: zo[hc:] + mm(pq, zv), zos, p_qs, zvs)
        sts = _each(lambda st, zv, bcs, gc: (st + mm(zv, bcs, _TN)) * gc, sts, zvs, bcss, gam_cs)

        ys = _each(lambda o: _unstack_heads(o, c), outs)
        ycs = _each(lambda y: y - _mm_exact_rhs(y, ones) * (1.0 / HEAD_DIM), ys)
        yns = _each(lambda yc: yc * lax.rsqrt(_mm_exact_rhs(yc * yc, ones) * (1.0 / HEAD_DIM) + GN_EPS)
                    * g_ref[...] + b_ref[...], ycs)
        for j in range(ns):
            y_ref[j, rows, :] = (yns[j] + bonuses[j] * vs[j]) * gs[j]
        return tuple((xs[j][c - 1:c, :], sts[j]) for j in range(ns))

    init = tuple((sho_ref[j], jnp.concatenate([s_ref[j, h] for h in range(N_HEADS)], axis=1)) for j in range(ns))
    fin = lax.fori_loop(0, tblk // c, chunk, init)
    for j in range(ns):
        last_row, st = fin[j]
        sho_ref[j] = last_row
        for h in range(N_HEADS):
            s_ref[j, h] = st[:, h * HEAD_DIM:(h + 1) * HEAD_DIM]


def _rwkv_lora_weights(w2, a2, g2):
    m = jnp.zeros((LANES, 3 * W_MIX), F32)
    m = m.at[:LORA_W, :W_MIX].set(w2)
    m = m.at[LORA_W:LORA_W + LORA_A, W_MIX:2 * W_MIX].set(a2)
    return m.at[LORA_W + LORA_A:, 2 * W_MIX:].set(g2)


def _rwkv_call(p_d, shift, s0, mu, w0, a0, lora_w, k_k, k_a, r_k, ln_g, ln_b, row0, t):
    nseq = shift.shape[0]
    c = math.gcd(t, RWKV_CHUNK)
    ns = _pick(nseq, (SEQ_GROUP, 2, 1))
    tblk = min(t, SEQ_TBLK)
    nt = t // tblk
    blk0 = row0 // tblk
    assert row0 % tblk == 0 and t % tblk == 0 and tblk % c == 0 and c % SUBLANES == 0
    row = lambda x: x.reshape(1, -1)
    vec = lambda w: pl.BlockSpec((1, w), lambda i, tb: (0, 0))
    kern = functools.partial(_rwkv_kernel, ns=ns, tblk=tblk, c=c)
    in_specs = [pl.BlockSpec((tblk, P_D), functools.partial(_seq_block, j=j, ns=ns, nt=nt, blk0=blk0))
                for j in range(ns)] + [
        pl.BlockSpec((ns, 1, COLS_D), lambda i, tb: (i, 0, 0)),
        pl.BlockSpec((ns, N_HEADS, HEAD_DIM, HEAD_DIM), lambda i, tb: (i, 0, 0, 0)),
        vec(COLS_D), vec(W_MIX), vec(W_MIX),
        pl.BlockSpec((LANES, 3 * W_MIX), lambda i, tb: (0, 0)),
        vec(W_MIX), vec(W_MIX), vec(W_MIX), vec(W_MIX), vec(W_MIX)]
    args = [p_d] * ns + [shift.reshape(nseq, 1, COLS_D), s0, row(mu), row(w0), row(a0), lora_w,
                         row(k_k), row(k_a), row(r_k), row(ln_g), row(ln_b)]
    y, sh, st = pl.pallas_call(
        kern, grid=(nseq // ns, nt), in_specs=in_specs,
        out_specs=[pl.BlockSpec((ns, tblk, W_MIX), lambda i, tb: (i, tb, 0)),
                   pl.BlockSpec((ns, 1, COLS_D), lambda i, tb: (i, 0, 0)),
                   pl.BlockSpec((ns, N_HEADS, HEAD_DIM, HEAD_DIM), lambda i, tb: (i, 0, 0, 0))],
        out_shape=[jax.ShapeDtypeStruct((nseq, t, W_MIX), F32),
                   jax.ShapeDtypeStruct((nseq, 1, COLS_D), F32),
                   jax.ShapeDtypeStruct((nseq, N_HEADS, HEAD_DIM, HEAD_DIM), F32)],
        compiler_params=_params(("arbitrary", "arbitrary")), name="rwkv",
    )(*args)
    return y.reshape(nseq * t, W_MIX), sh.reshape(nseq, COLS_D), st


ROUTE_IDX, ROUTE_GATE, ROUTE_RANK = 0, TOP_K, 2 * TOP_K


def _outproj_router_kernel(*refs, tm, alpha, n_prompt_tiles):
    yp_refs, ys_refs = refs[0:4], refs[4:8]
    x_ref, wo_ref, g_ref, b_ref, rw_ref, rb_ref, x1_ref, x1t_ref, route_ref, cnt_ref, tri_ref = refs[8:]

    @pl.when(pl.program_id(0) == 0)
    def _():
        cnt_ref[...] = jnp.zeros_like(cnt_ref)
        tri_ref[...] = _tri(tm, True).astype(BF16)

    is_prompt = pl.program_id(0) < n_prompt_tiles
    mix = jnp.zeros((tm, D_MODEL), F32)
    for i in range(4):
        y = jnp.where(is_prompt, yp_refs[i][...], ys_refs[i][...])
        mix = mix + _dot(y.astype(BF16), wo_ref[pl.ds(i * W_MIX, W_MIX), :])
    x1 = _layer_norm(alpha * x_ref[...] + mix, g_ref[...], b_ref[...], LN_EPS)
    x1_ref[...] = x1
    _store_row_tiles(x1t_ref, x1)

    logits = _mm(x1, rw_ref[...], passes=3) + rb_ref[...]
    lane = lax.broadcasted_iota(jnp.int32, (tm, LANES), 1)
    lane_f = lane.astype(F32)
    work = logits
    vals, hots, ids = [], [], []
    for _ in range(TOP_K):
        m = jnp.max(work, axis=-1, keepdims=True)
        idx = jnp.min(jnp.where(work == m, lane_f, float(LANES)), axis=-1, keepdims=True)
        hot = lane_f == idx
        vals.append(m)
        hots.append(hot)
        ids.append(idx)
        work = jnp.where(hot, -jnp.inf, work)
    exps = [jnp.exp(v - vals[0]) for v in vals]
    denom = exps[0] + exps[1] + exps[2] + exps[3]

    any_hot = jnp.zeros((tm, LANES), F32)
    for hot in hots:
        any_hot = any_hot + hot.astype(F32)
    before = _dot(tri_ref[...], any_hot.astype(BF16)) + cnt_ref[...]
    cnt_ref[...] = cnt_ref[...] + jnp.sum(any_hot, axis=0, keepdims=True)

    route = jnp.zeros((tm, LANES), F32)
    for kk in range(TOP_K):
        rank = jnp.sum(jnp.where(hots[kk], before, 0.0), axis=-1, keepdims=True)
        route = jnp.where(lane == ROUTE_IDX + kk, ids[kk], route)
        route = jnp.where(lane == ROUTE_GATE + kk, exps[kk] / denom, route)
        route = jnp.where(lane == ROUTE_RANK + kk, rank, route)
    route_ref[...] = route


def _outproj_router_call(ys_prompt, ys_sample, x, w_out, ln_g, ln_b, router_w, router_b, alpha):
    n = x.shape[0]
    n_p, n_s = ys_prompt[-1].shape[0], ys_sample[-1].shape[0]
    tm = _pick(math.gcd(n_p, n_s), (256, 128))
    npt = n_p // tm
    pmap = lambda y: (lambda i: (i, 0)) if y.shape[0] == n else (lambda i: (jnp.minimum(i, npt - 1), 0))
    smap = lambda y: (lambda i: (i, 0)) if y.shape[0] == n else (lambda i: (jnp.maximum(i - npt, 0), 0))
    row = lambda v: v.reshape(1, -1)
    vec = lambda w: pl.BlockSpec((1, w), lambda i: (0, 0))
    rw = jnp.pad(router_w, ((0, 0), (0, LANES - N_EXPERTS)))
    rb = jnp.pad(router_b, (0, LANES - N_EXPERTS), constant_values=NEG_BIG)
    return pl.pallas_call(
        functools.partial(_outproj_router_kernel, tm=tm, alpha=alpha, n_prompt_tiles=npt), grid=(n // tm,),
        in_specs=[pl.BlockSpec((tm, W_MIX), pmap(y)) for y in ys_prompt] + [
            pl.BlockSpec((tm, W_MIX), smap(y)) for y in ys_sample] + [
            pl.BlockSpec((tm, D_MODEL), lambda i: (i, 0)),
            pl.BlockSpec((D_MODEL, D_MODEL), lambda i: (0, 0)),
            vec(D_MODEL), vec(D_MODEL),
            pl.BlockSpec((D_MODEL, LANES), lambda i: (0, 0)), vec(LANES)],
        out_specs=[pl.BlockSpec((tm, D_MODEL), lambda i: (i, 0)),
                   pl.BlockSpec((tm * ROW_TILE, LANES), lambda i: (i, 0)),
                   pl.BlockSpec((tm, LANES), lambda i: (i, 0)),
                   pl.BlockSpec((1, LANES), lambda i: (0, 0))],
        out_shape=[jax.ShapeDtypeStruct((n, D_MODEL), F32),
                   jax.ShapeDtypeStruct((n * ROW_TILE, LANES), F32),
                   jax.ShapeDtypeStruct((n, LANES), F32),
                   jax.ShapeDtypeStruct((1, LANES), F32)],
        scratch_shapes=[pltpu.VMEM((tm, tm), BF16)],
        compiler_params=_params(("arbitrary",)), name="outproj_router",
    )(*ys_prompt, *ys_sample, x, w_out, row(ln_g), row(ln_b), rw, row(rb))


MOE_TB = 256


def _moe_plan(route, counts, n):
    nk = n * TOP_K
    e_idx = route[:, ROUTE_IDX:ROUTE_IDX + TOP_K].astype(jnp.int32)
    rank = route[:, ROUTE_RANK:ROUTE_RANK + TOP_K].astype(jnp.int32)
    cnt = counts[0, :N_EXPERTS].astype(jnp.int32)
    padded = (cnt + MOE_TB - 1) // MOE_TB * MOE_TB
    pad_end = jnp.cumsum(padded)
    pad_start = pad_end - padded
    dest = (pad_start[e_idx] + rank).reshape(nk)
    n_blocks = -(-nk // MOE_TB) + N_EXPERTS
    n_slots = n_blocks * MOE_TB
    n_used = pad_end[-1] // MOE_TB
    blk = jnp.minimum(jnp.arange(n_blocks), n_used - 1) * MOE_TB
    block_e = jnp.minimum(jnp.sum(pad_end[None, :] <= blk[:, None], axis=1), N_EXPERTS - 1).astype(jnp.int32)
    assign = jnp.full((n_slots,), -1, jnp.int32).at[dest].set(jnp.arange(nk, dtype=jnp.int32), unique_indices=True)
    is_pad = assign < 0
    src = jnp.where(is_pad, 0, assign // TOP_K).astype(jnp.int32)
    dst = jnp.where(is_pad, nk + jnp.cumsum(is_pad.astype(jnp.int32)) - 1, assign).astype(jnp.int32)
    return src, dst, block_e, n_used.reshape(1).astype(jnp.int32), n_blocks


ROW_TILE = D_MODEL // LANES


def _store_row_tiles(ref, x):
    for s in range(ROW_TILE):
        ref[pl.ds(s, x.shape[0], stride=ROW_TILE), :] = x[:, s * LANES:(s + 1) * LANES]


def _load_row_tiles(ref, rows):
    return jnp.concatenate([ref[pl.ds(s, rows, stride=ROW_TILE), :] for s in range(ROW_TILE)], axis=1)


def _tile_rows(row):
    return pl.ds(pl.multiple_of(row * ROW_TILE, ROW_TILE), ROW_TILE)


PAIR_GROUP = 2 * LANES


def _regroup_bias(b1):
    e, f2 = b1.shape
    return b1.reshape(e, f2 // PAIR_GROUP, LANES, 2).swapaxes(2, 3).reshape(e, 1, f2)


def _expert_kernel(be_ref, nu_ref, src_ref, srcn_ref, dst_ref, xt_ref, w1_ref, b1_ref, w2_ref, b2_ref, yt_ref,
                   w1s_ref, w2s_ref, xin_ref, yout_ref, zbuf_ref, gsems, ssems, zsem, *, n_dump_chunks, dump_row0):
    step = pl.program_id(0)
    n_live = nu_ref[0]
    live = step < n_live
    slot = step % 2
    new_expert = (step == 0) | (be_ref[step] != be_ref[jnp.maximum(step - 1, 0)])

    def gather(rows_ref, s):
        def body(r2, c):
            for u in range(2):
                r = r2 * 2 + u
                pltpu.make_async_copy(xt_ref.at[_tile_rows(rows_ref[r])], xin_ref.at[s, _tile_rows(r)],
                                      gsems.at[s]).start(priority=u)
            return c
        lax.fori_loop(0, MOE_TB // 2, body, 0)

    def scatter(s):
        def body(r2, c):
            for u in range(2):
                r = r2 * 2 + u
                pltpu.make_async_copy(yout_ref.at[s, _tile_rows(r)], yt_ref.at[_tile_rows(dst_ref[r])],
                                      ssems.at[s]).start(priority=u)
            return c
        lax.fori_loop(0, MOE_TB // 2, body, 0)

    def drain(sems, s):
        def body(r, c):
            pltpu.make_async_copy(xt_ref.at[_tile_rows(0)], xin_ref.at[s, _tile_rows(0)], sems.at[s]).wait()
            return c
        lax.fori_loop(0, MOE_TB, body, 0)

    def zero_copy(k):
        row0 = (dump_row0 + k * MOE_TB) * ROW_TILE
        return pltpu.make_async_copy(zbuf_ref, yt_ref.at[pl.ds(row0, MOE_TB * ROW_TILE)], zsem)

    @pl.when(step == 0)
    def _():
        zbuf_ref[...] = jnp.zeros_like(zbuf_ref)
        for k in range(n_dump_chunks):
            zero_copy(k).start()
        for k in range(n_dump_chunks):
            zero_copy(k).wait()
        gather(src_ref, 0)

    @pl.when(step + 1 < n_live)
    def _():
        gather(srcn_ref, 1 - slot)

    @pl.when(live & new_expert)
    def _():
        r = lax.broadcasted_iota(jnp.int32, (PAIR_GROUP, PAIR_GROUP), 0)
        c = lax.broadcasted_iota(jnp.int32, (PAIR_GROUP, PAIR_GROUP), 1)
        perm = (r == jnp.where(c < LANES, 2 * c, 2 * (c - LANES) + 1)).astype(BF16)
        for g in range(2 * D_FF // PAIR_GROUP):
            cols = pl.ds(g * PAIR_GROUP, PAIR_GROUP)
            w1s_ref[:, cols] = _dot(w1_ref[0, :, cols].astype(BF16), perm).astype(BF16)
        w2s_ref[...] = w2_ref[0].astype(BF16)

    @pl.when(live)
    def _():
        drain(gsems, slot)
        x = _load_row_tiles(xin_ref.at[slot], MOE_TB)
        h = _dot(x.astype(BF16), w1s_ref[...]) + b1_ref[0]
        acts = []
        for g in range(2 * D_FF // PAIR_GROUP):
            hg = jnp.minimum(h[:, g * PAIR_GROUP:g * PAIR_GROUP + LANES], SWIGLU_LIMIT)
            hl = jnp.clip(h[:, g * PAIR_GROUP + LANES:(g + 1) * PAIR_GROUP], -SWIGLU_LIMIT, SWIGLU_LIMIT)
            acts.append((hg * _sigmoid(SWIGLU_ALPHA * hg) * (hl + 1.0)).astype(BF16))
        _store_row_tiles(yout_ref.at[slot], _dot(jnp.concatenate(acts, axis=1), w2s_ref[...]) + b2_ref[0])

    @pl.when(live & (step > 0))
    def _():
        drain(ssems, 1 - slot)

    @pl.when(live)
    def _():
        scatter(slot)

    @pl.when(step == n_live - 1)
    def _():
        drain(ssems, slot)


def _expert_call(xt, src, dst, block_e, n_used, n_blocks, w1, b1, w2, b2, e0):
    nk = xt.shape[0] // ROW_TILE * TOP_K
    n_dump_chunks = -(-(n_blocks * MOE_TB - nk) // MOE_TB)
    out_rows = nk + n_dump_chunks * MOE_TB
    emap3 = lambda i, be, nu: (e0 + be[i], 0, 0)
    cur = lambda i, be, nu: (jnp.minimum(i, nu[0] - 1),)
    nxt = lambda i, be, nu: (jnp.minimum(i + 1, nu[0] - 1),)
    smem = lambda m: pl.BlockSpec((MOE_TB,), m, memory_space=pltpu.SMEM)
    tile_buf = pltpu.VMEM((2, MOE_TB * ROW_TILE, LANES), F32)
    return pl.pallas_call(
        functools.partial(_expert_kernel, n_dump_chunks=n_dump_chunks, dump_row0=nk),
        grid_spec=pltpu.PrefetchScalarGridSpec(
            num_scalar_prefetch=2, grid=(n_blocks,),
            in_specs=[smem(cur), smem(nxt), smem(cur), pl.BlockSpec(memory_space=pl.ANY),
                      pl.BlockSpec((1, D_MODEL, 2 * D_FF), emap3), pl.BlockSpec((1, 1, 2 * D_FF), emap3),
                      pl.BlockSpec((1, D_FF, D_MODEL), emap3), pl.BlockSpec((1, 1, D_MODEL), emap3)],
            out_specs=pl.BlockSpec(memory_space=pl.ANY),
            scratch_shapes=[pltpu.VMEM((D_MODEL, 2 * D_FF), BF16), pltpu.VMEM((D_FF, D_MODEL), BF16),
                            tile_buf, tile_buf, pltpu.VMEM((MOE_TB * ROW_TILE, LANES), F32),
                            pltpu.SemaphoreType.DMA((2,)), pltpu.SemaphoreType.DMA((2,)),
                            pltpu.SemaphoreType.DMA(())]),
        out_shape=jax.ShapeDtypeStruct((out_rows * ROW_TILE, LANES), F32),
        compiler_params=_params(("arbitrary",)), name="moe_experts",
    )(block_e, n_used, src, src, dst, xt, w1, b1, w2, b2)


def _combine_kernel(route_ref, x1_ref, yt_ref, g_ref, b_ref, x2_ref, *, tm, alpha):
    route = route_ref[...]
    f = jnp.zeros((tm, D_MODEL), F32)
    for kk in range(TOP_K):
        rows = jnp.concatenate([yt_ref[pl.ds(kk * ROW_TILE + s, tm, stride=TOP_K * ROW_TILE), :]
                                for s in range(ROW_TILE)], axis=1)
        f = f + rows * route[:, ROUTE_GATE + kk:ROUTE_GATE + kk + 1]
    x2_ref[...] = _layer_norm(alpha * x1_ref[...] + f, g_ref[...], b_ref[...], LN_EPS)


def _combine_call(x1, route, yt, ln_g, ln_b, alpha):
    n = x1.shape[0]
    tm = _pick(n, (256, 128))
    vec = pl.BlockSpec((1, D_MODEL), lambda i: (0, 0))
    return pl.pallas_call(
        functools.partial(_combine_kernel, tm=tm, alpha=alpha), grid=(n // tm,),
        in_specs=[pl.BlockSpec((tm, LANES), lambda i: (i, 0)),
                  pl.BlockSpec((tm, D_MODEL), lambda i: (i, 0)),
                  pl.BlockSpec((tm * TOP_K * ROW_TILE, LANES), lambda i: (i, 0)), vec, vec],
        out_specs=pl.BlockSpec((tm, D_MODEL), lambda i: (i, 0)),
        out_shape=jax.ShapeDtypeStruct((n, D_MODEL), F32),
        compiler_params=_params(("parallel",)), name="moe_combine",
    )(route, x1, yt, ln_g.reshape(1, -1), ln_b.reshape(1, -1))


def _moe_ffn(x1, x1t, route, counts, w1, b1, w2, b2, e0, ln_g, ln_b, alpha):
    n = x1.shape[0]
    src, dst, block_e, n_used, n_blocks = _moe_plan(route, counts, n)
    yt = _expert_call(x1t, src, dst, block_e, n_used, n_blocks, w1, b1, w2, b2, e0)
    return _combine_call(x1, route, yt, ln_g, ln_b, alpha)


def kernel(x_prompt, x_sample, state_gdn_conv, state_gdn_S, state_cc_conv, state_rwkv_shift, state_rwkv_S, ln_in_g, ln_in_b, w_in, sgu_ln_g, sgu_ln_b, sgu_w, sgu_b, gdn_conv_w, gdn_A_log, gdn_dt_bias, gdn_norm_g, cc_dw_w, cc_dw_b, cc_ln_g, cc_ln_b, rw_mu, rw_w0, rw_w2, rw_a0, rw_a2, rw_g2, rw_k_k, rw_k_a, rw_r_k, rw_ln_g, rw_ln_b, w_out, ln_mix_g, ln_mix_b, router_w, router_b, moe_w1, moe_b1, moe_w2, moe_b2, ln_ffn_g, ln_ffn_b):
    bp, tp, _ = x_prompt.shape
    bs, ts, _ = x_sample.shape
    n_p, n_s = bp * tp, bs * ts
    depth = w_in.shape[0]
    alpha = (2 * depth) ** 0.25
    assert tp % SGU_CHUNK == 0 and SGU_CHUNK % ts == 0

    x = jnp.concatenate([x_prompt.reshape(n_p, D_MODEL), x_sample.reshape(n_s, D_MODEL)], axis=0)
    x = _ln_call(x, ln_in_g, ln_in_b)
    zeros = lambda *s: jnp.zeros(s, F32)
    n_exp = moe_w1.shape[1]
    w1_all = moe_w1.reshape(depth * n_exp, D_MODEL, 2 * D_FF)
    w2_all = moe_w2.reshape(depth * n_exp, D_FF, D_MODEL)
    b1_all = _regroup_bias(moe_b1.reshape(depth * n_exp, 2 * D_FF))
    b2_all = moe_b2.reshape(depth * n_exp, 1, D_MODEL)
    outs_p, outs_s = [], []
    for l in range(depth):
        p_d, p_bg, p_b, p_a, p_c = _proj_call(x, _reorder_w_in(w_in[l], gdn_A_log.shape[1]))

        w_eff, b_eff = _sgu_weights(sgu_w[l], sgu_b[l], ts)
        y_a, v = _sgu_call(p_a, w_eff, b_eff, sgu_ln_g[l], sgu_ln_b[l], n_p)
        v_p = v[:n_p].reshape(bp, tp, W_MIX)[:, ((tp - 1) // SGU_CHUNK) * SGU_CHUNK:]
        v_s = v[n_p:].reshape(bs, ts, W_MIX)

        gdn_w = (gdn_conv_w[l], gdn_A_log[l], gdn_dt_bias[l], gdn_norm_g[l])
        yb_p, gbuf_p, gs_p = _gdn_call(p_b, p_bg, zeros(bp, GDN_CONV - 1, GDN_QKV),
                                       zeros(bp, N_HEADS, HEAD_DIM, HEAD_DIM), *gdn_w, 0, tp)
        yb_s, gbuf_s, gs_s = _gdn_call(p_b, p_bg, state_gdn_conv[l], state_gdn_S[l], *gdn_w, n_p, ts)

        cc_w = (cc_dw_w[l], cc_dw_b[l], cc_ln_g[l], cc_ln_b[l])
        yc_p, cbuf_p = _cc_call(p_c, zeros(bp, CC_WIDTH - 1, W_MIX), *cc_w, 0, tp)
        yc_s, cbuf_s = _cc_call(p_c, state_cc_conv[l], *cc_w, n_p, ts)

        rw_w = (rw_mu[l], rw_w0[l], rw_a0[l], _rwkv_lora_weights(rw_w2[l], rw_a2[l], rw_g2[l]),
                rw_k_k[l], rw_k_a[l], rw_r_k[l].reshape(-1), rw_ln_g[l], rw_ln_b[l])
        yd_p, rsh_p, rs_p = _rwkv_call(p_d, zeros(bp, COLS_D), zeros(bp, N_HEADS, HEAD_DIM, HEAD_DIM), *rw_w, 0, tp)
        yd_s, rsh_s, rs_s = _rwkv_call(p_d, state_rwkv_shift[l], state_rwkv_S[l], *rw_w, n_p, ts)

        x1, x1t, route, counts = _outproj_router_call((y_a, yb_p, yc_p, yd_p), (y_a, yb_s, yc_s, yd_s), x,
                                                      w_out[l].astype(BF16), ln_mix_g[l], ln_mix_b[l],
                                                      router_w[l], router_b[l], alpha)
        x = _moe_ffn(x1, x1t, route, counts, w1_all, b1_all, w2_all, b2_all, l * n_exp,
                     ln_ffn_g[l], ln_ffn_b[l], alpha)
        outs_p.append((v_p, gbuf_p, gs_p, cbuf_p, rsh_p, rs_p))
        outs_s.append((v_s, gbuf_s, gs_s, cbuf_s, rsh_s, rs_s))

    stack = lambda outs, i: jnp.stack([o[i] for o in outs])
    res = [x[:n_p].reshape(bp, tp, D_MODEL), x[n_p:].reshape(bs, ts, D_MODEL)]
    for i in range(6):
        res += [stack(outs_p, i), stack(outs_s, i)]
    return tuple(res)
```

```python
import functools
import math

import jax
import jax.numpy as jnp
from jax import lax
from jax.experimental import pallas as pl
from jax.experimental.pallas import tpu as pltpu

F32 = jnp.float32
BF16 = jnp.bfloat16

D_MODEL = 1024
HEAD_DIM = 64
W_MIX = 256
N_HEADS = W_MIX // HEAD_DIM
SGU_CHUNK = 128
GDN_CONV = 4
GDN_CHUNK = 32
CC_WIDTH = 31
RWKV_CHUNK = 32
SEQ_GROUP = 4
SEQ_TBLK = 256
RWKV_PASSES = 1
GDN_PASSES = 1
LORA_W, LORA_A, LORA_G = 32, 32, 64
COLS_D = 3 * W_MIX + LORA_W + LORA_A + LORA_G
N_EXPERTS = 32
TOP_K = 4
D_FF = D_MODEL
SWIGLU_ALPHA = 1.702
SWIGLU_LIMIT = 7.0
LN_EPS = 1e-5
RMS_EPS = 1e-6
GN_EPS = 64e-5
LANES = 128
SUBLANES = 8
VMEM_LIMIT = 56 * 1024 * 1024
NEG_BIG = -1e30

P_D = COLS_D
P_BG = LANES
P_B = 4 * W_MIX
P_A = 2 * W_MIX
P_C = 2 * W_MIX
P_TOTAL = P_D + P_BG + P_B + P_A + P_C


def _pick(n, cands):
    for c in cands:
        if n % c == 0:
            return c
    raise ValueError(f"no tile in {cands} divides {n}")


def _params(sem):
    return pltpu.CompilerParams(dimension_semantics=sem, vmem_limit_bytes=VMEM_LIMIT)


def _layer_norm(x, g, b, eps):
    xc = x - jnp.mean(x, -1, keepdims=True)
    var = jnp.mean(xc * xc, -1, keepdims=True)
    return xc * lax.rsqrt(var + eps) * g + b


def _sigmoid(x):
    return 1.0 / (1.0 + jnp.exp(-x))


def _silu(x):
    return x * _sigmoid(x)


def _softplus(x):
    return jnp.maximum(x, 0.0) + jnp.log(1.0 + jnp.exp(-jnp.abs(x)))


def _dot(a, b):
    return jnp.dot(a, b, preferred_element_type=F32)


def _head_ones():
    r = lax.broadcasted_iota(jnp.int32, (W_MIX, W_MIX), 0) // HEAD_DIM
    c = lax.broadcasted_iota(jnp.int32, (W_MIX, W_MIX), 1) // HEAD_DIM
    return (r == c).astype(F32)


def _tri(n, strict):
    r = lax.broadcasted_iota(jnp.int32, (n, n), 0)
    c = lax.broadcasted_iota(jnp.int32, (n, n), 1)
    return (r > c) if strict else (r >= c)


_NN =(((1,), (0,)), ((), ()))
_NT = (((1,), (1,)), ((), ()))
_TN = (((0,), (0,)), ((), ()))


def _split2(x):
    hi = x.astype(BF16)
    return hi, (x - hi.astype(F32)).astype(BF16)


def _split3(x):
    hi = x.astype(BF16)
    r = x - hi.astype(F32)
    mid = r.astype(BF16)
    return hi, mid, (r - mid.astype(F32)).astype(BF16)


def _mm(a, b, dn=_NN, passes=1):
    d = lambda x, y: lax.dot_general(x, y, dn, preferred_element_type=F32)
    if passes == 1:
        return d(a.astype(BF16), b.astype(BF16))
    a_hi, a_lo = _split2(a)
    b_hi, b_lo = _split2(b)
    return d(a_hi, b_hi) + (d(a_lo, b_hi) + d(a_hi, b_lo))


def _mm_exact_rhs(a, sel, dn=_NN, pieces=3):
    d = lambda x: lax.dot_general(x, sel, dn, preferred_element_type=F32)
    if pieces == 2:
        hi, lo = _split2(a)
        return d(hi) + d(lo)
    hi, mid, lo = _split3(a)
    return d(hi) + (d(mid) + d(lo))


def _mm_exact_lhs(sel, b, dn=_NN, pieces=3):
    d = lambda x: lax.dot_general(sel, x, dn, preferred_element_type=F32)
    if pieces == 2:
        hi, lo = _split2(b)
        return d(hi) + d(lo)
    hi, mid, lo = _split3(b)
    return d(hi) + (d(mid) + d(lo))


def _block_neumann_inverse(xs, block, passes):
    n = xs[0].shape[0]
    eye = (lax.broadcasted_iota(jnp.int32, (n, n), 0) == lax.broadcasted_iota(jnp.int32, (n, n), 1)).astype(F32)
    accs = [eye + x for x in xs]
    ps = list(xs)
    k = 2
    while k < block:
        ps = [_mm(p, p, passes=passes) for p in ps]
        accs = [acc + _mm(acc, p, passes=passes) for acc, p in zip(accs, ps)]
        k *= 2
    return accs


def _each(fn, *lists):
    return [fn(*args) for args in zip(*lists)]


def _stack_masked(x, c):
    lane_head = lax.broadcasted_iota(jnp.int32, (c, W_MIX), 1) // HEAD_DIM
    return jnp.concatenate([jnp.where(lane_head == h, x, 0.0) for h in range(N_HEADS)], axis=0)


def _stack_heads(x):
    return jnp.concatenate([x[:, h * HEAD_DIM:(h + 1) * HEAD_DIM] for h in range(N_HEADS)], axis=0)


def _unstack_heads(x, c):
    return jnp.concatenate([x[h * c:(h + 1) * c] for h in range(N_HEADS)], axis=1)


def _seq_block(i, tb, *, j, ns, nt, blk0):
    return (blk0 + (i * ns + j) * nt + tb, 0)


def _block_tri(c, strict):
    n = N_HEADS * c
    r = lax.broadcasted_iota(jnp.int32, (n, n), 0)
    q = lax.broadcasted_iota(jnp.int32, (n, n), 1)
    same = (r // c) == (q // c)
    return same & ((r > q) if strict else (r >= q))


def _ln_kernel(x_ref, g_ref, b_ref, o_ref):
    o_ref[...] = _layer_norm(x_ref[...], g_ref[...], b_ref[...], LN_EPS)


def _ln_call(x, g, b):
    n = x.shape[0]
    tm = _pick(n, (1024, 512, 256, 128))
    return pl.pallas_call(
        _ln_kernel, grid=(n // tm,),
        in_specs=[pl.BlockSpec((tm, D_MODEL), lambda i: (i, 0)),
                  pl.BlockSpec((1, D_MODEL), lambda i: (0, 0)),
                  pl.BlockSpec((1, D_MODEL), lambda i: (0, 0))],
        out_specs=pl.BlockSpec((tm, D_MODEL), lambda i: (i, 0)),
        out_shape=jax.ShapeDtypeStruct((n, D_MODEL), F32),
        compiler_params=_params(("parallel",)), name="ln_in",
    )(x, g.reshape(1, -1), b.reshape(1, -1))


def _proj_kernel(x_ref, w_ref, pd_ref, pbg_ref, pb_ref, pa_ref, pc_ref):
    p = _dot(x_ref[...].astype(BF16), w_ref[...])
    o = 0
    for ref, w in ((pd_ref, P_D), (pbg_ref, P_BG), (pb_ref, P_B), (pa_ref, P_A), (pc_ref, P_C)):
        ref[...] = p[:, o:o + w]
        o += w


def _proj_call(x, w_cat):
    n = x.shape[0]
    tm = _pick(n, (512, 256, 128))
    widths = (P_D, P_BG, P_B, P_A, P_C)
    return pl.pallas_call(
        _proj_kernel, grid=(n // tm,),
        in_specs=[pl.BlockSpec((tm, D_MODEL), lambda i: (i, 0)),
                  pl.BlockSpec((D_MODEL, P_TOTAL), lambda i: (0, 0))],
        out_specs=[pl.BlockSpec((tm, w), lambda i: (i, 0)) for w in widths],
        out_shape=[jax.ShapeDtypeStruct((n, w), F32) for w in widths],
        compiler_params=_params(("parallel",)), name="proj_in",
    )(x, w_cat)


def _reorder_w_in(w_in, h_b):
    cols_a = 2 * W_MIX
    cols_b = 3 * W_MIX + 2 * h_b + W_MIX
    o1, o2 = cols_a, cols_a + cols_b
    o3 = o2 + 2 * W_MIX
    wa, wb, wc, wd = w_in[:, :o1], w_in[:, o1:o2], w_in[:, o2:o3], w_in[:, o3:]
    qkv, bg, z = wb[:, :3 * W_MIX], wb[:, 3 * W_MIX:3 * W_MIX + 2 * h_b], wb[:, 3 * W_MIX + 2 * h_b:]
    bg = jnp.pad(bg, ((0, 0), (0, P_BG - 2 * h_b)))
    return jnp.concatenate([wd, bg, qkv, z, wa, wc], axis=1).astype(BF16)


def _sgu_kernel(p_ref, w_ref, b_ref, g_ref, beta_ref, y_ref, v_ref, *, n_chunks):
    w = w_ref[0]
    bias = b_ref[0]
    lane_head = lax.broadcasted_iota(jnp.int32, (SGU_CHUNK, W_MIX), 1) // HEAD_DIM
    for c in range(n_chunks):
        rows = pl.ds(c * SGU_CHUNK, SGU_CHUNK)
        x = p_ref[rows, :]
        h = 0.5 * x * (1.0 + lax.erf(x * (1.0 / math.sqrt(2.0))))
        u = h[:, :W_MIX]
        v = _layer_norm(h[:, W_MIX:], g_ref[...], beta_ref[...], LN_EPS)
        v_ref[rows, :] = v
        vb = jnp.concatenate([jnp.where(lane_head == hh, v, 0.0) for hh in range(N_HEADS)], axis=0)
        s = _dot(w, vb.astype(BF16)) + bias
        y_ref[rows, :] = u * s


def _sgu_weights(sgu_w, sgu_b, t_s):
    causal = jnp.tril(jnp.ones((SGU_CHUNK, SGU_CHUNK), bool))
    wp = jnp.where(causal, sgu_w, 0.0)
    reps = SGU_CHUNK // t_s
    ws = jnp.stack([jnp.kron(jnp.eye(reps, dtype=F32), wp[h, :t_s, :t_s]) for h in range(N_HEADS)])
    cat = lambda w: jnp.concatenate([w[h] for h in range(N_HEADS)], axis=1)
    w_eff = jnp.stack([cat(wp), cat(ws)]).astype(BF16)
    bp = jnp.repeat(sgu_b.T, HEAD_DIM, axis=1)
    bs = jnp.tile(bp[:t_s], (reps, 1))
    return w_eff, jnp.stack([bp, bs])


def _sgu_call(p_a, w_eff, b_eff, ln_g, ln_b, n_prompt_rows):
    n = p_a.shape[0]
    tb = _pick(math.gcd(n_prompt_rows, n - n_prompt_rows), (1024, 512, 256, 128))
    n_prompt_tiles = n_prompt_rows // tb
    grp = lambda i: jnp.minimum(i // n_prompt_tiles, 1)
    return pl.pallas_call(
        functools.partial(_sgu_kernel, n_chunks=tb // SGU_CHUNK), grid=(n // tb,),
        in_specs=[pl.BlockSpec((tb, P_A), lambda i: (i, 0)),
                  pl.BlockSpec((1, SGU_CHUNK, N_HEADS * SGU_CHUNK), lambda i: (grp(i), 0, 0)),
                  pl.BlockSpec((1, SGU_CHUNK, W_MIX), lambda i: (grp(i), 0, 0)),
                  pl.BlockSpec((1, W_MIX), lambda i: (0, 0)),
                  pl.BlockSpec((1, W_MIX), lambda i: (0, 0))],
        out_specs=[pl.BlockSpec((tb, W_MIX), lambda i: (i, 0))] * 2,
        out_shape=[jax.ShapeDtypeStruct((n, W_MIX), F32)] * 2,
        compiler_params=_params(("parallel",)), name="sgu",
    )(p_a, w_eff, b_eff, ln_g.reshape(1, -1), ln_b.reshape(1, -1))


CC_HDR = 32


def _cc_kernel(p_ref, buf_ref, w_ref, wb_ref, g_ref, b_ref, y_ref, nb_ref, xp_ref, *, sb, t, tt):
    hist = CC_WIDTH - 1
    ones = _head_ones().astype(BF16)
    w = w_ref[...]
    for s in range(sb):
        x = p_ref[pl.ds(s * t, t), :]
        xp_ref[pl.ds(CC_HDR - hist, hist), :] = buf_ref[s]
        xp_ref[pl.ds(CC_HDR, t), :] = x[:, :W_MIX] * _sigmoid(x[:, W_MIX:])
        nb_ref[s] = xp_ref[pl.ds(t + CC_HDR - hist, hist), :]

        def tile(i, carry):
            base = pl.multiple_of(i * tt, SUBLANES)
            win = xp_ref[pl.ds(base, tt + CC_HDR), :]
            acc = jnp.zeros((tt, W_MIX), F32)
            shifted = [win[b:] for b in range(SUBLANES)]
            for j in range(CC_WIDTH):
                o = j + CC_HDR - hist
                a8 = (o // SUBLANES) * SUBLANES
                acc = acc + shifted[o % SUBLANES][a8:a8 + tt] * w[j:j + 1]
            hh = acc + wb_ref[...]
            mean = _mm_exact_rhs(hh, ones, pieces=2) * (1.0 / HEAD_DIM)
            xc = hh - mean
            var = _mm_exact_rhs(xc * xc, ones, pieces=2) * (1.0 / HEAD_DIM)
            yy = xc * lax.rsqrt(var + LN_EPS) * g_ref[...] + b_ref[...]
            y_ref[pl.ds(pl.multiple_of(s * t + base, SUBLANES), tt), :] = _silu(yy)
            return carry

        lax.fori_loop(0, t // tt, tile, 0)


def _cc_call(p_c, buf, w, wb, g, b, row0, t):
    nseq = buf.shape[0]
    sb = 1 if t >= 256 else _pick(nseq, (16, 8, 4, 2, 1))
    tt = min(t, 256)
    rows = sb * t
    blk0 = row0 // rows
    assert row0 % rows == 0 and t % tt == 0
    kern = functools.partial(_cc_kernel, sb=sb, t=t, tt=tt)
    in_specs = [pl.BlockSpec((rows, P_C), lambda i: (blk0 + i, 0)),
                pl.BlockSpec((sb, CC_WIDTH - 1, W_MIX), lambda i: (i, 0, 0)),
                pl.BlockSpec((CC_WIDTH, W_MIX), lambda i: (0, 0)),
                pl.BlockSpec((1, W_MIX), lambda i: (0, 0)),
                pl.BlockSpec((1, W_MIX), lambda i: (0, 0)),
                pl.BlockSpec((1, W_MIX), lambda i: (0, 0))]
    args = [p_c, buf, w, wb.reshape(1, -1), g.reshape(1, -1), b.reshape(1, -1)]
    return pl.pallas_call(
        kern, grid=(nseq // sb,), in_specs=in_specs,
        out_specs=[pl.BlockSpec((rows, W_MIX), lambda i: (i, 0)),
                   pl.BlockSpec((sb, CC_WIDTH - 1, W_MIX), lambda i: (i, 0, 0))],
        out_shape=[jax.ShapeDtypeStruct((nseq * t, W_MIX), F32),
                   jax.ShapeDtypeStruct((nseq, CC_WIDTH - 1, W_MIX), F32)],
        scratch_shapes=[pltpu.VMEM((t + CC_HDR, W_MIX), F32)],
        compiler_params=_params(("arbitrary",)), name="cc",
    )(*args)


GDN_HDR = 8
GDN_QKV = 3 * W_MIX


def _lane_expand(src_lane0):
    r = lax.broadcasted_iota(jnp.int32, (LANES, W_MIX), 0)
    c = lax.broadcasted_iota(jnp.int32, (LANES, W_MIX), 1) // HEAD_DIM
    return (r == c + src_lane0).astype(F32)


def _gdn_kernel(*refs, ns, tblk, c):
    p_refs, bg_refs = refs[:ns], refs[ns:2 * ns]
    buf_ref, s0_ref, cw_ref, alog_ref, dt_ref, ng_ref, y_ref, nb_ref, s_ref, hdr_ref = refs[2 * ns:]
    hist = GDN_CONV - 1
    hc = N_HEADS * c

    @pl.when(pl.program_id(1) == 0)
    def _():
        nb_ref[...] = buf_ref[...]
        s_ref[...] = s0_ref[...]
        hdr_ref[...] = jnp.zeros_like(hdr_ref)

    ones = _head_ones().astype(BF16)
    e_beta = _lane_expand(0).astype(BF16)
    e_g = _lane_expand(N_HEADS).astype(BF16)
    tri_ones = _tri(c, False).astype(BF16)
    strict_bd = _block_tri(c, True)
    incl_bd = _block_tri(c, False)
    eye_hc = (lax.broadcasted_iota(jnp.int32, (hc, hc), 0) == lax.broadcasted_iota(jnp.int32, (hc, hc), 1))
    eye_w = (lax.broadcasted_iota(jnp.int32, (W_MIX, W_MIX), 0) == lax.broadcasted_iota(jnp.int32, (W_MIX, W_MIX), 1))
    first_lane = (lax.broadcasted_iota(jnp.int32, (W_MIX, hc), 0) % HEAD_DIM == 0).astype(BF16)
    ones_hc = jnp.ones((hc, hc), BF16)
    ones_wv = jnp.ones((W_MIX, HEAD_DIM), BF16)
    cw = cw_ref[...]
    neg_a = -jnp.exp(alog_ref[...])
    mm = functools.partial(_mm, passes=GDN_PASSES)

    def chunk(n, carries):
        rows = pl.ds(pl.multiple_of(n * c, SUBLANES), c)
        sm = functools.partial(_stack_masked, c=c)
        cat = jnp.concatenate
        tails = [cr[0] for cr in carries]
        sts = [cr[1] for cr in carries]
        wins = [cat([tails[j], p_refs[j][rows, :GDN_QKV]], axis=0) for j in range(ns)]

        def conv_act(win):
            conv = jnp.zeros((c, GDN_QKV), F32)
            for tap in range(GDN_CONV):
                o = tap + GDN_HDR - hist
                conv = conv + win[o:o + c] * cw[tap:tap + 1]
            return _silu(conv)

        acts = _each(conv_act, wins)
        qs = [a[:, :W_MIX] for a in acts]
        ks = [a[:, W_MIX:2 * W_MIX] for a in acts]
        vs = [a[:, 2 * W_MIX:] for a in acts]
        sqs = _each(lambda q, k: _mm_exact_rhs(cat([q * q, k * k], axis=0), ones, pieces=2), qs, ks)
        qs = _each(lambda q, sq: q * lax.rsqrt(sq[:c] + 1e-6) * (HEAD_DIM ** -0.5), qs, sqs)
        ks = _each(lambda k, sq: k * lax.rsqrt(sq[c:] + 1e-6), ks, sqs)
        bgs = [bg_refs[j][rows, :] for j in range(ns)]
        betas = _each(lambda bg: _mm_exact_rhs(_sigmoid(bg), e_beta, pieces=2), bgs)
        gsums = _each(lambda bg: _mm_exact_lhs(tri_ones, neg_a * _softplus(bg + dt_ref[...])), bgs)
        gcums = _each(lambda gs: _mm_exact_rhs(gs, e_g), gsums)
        kbs = _each(lambda k, b: k * b, ks, betas)
        vbs = _each(lambda v, b: v * b, vs, betas)
        egcs = _each(jnp.exp, gcums)
        glasts = [gc[c - 1:c, :] for gc in gcums]
        kdecs = _each(lambda k, gl, gc: k * jnp.exp(gl - gc), ks, glasts, gcums)

        gcols = _each(lambda gc: _mm_exact_rhs(sm(gc), first_lane), gcums)
        grows = _each(lambda gcol: _mm_exact_lhs(ones_hc, jnp.where(eye_hc, gcol, 0.0)), gcols)
        decays = _each(lambda gcol, grow: jnp.where(incl_bd, jnp.exp(gcol - grow), 0.0), gcols, grows)
        prods = _each(lambda kb, q, k: mm(cat([sm(kb), sm(q)], axis=0), sm(k), _NT), kbs, qs, ks)
        lms = _each(lambda pr, dec: jnp.where(strict_bd, pr[:hc] * dec, 0.0), prods, decays)
        aqks = _each(lambda pr, dec: pr[hc:] * dec, prods, decays)
        tinvs = _block_neumann_inverse([-lm for lm in lms], c, GDN_PASSES)
        uws = _each(lambda ti, vb, kb, egc: mm(ti, cat([_stack_heads(vb), sm(kb * egc)], axis=1)),
                    tinvs, vbs, kbs, egcs)
        wss = _each(lambda uw, q, egc, st: mm(cat([uw[:, HEAD_DIM:], sm(q * egc)], axis=0), st),
                    uws, qs, egcs, sts)
        vnews = _each(lambda uw, ws: uw[:, :HEAD_DIM] - ws[:hc], uws, wss)
        outs = _each(lambda ws, aqk, vn: _unstack_heads(ws[hc:] + mm(aqk, vn), c), wss, aqks, vnews)
        grs = _each(lambda gl: _mm_exact_rhs(jnp.where(eye_w, jnp.exp(gl), 0.0), ones_wv), glasts)
        sts = _each(lambda st, gr, kd, vn: st * gr + mm(sm(kd), vn, _TN), sts, grs, kdecs, vnews)
        outs = _each(lambda o: o * lax.rsqrt(_mm_exact_rhs(o * o, ones, pieces=2) * (1.0 / HEAD_DIM) + RMS_EPS)
                     * ng_ref[...], outs)
        for j in range(ns):
            y_ref[j, rows, :] = outs[j] * _silu(p_refs[j][rows, GDN_QKV:])
        return tuple((wins[j][c:c + GDN_HDR], sts[j]) for j in range(ns))

    init = []
    for j in range(ns):
        hdr_ref[j, pl.ds(GDN_HDR - hist, hist), :] = nb_ref[j]
        init.append((hdr_ref[j], jnp.concatenate([s_ref[j, h] for h in range(N_HEADS)], axis=0)))
    fin = lax.fori_loop(0, tblk // c, chunk, tuple(init))
    for j in range(ns):
        tail, st = fin[j]
        nb_ref[j] = tail[GDN_HDR - hist:]
        for h in range(N_HEADS):
            s_ref[j, h] = st[h * HEAD_DIM:(h + 1) * HEAD_DIM]


def _gdn_call(p_b, p_bg, buf, s0, conv_w, a_log, dt_bias, norm_g, row0, t):
    nseq = buf.shape[0]
    h_b = a_log.shape[0]
    assert h_b == N_HEADS
    c = math.gcd(t, GDN_CHUNK)
    ns = _pick(nseq, (SEQ_GROUP, 2, 1))
    tblk = min(t, SEQ_TBLK)
    nt = t // tblk
    blk0 = row0 // tblk
    assert row0 % tblk == 0 and t % tblk == 0 and tblk % c == 0 and c % SUBLANES == 0
    lane_pad = lambda x: jnp.pad(x.reshape(1, -1), ((0, 0), (h_b, LANES - 2 * h_b)))
    seq_rows = lambda w: [pl.BlockSpec((tblk, w), functools.partial(_seq_block, j=j, ns=ns, nt=nt, blk0=blk0))
                          for j in range(ns)]
    kern = functools.partial(_gdn_kernel, ns=ns, tblk=tblk, c=c)
    in_specs = seq_rows(P_B) + seq_rows(P_BG) + [
        pl.BlockSpec((ns, GDN_CONV - 1, GDN_QKV), lambda i, tb: (i, 0, 0)),
        pl.BlockSpec((ns, N_HEADS, HEAD_DIM, HEAD_DIM), lambda i, tb: (i, 0, 0, 0)),
        pl.BlockSpec((GDN_CONV, GDN_QKV), lambda i, tb: (0, 0)),
        pl.BlockSpec((1, LANES), lambda i, tb: (0, 0)),
        pl.BlockSpec((1, LANES), lambda i, tb: (0, 0)),
        pl.BlockSpec((1, W_MIX), lambda i, tb: (0, 0))]
    args = [p_b] * ns + [p_bg] * ns + [buf, s0, conv_w, lane_pad(a_log), lane_pad(dt_bias),
                                       jnp.tile(norm_g, N_HEADS).reshape(1, -1)]
    y, nb, st = pl.pallas_call(
        kern, grid=(nseq // ns, nt), in_specs=in_specs,
        out_specs=[pl.BlockSpec((ns, tblk, W_MIX), lambda i, tb: (i, tb, 0)),
                   pl.BlockSpec((ns, GDN_CONV - 1, GDN_QKV), lambda i, tb: (i, 0, 0)),
                   pl.BlockSpec((ns, N_HEADS, HEAD_DIM, HEAD_DIM), lambda i, tb: (i, 0, 0, 0))],
        out_shape=[jax.ShapeDtypeStruct((nseq, t, W_MIX), F32),
                   jax.ShapeDtypeStruct((nseq, GDN_CONV - 1, GDN_QKV), F32),
                   jax.ShapeDtypeStruct((nseq, N_HEADS, HEAD_DIM, HEAD_DIM), F32)],
        scratch_shapes=[pltpu.VMEM((ns, GDN_HDR, GDN_QKV), F32)],
        compiler_params=_params(("arbitrary", "arbitrary")), name="gdn",
    )(*args)
    return y.reshape(nseq * t, W_MIX), nb, st


def _rwkv_kernel(*refs, ns, tblk, c):
    p_refs = refs[:ns]
    (sh_ref, s0_ref, mu_ref, w0_ref, a0_ref, lora_ref, kk_ref, ka_ref, rk_ref, g_ref, b_ref,
     y_ref, sho_ref, s_ref) = refs[ns:]
    hc = N_HEADS * c
    ones = _head_ones().astype(BF16)
    tri_ones = _tri(c, False).astype(BF16)
    strict_bd = _block_tri(c, True)
    incl_bd = _block_tri(c, False)
    lane = lax.broadcasted_iota(jnp.int32, (c, LANES), 1)
    row_id = lax.broadcasted_iota(jnp.int32, (c, COLS_D), 0)
    mm = functools.partial(_mm, passes=RWKV_PASSES)

    @pl.when(pl.program_id(1) == 0)
    def _():
        sho_ref[...] = sh_ref[...]
        s_ref[...] = s0_ref[...]

    def chunk(n, carries):
        rows = pl.ds(pl.multiple_of(n * c, SUBLANES), c)
        sm = functools.partial(_stack_masked, c=c)
        cat = jnp.concatenate
        prev_rows = [cr[0] for cr in carries]
        sts = [cr[1] for cr in carries]
        xs = [p_refs[j][rows, :] for j in range(ns)]
        xls = _each(lambda x, pr: x + (jnp.where(row_id == 0, pr, pltpu.roll(x, 1, 0)) - x) * mu_ref[...],
                    xs, prev_rows)
        rs = [xl[:, :W_MIX] for xl in xls]
        ks = [xl[:, W_MIX:2 * W_MIX] for xl in xls]
        vs = [xl[:, 2 * W_MIX:3 * W_MIX] for xl in xls]

        def lora_act(xl):
            lo = xl[:, 3 * W_MIX:]
            return jnp.where(lane < LORA_W, jnp.tanh(lo), jnp.where(lane < LORA_W + LORA_A, lo, _sigmoid(lo)))

        loras = _each(lambda xl: _mm(lora_act(xl), lora_ref[...], passes=3), xls)
        lws = _each(lambda lr: -jnp.exp(-_softplus(-(w0_ref[...] + lr[:, :W_MIX])) - 0.5), loras)
        a_s = _each(lambda lr: _sigmoid(a0_ref[...] + lr[:, W_MIX:2 * W_MIX]), loras)
        gs = [lr[:, 2 * W_MIX:] for lr in loras]
        kkps = _each(lambda k: k * kk_ref[...], ks)
        k2s = _each(lambda k, a: k * (1.0 + (a - 1.0) * ka_ref[...]), ks, a_s)
        sums = _each(lambda kkp, r, k2: _mm_exact_rhs(cat([kkp * kkp, r * k2 * rk_ref[...]], axis=0), ones, pieces=2),
                     kkps, rs, k2s)
        kks = _each(lambda kkp, sm_: kkp * lax.rsqrt(sm_[:c] + 1e-6), kkps, sums)
        bonuses = [sm_[c:] for sm_ in sums]
        cums = _each(lambda lw: _mm_exact_lhs(tri_ones, lw, pieces=2), lws)
        invs = _each(lambda cum: jnp.exp(-cum), cums)
        a_hats = _each(lambda kk, cum, lw: -kk * jnp.exp(cum - lw), kks, cums, lws)
        b_hats = _each(lambda kk, a, inv: kk * a * inv, kks, a_s, invs)
        c_hats = _each(lambda k2, inv: k2 * inv, k2s, invs)
        q_hats = _each(lambda r, cum: r * jnp.exp(cum), rs, cums)
        gam_cs = [jnp.exp(cum[c - 1:c, :]) for cum in cums]

        xaqs = _each(lambda ah, qh: cat([sm(ah), sm(qh)], axis=0), a_hats, q_hats)
        bcss = _each(lambda bh, ch: cat([sm(bh), sm(ch)], axis=0), b_hats, c_hats)
        prods = _each(lambda xaq, bcs: mm(xaq, bcs, _NT), xaqs, bcss)
        a_ms = [jnp.where(strict_bd, pr[:hc, :hc], 0.0) for pr in prods]
        b_ms = [jnp.where(strict_bd, pr[:hc, hc:], 0.0) for pr in prods]
        p_qs = [cat([jnp.where(incl_bd, pr[hc:, :hc], 0.0), jnp.where(incl_bd, pr[hc:, hc:], 0.0)], axis=1)
                for pr in prods]
        tinvs = _block_neumann_inverse(a_ms, c, RWKV_PASSES)
        vss = _each(_stack_heads, vs)
        zos = _each(lambda xaq, st: mm(xaq, st, _NT), xaqs, sts)
        bvs = _each(lambda bm, v_: mm(bm, v_), b_ms, vss)
        zs = _each(lambda ti, zo, bv: mm(ti, zo[:hc] + bv), tinvs, zos, bvs)
        zvs = _each(lambda z, v_: cat([z, v_], axis=0), zs, vss)
        outs = _each(lambda zo, pq, zv: zo[hc:] + mm(pq, zv), zos, p_qs, zvs)
        sts = _each(lambda st, zv, bcs, gc: (st + mm(zv, bcs, _TN)) * gc, sts, zvs, bcss, gam_cs)

        ys = _each(lambda o: _unstack_heads(o, c), outs)
        ycs = _each(lambda y: y - _mm_exact_rhs(y, ones, pieces=2) * (1.0 / HEAD_DIM), ys)
        yns = _each(lambda yc: yc * lax.rsqrt(_mm_exact_rhs(yc * yc, ones, pieces=2) * (1.0 / HEAD_DIM) + GN_EPS)
                    * g_ref[...] + b_ref[...], ycs)
        for j in range(ns):
            y_ref[j, rows, :] = (yns[j] + bonuses[j] * vs[j]) * gs[j]
        return tuple((xs[j][c - 1:c, :], sts[j]) for j in range(ns))

    init = tuple((sho_ref[j], jnp.concatenate([s_ref[j, h] for h in range(N_HEADS)], axis=1)) for j in range(ns))
    fin = lax.fori_loop(0, tblk // c, chunk, init)
    for j in range(ns):
        last_row, st = fin[j]
        sho_ref[j] = last_row
        for h in range(N_HEADS):
            s_ref[j, h] = st[:, h * HEAD_DIM:(h + 1) * HEAD_DIM]


def _rwkv_lora_weights(w2, a2, g2):
    m = jnp.zeros((LANES, 3 * W_MIX), F32)
    m = m.at[:LORA_W, :W_MIX].set(w2)
    m = m.at[LORA_W:LORA_W + LORA_A, W_MIX:2 * W_MIX].set(a2)
    return m.at[LORA_W + LORA_A:, 2 * W_MIX:].set(g2)


def _rwkv_call(p_d, shift, s0, mu, w0, a0, lora_w, k_k, k_a, r_k, ln_g, ln_b, row0, t):
    nseq = shift.shape[0]
    c = math.gcd(t, RWKV_CHUNK)
    ns = _pick(nseq, (SEQ_GROUP, 2, 1))
    tblk = min(t, SEQ_TBLK)
    nt = t // tblk
    blk0 = row0 // tblk
    assert row0 % tblk == 0 and t % tblk == 0 and tblk % c == 0 and c % SUBLANES == 0
    row = lambda x: x.reshape(1, -1)
    vec = lambda w: pl.BlockSpec((1, w), lambda i, tb: (0, 0))
    kern = functools.partial(_rwkv_kernel, ns=ns, tblk=tblk, c=c)
    in_specs = [pl.BlockSpec((tblk, P_D), functools.partial(_seq_block, j=j, ns=ns, nt=nt, blk0=blk0))
                for j in range(ns)] + [
        pl.BlockSpec((ns, 1, COLS_D), lambda i, tb: (i, 0, 0)),
        pl.BlockSpec((ns, N_HEADS, HEAD_DIM, HEAD_DIM), lambda i, tb: (i, 0, 0, 0)),
        vec(COLS_D), vec(W_MIX), vec(W_MIX),
        pl.BlockSpec((LANES, 3 * W_MIX), lambda i, tb: (0, 0)),
        vec(W_MIX), vec(W_MIX), vec(W_MIX), vec(W_MIX), vec(W_MIX)]
    args = [p_d] * ns + [shift.reshape(nseq, 1, COLS_D), s0, row(mu), row(w0), row(a0), lora_w,
                         row(k_k), row(k_a), row(r_k), row(ln_g), row(ln_b)]
    y, sh, st = pl.pallas_call(
        kern, grid=(nseq // ns, nt), in_specs=in_specs,
        out_specs=[pl.BlockSpec((ns, tblk, W_MIX), lambda i, tb: (i, tb, 0)),
                   pl.BlockSpec((ns, 1, COLS_D), lambda i, tb: (i, 0, 0)),
                   pl.BlockSpec((ns, N_HEADS, HEAD_DIM, HEAD_DIM), lambda i, tb: (i, 0, 0, 0))],
        out_shape=[jax.ShapeDtypeStruct((nseq, t, W_MIX), F32),
                   jax.ShapeDtypeStruct((nseq, 1, COLS_D), F32),
                   jax.ShapeDtypeStruct((nseq, N_HEADS, HEAD_DIM, HEAD_DIM), F32)],
        compiler_params=_params(("arbitrary", "arbitrary")), name="rwkv",
    )(*args)
    return y.reshape(nseq * t, W_MIX), sh.reshape(nseq, COLS_D), st


ROUTE_IDX, ROUTE_GATE, ROUTE_RANK = 0, TOP_K, 2 * TOP_K


def _outproj_router_kernel(*refs, tm, alpha, n_prompt_tiles):
    yp_refs, ys_refs = refs[0:4], refs[4:8]
    x_ref, wo_ref, g_ref, b_ref, rw_ref, rb_ref, x1_ref, route_ref, cnt_ref, tri_ref = refs[8:]

    @pl.when(pl.program_id(0) == 0)
    def _():
        cnt_ref[...] = jnp.zeros_like(cnt_ref)
        tri_ref[...] = _tri(tm, True).astype(BF16)

    is_prompt = pl.program_id(0) < n_prompt_tiles
    mix = jnp.zeros((tm, D_MODEL), F32)
    for i in range(4):
        y = jnp.where(is_prompt, yp_refs[i][...], ys_refs[i][...])
        mix = mix + _dot(y.astype(BF16), wo_ref[pl.ds(i * W_MIX, W_MIX), :])
    x1 = _layer_norm(alpha * x_ref[...] + mix, g_ref[...], b_ref[...], LN_EPS)
    x1_ref[...] = x1

    logits = _mm(x1, rw_ref[...], passes=3) + rb_ref[...]
    lane = lax.broadcasted_iota(jnp.int32, (tm, LANES), 1)
    lane_f = lane.astype(F32)
    work = logits
    vals, hots, ids = [], [], []
    for _ in range(TOP_K):
        m = jnp.max(work, axis=-1, keepdims=True)
        idx = jnp.min(jnp.where(work == m, lane_f, float(LANES)), axis=-1, keepdims=True)
        hot = lane_f == idx
        vals.append(m)
        hots.append(hot)
        ids.append(idx)
        work = jnp.where(hot, -jnp.inf, work)
    exps = [jnp.exp(v - vals[0]) for v in vals]
    denom = exps[0] + exps[1] + exps[2] + exps[3]

    any_hot = jnp.zeros((tm, LANES), F32)
    for hot in hots:
        any_hot = any_hot + hot.astype(F32)
    before = _dot(tri_ref[...], any_hot.astype(BF16)) + cnt_ref[...]
    cnt_ref[...] = cnt_ref[...] + jnp.sum(any_hot, axis=0, keepdims=True)

    route = jnp.zeros((tm, LANES), F32)
    for kk in range(TOP_K):
        rank = jnp.sum(jnp.where(hots[kk], before, 0.0), axis=-1, keepdims=True)
        route = jnp.where(lane == ROUTE_IDX + kk, ids[kk], route)
        route = jnp.where(lane == ROUTE_GATE + kk, exps[kk] / denom, route)
        route = jnp.where(lane == ROUTE_RANK + kk, rank, route)
    route_ref[...] = route


def _outproj_router_call(ys_prompt, ys_sample, x, w_out, ln_g, ln_b, router_w, router_b, alpha):
    n = x.shape[0]
    n_p, n_s = ys_prompt[-1].shape[0], ys_sample[-1].shape[0]
    tm = _pick(math.gcd(n_p, n_s), (256, 128))
    npt = n_p // tm
    pmap = lambda y: (lambda i: (i, 0)) if y.shape[0] == n else (lambda i: (jnp.minimum(i, npt - 1), 0))
    smap = lambda y: (lambda i: (i, 0)) if y.shape[0] == n else (lambda i: (jnp.maximum(i - npt, 0), 0))
    row = lambda v: v.reshape(1, -1)
    vec = lambda w: pl.BlockSpec((1, w), lambda i: (0, 0))
    rw = jnp.pad(router_w, ((0, 0), (0, LANES - N_EXPERTS)))
    rb = jnp.pad(router_b, (0, LANES - N_EXPERTS), constant_values=NEG_BIG)
    return pl.pallas_call(
        functools.partial(_outproj_router_kernel, tm=tm, alpha=alpha, n_prompt_tiles=npt), grid=(n // tm,),
        in_specs=[pl.BlockSpec((tm, W_MIX), pmap(y)) for y in ys_prompt] + [
            pl.BlockSpec((tm, W_MIX), smap(y)) for y in ys_sample] + [
            pl.BlockSpec((tm, D_MODEL), lambda i: (i, 0)),
            pl.BlockSpec((D_MODEL, D_MODEL), lambda i: (0, 0)),
            vec(D_MODEL), vec(D_MODEL),
            pl.BlockSpec((D_MODEL, LANES), lambda i: (0, 0)), vec(LANES)],
        out_specs=[pl.BlockSpec((tm, D_MODEL), lambda i: (i, 0)),
                   pl.BlockSpec((tm, LANES), lambda i: (i, 0)),
                   pl.BlockSpec((1, LANES), lambda i: (0, 0))],
        out_shape=[jax.ShapeDtypeStruct((n, D_MODEL), F32),
                   jax.ShapeDtypeStruct((n, LANES), F32),
                   jax.ShapeDtypeStruct((1, LANES), F32)],
        scratch_shapes=[pltpu.VMEM((tm, tm), BF16)],
        compiler_params=_params(("arbitrary",)), name="outproj_router",
    )(*ys_prompt, *ys_sample, x, w_out, row(ln_g), row(ln_b), rw, row(rb))


MOE_TB = 256


def _moe_plan(route, counts, n):
    e_idx = route[:, ROUTE_IDX:ROUTE_IDX + TOP_K].astype(jnp.int32)
    rank = route[:, ROUTE_RANK:ROUTE_RANK + TOP_K].astype(jnp.int32)
    cnt = counts[0, :N_EXPERTS].astype(jnp.int32)
    padded = (cnt + MOE_TB - 1) // MOE_TB * MOE_TB
    pad_end = jnp.cumsum(padded)
    pad_start = pad_end - padded
    dest = (pad_start[e_idx] + rank).reshape(n * TOP_K)
    n_blocks = -(-n * TOP_K // MOE_TB) + N_EXPERTS
    n_used = pad_end[-1] // MOE_TB
    blk = jnp.minimum(jnp.arange(n_blocks), n_used - 1) * MOE_TB
    block_e = jnp.minimum(jnp.sum(pad_end[None, :] <= blk[:, None], axis=1), N_EXPERTS - 1).astype(jnp.int32)
    last_block_row = jnp.where(padded > 0, pad_end - MOE_TB, -1)
    tail = n_used + jnp.arange(N_EXPERTS)
    tail_row = jnp.where(tail < n_blocks, tail * MOE_TB, -1)
    zero_rows = jnp.concatenate([last_block_row, tail_row]).astype(jnp.int32)
    return dest, block_e, n_used.reshape(1).astype(jnp.int32), zero_rows, n_blocks


ROW_TILE = D_MODEL // LANES


def _store_row_tiles(ref, x):
    for s in range(ROW_TILE):
        ref[pl.ds(s, x.shape[0], stride=ROW_TILE), :] = x[:, s * LANES:(s + 1) * LANES]


def _load_row_tiles(ref, rows):
    return jnp.concatenate([ref[pl.ds(s, rows, stride=ROW_TILE), :] for s in range(ROW_TILE)], axis=1)


def _tile_rows(row):
    return pl.ds(pl.multiple_of(row * ROW_TILE, ROW_TILE), ROW_TILE)


def _dispatch_kernel(zrow_ref, dest_ref, x_ref, xs_ref, zbuf_ref, xbuf_ref, zsem, sems, *, tm, n_tiles):
    step = pl.program_id(0)
    slot = step % 2

    def zero_copy(e):
        row = pl.multiple_of(jnp.maximum(zrow_ref[e], 0) * ROW_TILE, MOE_TB * ROW_TILE)
        return pltpu.make_async_copy(zbuf_ref, xs_ref.at[pl.ds(row, MOE_TB * ROW_TILE)], zsem)

    @pl.when(pl.program_id(0) == 0)
    def _():
        zbuf_ref[...] = jnp.zeros_like(zbuf_ref)
        for e in range(2 * N_EXPERTS):
            @pl.when(zrow_ref[e] >= 0)
            def _():
                zero_copy(e).start()
        for e in range(2 * N_EXPERTS):
            @pl.when(zrow_ref[e] >= 0)
            def _():
                zero_copy(e).wait()

    def row_copy(s, t, dst_row):
        return pltpu.make_async_copy(xbuf_ref.at[s, _tile_rows(t)], xs_ref.at[_tile_rows(dst_row)], sems.at[s])

    def issue(t, c):
        for kk in range(TOP_K):
            row_copy(slot, t, dest_ref[t * TOP_K + kk]).start(priority=kk % 2)
        return c

    def drain(s):
        def body(t, c):
            for kk in range(TOP_K):
                row_copy(s, 0, 0).wait()
            return c
        lax.fori_loop(0, tm, body, 0)

    _store_row_tiles(xbuf_ref.at[slot], x_ref[...])
    lax.fori_loop(0, tm, issue, 0)

    @pl.when(step > 0)
    def _():
        drain(1 - slot)

    @pl.when(step == n_tiles - 1)
    def _():
        drain(slot)


def _dispatch_call(x1, dest, last_block_row, n_blocks):
    n = x1.shape[0]
    tm = _pick(n, (256, 128))
    return pl.pallas_call(
        functools.partial(_dispatch_kernel, tm=tm, n_tiles=n // tm),
        grid_spec=pltpu.PrefetchScalarGridSpec(
            num_scalar_prefetch=1, grid=(n // tm,),
            in_specs=[pl.BlockSpec((tm * TOP_K,), lambda i, z: (i,), memory_space=pltpu.SMEM),
                      pl.BlockSpec((tm, D_MODEL), lambda i, z: (i, 0))],
            out_specs=pl.BlockSpec(memory_space=pl.ANY),
            scratch_shapes=[pltpu.VMEM((MOE_TB * ROW_TILE, LANES), F32), pltpu.VMEM((2, tm * ROW_TILE, LANES), F32),
                            pltpu.SemaphoreType.DMA(()), pltpu.SemaphoreType.DMA((2,))]),
        out_shape=jax.ShapeDtypeStruct((n_blocks * MOE_TB * ROW_TILE, LANES), F32),
        compiler_params=_params(("arbitrary",)), name="moe_dispatch",
    )(last_block_row, dest, x1)


PAIR_GROUP = 2 * LANES


def _regroup_bias(b1):
    e, f2 = b1.shape
    return b1.reshape(e, f2 // PAIR_GROUP, LANES, 2).swapaxes(2, 3).reshape(e, 1, f2)


def _expert_kernel(be_ref, nu_ref, x_ref, w1_ref, b1_ref, w2_ref, b2_ref, y_ref, w1s_ref, w2s_ref):
    step = pl.program_id(0)
    live = step < nu_ref[0]
    new_expert = (step == 0) | (be_ref[step] != be_ref[jnp.maximum(step - 1, 0)])

    @pl.when(live & new_expert)
    def _():
        r = lax.broadcasted_iota(jnp.int32, (PAIR_GROUP, PAIR_GROUP), 0)
        c = lax.broadcasted_iota(jnp.int32, (PAIR_GROUP, PAIR_GROUP), 1)
        perm = (r == jnp.where(c < LANES, 2 * c, 2 * (c - LANES) + 1)).astype(BF16)
        for g in range(2 * D_FF // PAIR_GROUP):
            cols = pl.ds(g * PAIR_GROUP, PAIR_GROUP)
            w1s_ref[:, cols] = _dot(w1_ref[0, :, cols].astype(BF16), perm).astype(BF16)
        w2s_ref[...] = w2_ref[0].astype(BF16)

    @pl.when(live)
    def _():
        x = _load_row_tiles(x_ref, MOE_TB)
        h = _dot(x.astype(BF16), w1s_ref[...]) + b1_ref[0]
        acts = []
        for g in range(2 * D_FF // PAIR_GROUP):
            hg = jnp.minimum(h[:, g * PAIR_GROUP:g * PAIR_GROUP + LANES], SWIGLU_LIMIT)
            hl = jnp.clip(h[:, g * PAIR_GROUP + LANES:(g + 1) * PAIR_GROUP], -SWIGLU_LIMIT, SWIGLU_LIMIT)
            acts.append((hg * _sigmoid(SWIGLU_ALPHA * hg) * (hl + 1.0)).astype(BF16))
        _store_row_tiles(y_ref, _dot(jnp.concatenate(acts, axis=1), w2s_ref[...]) + b2_ref[0])

    @pl.when(jnp.logical_not(live))
    def _():
        y_ref[...] = jnp.zeros_like(y_ref)


def _expert_call(xs, block_e, n_used, w1, b1, w2, b2, e0):
    n_blocks = xs.shape[0] // (MOE_TB * ROW_TILE)
    xmap = lambda i, be, nu: (jnp.minimum(i, nu[0] - 1), 0)
    emap3 = lambda i, be, nu: (e0 + be[i], 0, 0)
    return pl.pallas_call(
        _expert_kernel,
        grid_spec=pltpu.PrefetchScalarGridSpec(
            num_scalar_prefetch=2, grid=(n_blocks,),
            in_specs=[pl.BlockSpec((MOE_TB * ROW_TILE, LANES), xmap),
                      pl.BlockSpec((1, D_MODEL, 2 * D_FF), emap3), pl.BlockSpec((1, 1, 2 * D_FF), emap3),
                      pl.BlockSpec((1, D_FF, D_MODEL), emap3), pl.BlockSpec((1, 1, D_MODEL), emap3)],
            out_specs=pl.BlockSpec((MOE_TB * ROW_TILE, LANES), lambda i, be, nu: (i, 0)),
            scratch_shapes=[pltpu.VMEM((D_MODEL, 2 * D_FF), BF16), pltpu.VMEM((D_FF, D_MODEL), BF16)]),
        out_shape=jax.ShapeDtypeStruct(xs.shape, F32),
        compiler_params=_params(("arbitrary",)), name="moe_experts",
    )(block_e, n_used, xs, w1, b1, w2, b2)


def _combine_kernel(dest_ref, dnext_ref, route_ref, x1_ref, ys_ref, g_ref, b_ref, x2_ref, buf_ref, sems,
                    *, tm, alpha, n_tiles):
    step = pl.program_id(0)
    slot = step % 2

    def row_copy(s, t, kk, src_row):
        return pltpu.make_async_copy(ys_ref.at[_tile_rows(src_row)], buf_ref.at[s, kk, _tile_rows(t)], sems.at[s])

    def issue(dref, s):
        def body(t, c):
            for kk in range(TOP_K):
                row_copy(s, t, kk, dref[t * TOP_K + kk]).start(priority=kk % 2)
            return c
        lax.fori_loop(0, tm, body, 0)

    @pl.when(step == 0)
    def _():
        issue(dest_ref, 0)

    @pl.when(step + 1 < n_tiles)
    def _():
        issue(dnext_ref, 1 - slot)

    def drain(t, c):
        for kk in range(TOP_K):
            row_copy(slot, 0, kk, 0).wait()
        return c

    lax.fori_loop(0, tm, drain, 0)
    route = route_ref[...]
    f = jnp.zeros((tm, D_MODEL), F32)
    for kk in range(TOP_K):
        f = f + _load_row_tiles(buf_ref.at[slot, kk], tm) * route[:, ROUTE_GATE + kk:ROUTE_GATE + kk + 1]
    x2_ref[...] = _layer_norm(alpha * x1_ref[...] + f, g_ref[...], b_ref[...], LN_EPS)


def _combine_call(x1, route, dest, ys, ln_g, ln_b, alpha):
    n = x1.shape[0]
    tm = _pick(n, (256, 128))
    n_tiles = n // tm
    vec = pl.BlockSpec((1, D_MODEL), lambda i: (0, 0))
    return pl.pallas_call(
        functools.partial(_combine_kernel, tm=tm, alpha=alpha, n_tiles=n_tiles), grid=(n_tiles,),
        in_specs=[pl.BlockSpec((tm * TOP_K,), lambda i: (i,), memory_space=pltpu.SMEM),
                  pl.BlockSpec((tm * TOP_K,), lambda i: (jnp.minimum(i + 1, n_tiles - 1),), memory_space=pltpu.SMEM),
                  pl.BlockSpec((tm, LANES), lambda i: (i, 0)),
                  pl.BlockSpec((tm, D_MODEL), lambda i: (i, 0)),
                  pl.BlockSpec(memory_space=pl.ANY), vec, vec],
        out_specs=pl.BlockSpec((tm, D_MODEL), lambda i: (i, 0)),
        out_shape=jax.ShapeDtypeStruct((n, D_MODEL), F32),
        scratch_shapes=[pltpu.VMEM((2, TOP_K, tm * ROW_TILE, LANES), F32), pltpu.SemaphoreType.DMA((2,))],
        compiler_params=_params(("arbitrary",)), name="moe_combine",
    )(dest, dest, route, x1, ys, ln_g.reshape(1, -1), ln_b.reshape(1, -1))


def _moe_ffn(x1, route, counts, w1, b1, w2, b2, e0, ln_g, ln_b, alpha):
    n = x1.shape[0]
    dest, block_e, n_used, zero_rows, n_blocks = _moe_plan(route, counts, n)
    xs = _dispatch_call(x1, dest, zero_rows, n_blocks)
    ys = _expert_call(xs, block_e, n_used, w1, b1, w2, b2, e0)
    return _combine_call(x1, route, dest, ys, ln_g, ln_b, alpha)


def kernel(x_prompt, x_sample, state_gdn_conv, state_gdn_S, state_cc_conv, state_rwkv_shift, state_rwkv_S, ln_in_g, ln_in_b, w_in, sgu_ln_g, sgu_ln_b, sgu_w, sgu_b, gdn_conv_w, gdn_A_log, gdn_dt_bias, gdn_norm_g, cc_dw_w, cc_dw_b, cc_ln_g, cc_ln_b, rw_mu, rw_w0, rw_w2, rw_a0, rw_a2, rw_g2, rw_k_k, rw_k_a, rw_r_k, rw_ln_g, rw_ln_b, w_out, ln_mix_g, ln_mix_b, router_w, router_b, moe_w1, moe_b1, moe_w2, moe_b2, ln_ffn_g, ln_ffn_b):
    bp, tp, _ = x_prompt.shape
    bs, ts, _ = x_sample.shape
    n_p, n_s = bp * tp, bs * ts
    depth = w_in.shape[0]
    alpha = (2 * depth) ** 0.25
    assert tp % SGU_CHUNK == 0 and SGU_CHUNK % ts == 0

    x = jnp.concatenate([x_prompt.reshape(n_p, D_MODEL), x_sample.reshape(n_s, D_MODEL)], axis=0)
    x = _ln_call(x, ln_in_g, ln_in_b)
    zeros = lambda *s: jnp.zeros(s, F32)
    n_exp = moe_w1.shape[1]
    w1_all = moe_w1.reshape(depth * n_exp, D_MODEL, 2 * D_FF)
    w2_all = moe_w2.reshape(depth * n_exp, D_FF, D_MODEL)
    b1_all = _regroup_bias(moe_b1.reshape(depth * n_exp, 2 * D_FF))
    b2_all = moe_b2.reshape(depth * n_exp, 1, D_MODEL)
    outs_p, outs_s = [], []
    for l in range(depth):
        p_d, p_bg, p_b, p_a, p_c = _proj_call(x, _reorder_w_in(w_in[l], gdn_A_log.shape[1]))

        w_eff, b_eff = _sgu_weights(sgu_w[l], sgu_b[l], ts)
        y_a, v = _sgu_call(p_a, w_eff, b_eff, sgu_ln_g[l], sgu_ln_b[l], n_p)
        v_p = v[:n_p].reshape(bp, tp, W_MIX)[:, ((tp - 1) // SGU_CHUNK) * SGU_CHUNK:]
        v_s = v[n_p:].reshape(bs, ts, W_MIX)

        gdn_w = (gdn_conv_w[l], gdn_A_log[l], gdn_dt_bias[l], gdn_norm_g[l])
        yb_p, gbuf_p, gs_p = _gdn_call(p_b, p_bg, zeros(bp, GDN_CONV - 1, GDN_QKV),
                                       zeros(bp, N_HEADS, HEAD_DIM, HEAD_DIM), *gdn_w, 0, tp)
        yb_s, gbuf_s, gs_s = _gdn_call(p_b, p_bg, state_gdn_conv[l], state_gdn_S[l], *gdn_w, n_p, ts)

        cc_w = (cc_dw_w[l], cc_dw_b[l], cc_ln_g[l], cc_ln_b[l])
        yc_p, cbuf_p = _cc_call(p_c, zeros(bp, CC_WIDTH - 1, W_MIX), *cc_w, 0, tp)
        yc_s, cbuf_s = _cc_call(p_c, state_cc_conv[l], *cc_w, n_p, ts)

        rw_w = (rw_mu[l], rw_w0[l], rw_a0[l], _rwkv_lora_weights(rw_w2[l], rw_a2[l], rw_g2[l]),
                rw_k_k[l], rw_k_a[l], rw_r_k[l].reshape(-1), rw_ln_g[l], rw_ln_b[l])
        yd_p, rsh_p, rs_p = _rwkv_call(p_d, zeros(bp, COLS_D), zeros(bp, N_HEADS, HEAD_DIM, HEAD_DIM), *rw_w, 0, tp)
        yd_s, rsh_s, rs_s = _rwkv_call(p_d, state_rwkv_shift[l], state_rwkv_S[l], *rw_w, n_p, ts)

        x1, route, counts = _outproj_router_call((y_a, yb_p, yc_p, yd_p), (y_a, yb_s, yc_s, yd_s), x,
                                                 w_out[l].astype(BF16), ln_mix_g[l], ln_mix_b[l],
                                                 router_w[l], router_b[l], alpha)
        x = _moe_ffn(x1, route, counts, w1_all, b1_all, w2_all, b2_all, l * n_exp,
                     ln_ffn_g[l], ln_ffn_b[l], alpha)
        outs_p.append((v_p, gbuf_p, gs_p, cbuf_p, rsh_p, rs_p))
        outs_s.append((v_s, gbuf_s, gs_s, cbuf_s, rsh_s, rs_s))

    stack = lambda outs, i: jnp.stack([o[i] for o in outs])
    res = [x[:n_p].reshape(bp, tp, D_MODEL), x[n_p:].reshape(bs, ts, D_MODEL)]
    for i in range(6):
        res += [stack(outs_p, i), stack(outs_s, i)]
    return tuple(res)
```

```python
import functools
import math

import jax
import jax.numpy as jnp
from jax import lax
from jax.experimental import pallas as pl
from jax.experimental.pallas import tpu as pltpu

F32 = jnp.float32
BF16 = jnp.bfloat16

D_MODEL = 1024
HEAD_DIM = 64
W_MIX = 256
N_HEADS = W_MIX // HEAD_DIM
SGU_CHUNK = 128
GDN_CONV = 4
GDN_CHUNK = 32
CC_WIDTH = 31
RWKV_CHUNK = 32
SEQ_GROUP = 4
SEQ_TBLK = 256
RWKV_PASSES = 1
GDN_PASSES = 1
LORA_W, LORA_A, LORA_G = 32, 32, 64
COLS_D = 3 * W_MIX + LORA_W + LORA_A + LORA_G
N_EXPERTS = 32
TOP_K = 4
D_FF = D_MODEL
SWIGLU_ALPHA = 1.702
SWIGLU_LIMIT = 7.0
LN_EPS = 1e-5
RMS_EPS = 1e-6
GN_EPS = 64e-5
LANES = 128
SUBLANES = 8
VMEM_LIMIT = 56 * 1024 * 1024
NEG_BIG = -1e30

P_D = COLS_D
P_BG = LANES
P_B = 4 * W_MIX
P_A = 2 * W_MIX
P_C = 2 * W_MIX
P_TOTAL = P_D + P_BG + P_B + P_A + P_C


def _pick(n, cands):
    for c in cands:
        if n % c == 0:
            return c
    raise ValueError(f"no tile in {cands} divides {n}")


def _params(sem):
    return pltpu.CompilerParams(dimension_semantics=sem, vmem_limit_bytes=VMEM_LIMIT)


def _layer_norm(x, g, b, eps):
    xc = x - jnp.mean(x, -1, keepdims=True)
    var = jnp.mean(xc * xc, -1, keepdims=True)
    return xc * lax.rsqrt(var + eps) * g + b


def _sigmoid(x):
    return 1.0 / (1.0 + jnp.exp(-x))


def _silu(x):
    return x * _sigmoid(x)


def _softplus(x):
    return jnp.maximum(x, 0.0) + jnp.log(1.0 + jnp.exp(-jnp.abs(x)))


def _dot(a, b):
    return jnp.dot(a, b, preferred_element_type=F32)


def _head_ones():
    r = lax.broadcasted_iota(jnp.int32, (W_MIX, W_MIX), 0) // HEAD_DIM
    c = lax.broadcasted_iota(jnp.int32, (W_MIX, W_MIX), 1) // HEAD_DIM
    return (r == c).astype(F32)


def _tri(n, strict):
    r = lax.broadcasted_iota(jnp.int32, (n, n), 0)
    c = lax.broadcasted_iota(jnp.int32, (n, n), 1)
    return (r > c) if strict else (r >= c)


_NN =(((1,), (0,)), ((), ()))
_NT = (((1,), (1,)), ((), ()))
_TN = (((0,), (0,)), ((), ()))


def _split2(x):
    hi = x.astype(BF16)
    return hi, (x - hi.astype(F32)).astype(BF16)


def _split3(x):
    hi = x.astype(BF16)
    r = x - hi.astype(F32)
    mid = r.astype(BF16)
    return hi, mid, (r - mid.astype(F32)).astype(BF16)


def _mm(a, b, dn=_NN, passes=1):
    d = lambda x, y: lax.dot_general(x, y, dn, preferred_element_type=F32)
    if passes == 1:
        return d(a.astype(BF16), b.astype(BF16))
    a_hi, a_lo = _split2(a)
    b_hi, b_lo = _split2(b)
    return d(a_hi, b_hi) + (d(a_lo, b_hi) + d(a_hi, b_lo))


def _mm_exact_rhs(a, sel, dn=_NN, pieces=3):
    d = lambda x: lax.dot_general(x, sel, dn, preferred_element_type=F32)
    if pieces == 2:
        hi, lo = _split2(a)
        return d(hi) + d(lo)
    hi, mid, lo = _split3(a)
    return d(hi) + (d(mid) + d(lo))


def _mm_exact_lhs(sel, b, dn=_NN, pieces=3):
    d = lambda x: lax.dot_general(sel, x, dn, preferred_element_type=F32)
    if pieces == 2:
        hi, lo = _split2(b)
        return d(hi) + d(lo)
    hi, mid, lo = _split3(b)
    return d(hi) + (d(mid) + d(lo))


def _block_neumann_inverse(xs, block, passes):
    n = xs[0].shape[0]
    eye = (lax.broadcasted_iota(jnp.int32, (n, n), 0) == lax.broadcasted_iota(jnp.int32, (n, n), 1)).astype(F32)
    accs = [eye + x for x in xs]
    ps = list(xs)
    k = 2
    while k < block:
        ps = [_mm(p, p, passes=passes) for p in ps]
        accs = [acc + _mm(acc, p, passes=passes) for acc, p in zip(accs, ps)]
        k *= 2
    return accs


def _each(fn, *lists):
    return [fn(*args) for args in zip(*lists)]


def _stack_masked(x, c):
    lane_head = lax.broadcasted_iota(jnp.int32, (c, W_MIX), 1) // HEAD_DIM
    return jnp.concatenate([jnp.where(lane_head == h, x, 0.0) for h in range(N_HEADS)], axis=0)


def _stack_heads(x):
    return jnp.concatenate([x[:, h * HEAD_DIM:(h + 1) * HEAD_DIM] for h in range(N_HEADS)], axis=0)


def _unstack_heads(x, c):
    return jnp.concatenate([x[h * c:(h + 1) * c] for h in range(N_HEADS)], axis=1)


def _seq_block(i, tb, *, j, ns, nt, blk0):
    return (blk0 + (i * ns + j) * nt + tb, 0)


def _block_tri(c, strict):
    n = N_HEADS * c
    r = lax.broadcasted_iota(jnp.int32, (n, n), 0)
    q = lax.broadcasted_iota(jnp.int32, (n, n), 1)
    same = (r // c) == (q // c)
    return same & ((r > q) if strict else (r >= q))


def _ln_kernel(x_ref, g_ref, b_ref, o_ref):
    o_ref[...] = _layer_norm(x_ref[...], g_ref[...], b_ref[...], LN_EPS)


def _ln_call(x, g, b):
    n = x.shape[0]
    tm = _pick(n, (1024, 512, 256, 128))
    return pl.pallas_call(
        _ln_kernel, grid=(n // tm,),
        in_specs=[pl.BlockSpec((tm, D_MODEL), lambda i: (i, 0)),
                  pl.BlockSpec((1, D_MODEL), lambda i: (0, 0)),
                  pl.BlockSpec((1, D_MODEL), lambda i: (0, 0))],
        out_specs=pl.BlockSpec((tm, D_MODEL), lambda i: (i, 0)),
        out_shape=jax.ShapeDtypeStruct((n, D_MODEL), F32),
        compiler_params=_params(("parallel",)), name="ln_in",
    )(x, g.reshape(1, -1), b.reshape(1, -1))


def _proj_kernel(x_ref, w_ref, pd_ref, pbg_ref, pb_ref, pa_ref, pc_ref):
    p = _dot(x_ref[...].astype(BF16), w_ref[...])
    o = 0
    for ref, w in ((pd_ref, P_D), (pbg_ref, P_BG), (pb_ref, P_B), (pa_ref, P_A), (pc_ref, P_C)):
        ref[...] = p[:, o:o + w]
        o += w


def _proj_call(x, w_cat):
    n = x.shape[0]
    tm = _pick(n, (512, 256, 128))
    widths = (P_D, P_BG, P_B, P_A, P_C)
    return pl.pallas_call(
        _proj_kernel, grid=(n // tm,),
        in_specs=[pl.BlockSpec((tm, D_MODEL), lambda i: (i, 0)),
                  pl.BlockSpec((D_MODEL, P_TOTAL), lambda i: (0, 0))],
        out_specs=[pl.BlockSpec((tm, w), lambda i: (i, 0)) for w in widths],
        out_shape=[jax.ShapeDtypeStruct((n, w), F32) for w in widths],
        compiler_params=_params(("parallel",)), name="proj_in",
    )(x, w_cat)


def _reorder_w_in(w_in, h_b):
    cols_a = 2 * W_MIX
    cols_b = 3 * W_MIX + 2 * h_b + W_MIX
    o1, o2 = cols_a, cols_a + cols_b
    o3 = o2 + 2 * W_MIX
    wa, wb, wc, wd = w_in[:, :o1], w_in[:, o1:o2], w_in[:, o2:o3], w_in[:, o3:]
    qkv, bg, z = wb[:, :3 * W_MIX], wb[:, 3 * W_MIX:3 * W_MIX + 2 * h_b], wb[:, 3 * W_MIX + 2 * h_b:]
    bg = jnp.pad(bg, ((0, 0), (0, P_BG - 2 * h_b)))
    return jnp.concatenate([wd, bg, qkv, z, wa, wc], axis=1).astype(BF16)


def _sgu_kernel(p_ref, w_ref, b_ref, g_ref, beta_ref, y_ref, v_ref, *, n_chunks):
    w = w_ref[0]
    bias = b_ref[0]
    lane_head = lax.broadcasted_iota(jnp.int32, (SGU_CHUNK, W_MIX), 1) // HEAD_DIM
    for c in range(n_chunks):
        rows = pl.ds(c * SGU_CHUNK, SGU_CHUNK)
        x = p_ref[rows, :]
        h = 0.5 * x * (1.0 + lax.erf(x * (1.0 / math.sqrt(2.0))))
        u = h[:, :W_MIX]
        v = _layer_norm(h[:, W_MIX:], g_ref[...], beta_ref[...], LN_EPS)
        v_ref[rows, :] = v
        vb = jnp.concatenate([jnp.where(lane_head == hh, v, 0.0) for hh in range(N_HEADS)], axis=0)
        s = _dot(w, vb.astype(BF16)) + bias
        y_ref[rows, :] = u * s


def _sgu_weights(sgu_w, sgu_b, t_s):
    causal = jnp.tril(jnp.ones((SGU_CHUNK, SGU_CHUNK), bool))
    wp = jnp.where(causal, sgu_w, 0.0)
    reps = SGU_CHUNK // t_s
    ws = jnp.stack([jnp.kron(jnp.eye(reps, dtype=F32), wp[h, :t_s, :t_s]) for h in range(N_HEADS)])
    cat = lambda w: jnp.concatenate([w[h] for h in range(N_HEADS)], axis=1)
    w_eff = jnp.stack([cat(wp), cat(ws)]).astype(BF16)
    bp = jnp.repeat(sgu_b.T, HEAD_DIM, axis=1)
    bs = jnp.tile(bp[:t_s], (reps, 1))
    return w_eff, jnp.stack([bp, bs])


def _sgu_call(p_a, w_eff, b_eff, ln_g, ln_b, n_prompt_rows):
    n = p_a.shape[0]
    tb = _pick(math.gcd(n_prompt_rows, n - n_prompt_rows), (1024, 512, 256, 128))
    n_prompt_tiles = n_prompt_rows // tb
    grp = lambda i: jnp.minimum(i // n_prompt_tiles, 1)
    return pl.pallas_call(
        functools.partial(_sgu_kernel, n_chunks=tb // SGU_CHUNK), grid=(n // tb,),
        in_specs=[pl.BlockSpec((tb, P_A), lambda i: (i, 0)),
                  pl.BlockSpec((1, SGU_CHUNK, N_HEADS * SGU_CHUNK), lambda i: (grp(i), 0, 0)),
                  pl.BlockSpec((1, SGU_CHUNK, W_MIX), lambda i: (grp(i), 0, 0)),
                  pl.BlockSpec((1, W_MIX), lambda i: (0, 0)),
                  pl.BlockSpec((1, W_MIX), lambda i: (0, 0))],
        out_specs=[pl.BlockSpec((tb, W_MIX), lambda i: (i, 0))] * 2,
        out_shape=[jax.ShapeDtypeStruct((n, W_MIX), F32)] * 2,
        compiler_params=_params(("parallel",)), name="sgu",
    )(p_a, w_eff, b_eff, ln_g.reshape(1, -1), ln_b.reshape(1, -1))


CC_HDR = 32


def _cc_kernel(p_ref, buf_ref, w_ref, wb_ref, g_ref, b_ref, y_ref, nb_ref, xp_ref, *, sb, t, tt):
    hist = CC_WIDTH - 1
    ones = _head_ones().astype(BF16)
    w = w_ref[...]
    for s in range(sb):
        x = p_ref[pl.ds(s * t, t), :]
        xp_ref[pl.ds(CC_HDR - hist, hist), :] = buf_ref[s]
        xp_ref[pl.ds(CC_HDR, t), :] = x[:, :W_MIX] * _sigmoid(x[:, W_MIX:])
        nb_ref[s] = xp_ref[pl.ds(t + CC_HDR - hist, hist), :]

        def tile(i, carry):
            base = pl.multiple_of(i * tt, SUBLANES)
            win = xp_ref[pl.ds(base, tt + CC_HDR), :]
            acc = jnp.zeros((tt, W_MIX), F32)
            shifted = [win[b:] for b in range(SUBLANES)]
            for j in range(CC_WIDTH):
                o = j + CC_HDR - hist
                a8 = (o // SUBLANES) * SUBLANES
                acc = acc + shifted[o % SUBLANES][a8:a8 + tt] * w[j:j + 1]
            hh = acc + wb_ref[...]
            mean = _mm_exact_rhs(hh, ones, pieces=2) * (1.0 / HEAD_DIM)
            xc = hh - mean
            var = _mm_exact_rhs(xc * xc, ones, pieces=2) * (1.0 / HEAD_DIM)
            yy = xc * lax.rsqrt(var + LN_EPS) * g_ref[...] + b_ref[...]
            y_ref[pl.ds(pl.multiple_of(s * t + base, SUBLANES), tt), :] = _silu(yy)
            return carry

        lax.fori_loop(0, t // tt, tile, 0)


def _cc_call(p_c, buf, w, wb, g, b, row0, t):
    nseq = buf.shape[0]
    sb = 1 if t >= 256 else _pick(nseq, (16, 8, 4, 2, 1))
    tt = min(t, 256)
    rows = sb * t
    blk0 = row0 // rows
    assert row0 % rows == 0 and t % tt == 0
    kern = functools.partial(_cc_kernel, sb=sb, t=t, tt=tt)
    in_specs = [pl.BlockSpec((rows, P_C), lambda i: (blk0 + i, 0)),
                pl.BlockSpec((sb, CC_WIDTH - 1, W_MIX), lambda i: (i, 0, 0)),
                pl.BlockSpec((CC_WIDTH, W_MIX), lambda i: (0, 0)),
                pl.BlockSpec((1, W_MIX), lambda i: (0, 0)),
                pl.BlockSpec((1, W_MIX), lambda i: (0, 0)),
                pl.BlockSpec((1, W_MIX), lambda i: (0, 0))]
    args = [p_c, buf, w, wb.reshape(1, -1), g.reshape(1, -1), b.reshape(1, -1)]
    return pl.pallas_call(
        kern, grid=(nseq // sb,), in_specs=in_specs,
        out_specs=[pl.BlockSpec((rows, W_MIX), lambda i: (i, 0)),
                   pl.BlockSpec((sb, CC_WIDTH - 1, W_MIX), lambda i: (i, 0, 0))],
        out_shape=[jax.ShapeDtypeStruct((nseq * t, W_MIX), F32),
                   jax.ShapeDtypeStruct((nseq, CC_WIDTH - 1, W_MIX), F32)],
        scratch_shapes=[pltpu.VMEM((t + CC_HDR, W_MIX), F32)],
        compiler_params=_params(("arbitrary",)), name="cc",
    )(*args)


GDN_HDR = 8
GDN_QKV = 3 * W_MIX


def _lane_expand(src_lane0):
    r = lax.broadcasted_iota(jnp.int32, (LANES, W_MIX), 0)
    c = lax.broadcasted_iota(jnp.int32, (LANES, W_MIX), 1) // HEAD_DIM
    return (r == c + src_lane0).astype(F32)


def _gdn_kernel(*refs, ns, tblk, c):
    p_refs, bg_refs = refs[:ns], refs[ns:2 * ns]
    buf_ref, s0_ref, cw_ref, alog_ref, dt_ref, ng_ref, y_ref, nb_ref, s_ref, hdr_ref = refs[2 * ns:]
    hist = GDN_CONV - 1
    hc = N_HEADS * c

    @pl.when(pl.program_id(1) == 0)
    def _():
        nb_ref[...] = buf_ref[...]
        s_ref[...] = s0_ref[...]
        hdr_ref[...] = jnp.zeros_like(hdr_ref)

    ones = _head_ones().astype(BF16)
    e_beta = _lane_expand(0).astype(BF16)
    e_g = _lane_expand(N_HEADS).astype(BF16)
    tri_ones = _tri(c, False).astype(BF16)
    strict_bd = _block_tri(c, True)
    incl_bd = _block_tri(c, False)
    eye_hc = (lax.broadcasted_iota(jnp.int32, (hc, hc), 0) == lax.broadcasted_iota(jnp.int32, (hc, hc), 1))
    eye_w = (lax.broadcasted_iota(jnp.int32, (W_MIX, W_MIX), 0) == lax.broadcasted_iota(jnp.int32, (W_MIX, W_MIX), 1))
    first_lane = (lax.broadcasted_iota(jnp.int32, (W_MIX, hc), 0) % HEAD_DIM == 0).astype(BF16)
    ones_hc = jnp.ones((hc, hc), BF16)
    ones_wv = jnp.ones((W_MIX, HEAD_DIM), BF16)
    cw = cw_ref[...]
    neg_a = -jnp.exp(alog_ref[...])
    mm = functools.partial(_mm, passes=GDN_PASSES)

    def chunk(n, carries):
        rows = pl.ds(pl.multiple_of(n * c, SUBLANES), c)
        sm = functools.partial(_stack_masked, c=c)
        cat = jnp.concatenate
        tails = [cr[0] for cr in carries]
        sts = [cr[1] for cr in carries]
        wins = [cat([tails[j], p_refs[j][rows, :GDN_QKV]], axis=0) for j in range(ns)]

        def conv_act(win):
            conv = jnp.zeros((c, GDN_QKV), F32)
            for tap in range(GDN_CONV):
                o = tap + GDN_HDR - hist
                conv = conv + win[o:o + c] * cw[tap:tap + 1]
            return _silu(conv)

        acts = _each(conv_act, wins)
        qs = [a[:, :W_MIX] for a in acts]
        ks = [a[:, W_MIX:2 * W_MIX] for a in acts]
        vs = [a[:, 2 * W_MIX:] for a in acts]
        sqs = _each(lambda q, k: _mm_exact_rhs(cat([q * q, k * k], axis=0), ones, pieces=2), qs, ks)
        qs = _each(lambda q, sq: q * lax.rsqrt(sq[:c] + 1e-6) * (HEAD_DIM ** -0.5), qs, sqs)
        ks = _each(lambda k, sq: k * lax.rsqrt(sq[c:] + 1e-6), ks, sqs)
        bgs = [bg_refs[j][rows, :] for j in range(ns)]
        betas = _each(lambda bg: _mm_exact_rhs(_sigmoid(bg), e_beta, pieces=2), bgs)
        gsums = _each(lambda bg: _mm_exact_lhs(tri_ones, neg_a * _softplus(bg + dt_ref[...])), bgs)
        gcums = _each(lambda gs: _mm_exact_rhs(gs, e_g), gsums)
        kbs = _each(lambda k, b: k * b, ks, betas)
        vbs = _each(lambda v, b: v * b, vs, betas)
        egcs = _each(jnp.exp, gcums)
        glasts = [gc[c - 1:c, :] for gc in gcums]
        kdecs = _each(lambda k, gl, gc: k * jnp.exp(gl - gc), ks, glasts, gcums)

        gcols = _each(lambda gc: _mm_exact_rhs(sm(gc), first_lane), gcums)
        grows = _each(lambda gcol: _mm_exact_lhs(ones_hc, jnp.where(eye_hc, gcol, 0.0)), gcols)
        decays = _each(lambda gcol, grow: jnp.where(incl_bd, jnp.exp(gcol - grow), 0.0), gcols, grows)
        prods = _each(lambda kb, q, k: mm(cat([sm(kb), sm(q)], axis=0), sm(k), _NT), kbs, qs, ks)
        lms = _each(lambda pr, dec: jnp.where(strict_bd, pr[:hc] * dec, 0.0), prods, decays)
        aqks = _each(lambda pr, dec: pr[hc:] * dec, prods, decays)
        tinvs = _block_neumann_inverse([-lm for lm in lms], c, GDN_PASSES)
        uws = _each(lambda ti, vb, kb, egc: mm(ti, cat([_stack_heads(vb), sm(kb * egc)], axis=1)),
                    tinvs, vbs, kbs, egcs)
        wss = _each(lambda uw, q, egc, st: mm(cat([uw[:, HEAD_DIM:], sm(q * egc)], axis=0), st),
                    uws, qs, egcs, sts)
        vnews = _each(lambda uw, ws: uw[:, :HEAD_DIM] - ws[:hc], uws, wss)
        outs = _each(lambda ws, aqk, vn: _unstack_heads(ws[hc:] + mm(aqk, vn), c), wss, aqks, vnews)
        grs = _each(lambda gl: _mm_exact_rhs(jnp.where(eye_w, jnp.exp(gl), 0.0), ones_wv), glasts)
        sts = _each(lambda st, gr, kd, vn: st * gr + mm(sm(kd), vn, _TN), sts, grs, kdecs, vnews)
        outs = _each(lambda o: o * lax.rsqrt(_mm_exact_rhs(o * o, ones, pieces=2) * (1.0 / HEAD_DIM) + RMS_EPS)
                     * ng_ref[...], outs)
        for j in range(ns):
            y_ref[j, rows, :] = outs[j] * _silu(p_refs[j][rows, GDN_QKV:])
        return tuple((wins[j][c:c + GDN_HDR], sts[j]) for j in range(ns))

    init = []
    for j in range(ns):
        hdr_ref[j, pl.ds(GDN_HDR - hist, hist), :] = nb_ref[j]
        init.append((hdr_ref[j], jnp.concatenate([s_ref[j, h] for h in range(N_HEADS)], axis=0)))
    fin = lax.fori_loop(0, tblk // c, chunk, tuple(init))
    for j in range(ns):
        tail, st = fin[j]
        nb_ref[j] = tail[GDN_HDR - hist:]
        for h in range(N_HEADS):
            s_ref[j, h] = st[h * HEAD_DIM:(h + 1) * HEAD_DIM]


def _gdn_call(p_b, p_bg, buf, s0, conv_w, a_log, dt_bias, norm_g, row0, t):
    nseq = buf.shape[0]
    h_b = a_log.shape[0]
    assert h_b == N_HEADS
    c = math.gcd(t, GDN_CHUNK)
    ns = _pick(nseq, (SEQ_GROUP, 2, 1))
    tblk = min(t, SEQ_TBLK)
    nt = t // tblk
    blk0 = row0 // tblk
    assert row0 % tblk == 0 and t % tblk == 0 and tblk % c == 0 and c % SUBLANES == 0
    lane_pad = lambda x: jnp.pad(x.reshape(1, -1), ((0, 0), (h_b, LANES - 2 * h_b)))
    seq_rows = lambda w: [pl.BlockSpec((tblk, w), functools.partial(_seq_block, j=j, ns=ns, nt=nt, blk0=blk0))
                          for j in range(ns)]
    kern = functools.partial(_gdn_kernel, ns=ns, tblk=tblk, c=c)
    in_specs = seq_rows(P_B) + seq_rows(P_BG) + [
        pl.BlockSpec((ns, GDN_CONV - 1, GDN_QKV), lambda i, tb: (i, 0, 0)),
        pl.BlockSpec((ns, N_HEADS, HEAD_DIM, HEAD_DIM), lambda i, tb: (i, 0, 0, 0)),
        pl.BlockSpec((GDN_CONV, GDN_QKV), lambda i, tb: (0, 0)),
        pl.BlockSpec((1, LANES), lambda i, tb: (0, 0)),
        pl.BlockSpec((1, LANES), lambda i, tb: (0, 0)),
        pl.BlockSpec((1, W_MIX), lambda i, tb: (0, 0))]
    args = [p_b] * ns + [p_bg] * ns + [buf, s0, conv_w, lane_pad(a_log), lane_pad(dt_bias),
                                       jnp.tile(norm_g, N_HEADS).reshape(1, -1)]
    y, nb, st = pl.pallas_call(
        kern, grid=(nseq // ns, nt), in_specs=in_specs,
        out_specs=[pl.BlockSpec((ns, tblk, W_MIX), lambda i, tb: (i, tb, 0)),
                   pl.BlockSpec((ns, GDN_CONV - 1, GDN_QKV), lambda i, tb: (i, 0, 0)),
                   pl.BlockSpec((ns, N_HEADS, HEAD_DIM, HEAD_DIM), lambda i, tb: (i, 0, 0, 0))],
        out_shape=[jax.ShapeDtypeStruct((nseq, t, W_MIX), F32),
                   jax.ShapeDtypeStruct((nseq, GDN_CONV - 1, GDN_QKV), F32),
                   jax.ShapeDtypeStruct((nseq, N_HEADS, HEAD_DIM, HEAD_DIM), F32)],
        scratch_shapes=[pltpu.VMEM((ns, GDN_HDR, GDN_QKV), F32)],
        compiler_params=_params(("arbitrary", "arbitrary")), name="gdn",
    )(*args)
    return y.reshape(nseq * t, W_MIX), nb, st


def _rwkv_kernel(*refs, ns, tblk, c):
    p_refs = refs[:ns]
    (sh_ref, s0_ref, mu_ref, w0_ref, a0_ref, lora_ref, kk_ref, ka_ref, rk_ref, g_ref, b_ref,
     y_ref, sho_ref, s_ref) = refs[ns:]
    hc = N_HEADS * c
    ones = _head_ones().astype(BF16)
    tri_ones = _tri(c, False).astype(BF16)
    strict_bd = _block_tri(c, True)
    incl_bd = _block_tri(c, False)
    lane = lax.broadcasted_iota(jnp.int32, (c, LANES), 1)
    row_id = lax.broadcasted_iota(jnp.int32, (c, COLS_D), 0)
    mm = functools.partial(_mm, passes=RWKV_PASSES)

    @pl.when(pl.program_id(1) == 0)
    def _():
        sho_ref[...] = sh_ref[...]
        s_ref[...] = s0_ref[...]

    def chunk(n, carries):
        rows = pl.ds(pl.multiple_of(n * c, SUBLANES), c)
        sm = functools.partial(_stack_masked, c=c)
        cat = jnp.concatenate
        prev_rows = [cr[0] for cr in carries]
        sts = [cr[1] for cr in carries]
        xs = [p_refs[j][rows, :] for j in range(ns)]
        xls = _each(lambda x, pr: x + (jnp.where(row_id == 0, pr, pltpu.roll(x, 1, 0)) - x) * mu_ref[...],
                    xs, prev_rows)
        rs = [xl[:, :W_MIX] for xl in xls]
        ks = [xl[:, W_MIX:2 * W_MIX] for xl in xls]
        vs = [xl[:, 2 * W_MIX:3 * W_MIX] for xl in xls]

        def lora_act(xl):
            lo = xl[:, 3 * W_MIX:]
            return jnp.where(lane < LORA_W, jnp.tanh(lo), jnp.where(lane < LORA_W + LORA_A, lo, _sigmoid(lo)))

        loras = _each(lambda xl: _mm(lora_act(xl), lora_ref[...], passes=3), xls)
        lws = _each(lambda lr: -jnp.exp(-_softplus(-(w0_ref[...] + lr[:, :W_MIX])) - 0.5), loras)
        a_s = _each(lambda lr: _sigmoid(a0_ref[...] + lr[:, W_MIX:2 * W_MIX]), loras)
        gs = [lr[:, 2 * W_MIX:] for lr in loras]
        kkps = _each(lambda k: k * kk_ref[...], ks)
        k2s = _each(lambda k, a: k * (1.0 + (a - 1.0) * ka_ref[...]), ks, a_s)
        sums = _each(lambda kkp, r, k2: _mm_exact_rhs(cat([kkp * kkp, r * k2 * rk_ref[...]], axis=0), ones, pieces=2),
                     kkps, rs, k2s)
        kks = _each(lambda kkp, sm_: kkp * lax.rsqrt(sm_[:c] + 1e-6), kkps, sums)
        bonuses = [sm_[c:] for sm_ in sums]
        cums = _each(lambda lw: _mm_exact_lhs(tri_ones, lw, pieces=2), lws)
        invs = _each(lambda cum: jnp.exp(-cum), cums)
        a_hats = _each(lambda kk, cum, lw: -kk * jnp.exp(cum - lw), kks, cums, lws)
        b_hats = _each(lambda kk, a, inv: kk * a * inv, kks, a_s, invs)
        c_hats = _each(lambda k2, inv: k2 * inv, k2s, invs)
        q_hats = _each(lambda r, cum: r * jnp.exp(cum), rs, cums)
        gam_cs = [jnp.exp(cum[c - 1:c, :]) for cum in cums]

        xaqs = _each(lambda ah, qh: cat([sm(ah), sm(qh)], axis=0), a_hats, q_hats)
        bcss = _each(lambda bh, ch: cat([sm(bh), sm(ch)], axis=0), b_hats, c_hats)
        prods = _each(lambda xaq, bcs: mm(xaq, bcs, _NT), xaqs, bcss)
        a_ms = [jnp.where(strict_bd, pr[:hc, :hc], 0.0) for pr in prods]
        b_ms = [jnp.where(strict_bd, pr[:hc, hc:], 0.0) for pr in prods]
        p_qs = [cat([jnp.where(incl_bd, pr[hc:, :hc], 0.0), jnp.where(incl_bd, pr[hc:, hc:], 0.0)], axis=1)
                for pr in prods]
        tinvs = _block_neumann_inverse(a_ms, c, RWKV_PASSES)
        vss = _each(_stack_heads, vs)
        zos = _each(lambda xaq, st: mm(xaq, st, _NT), xaqs, sts)
        bvs = _each(lambda bm, v_: mm(bm, v_), b_ms, vss)
        zs = _each(lambda ti, zo, bv: mm(ti, zo[:hc] + bv), tinvs, zos, bvs)
        zvs = _each(lambda z, v_: cat([z, v_], axis=0), zs, vss)
        outs = _each(lambda zo, pq, zv: zo[hc:] + mm(pq, zv), zos, p_qs, zvs)
        sts = _each(lambda st, zv, bcs, gc: (st + mm(zv, bcs, _TN)) * gc, sts, zvs, bcss, gam_cs)

        ys = _each(lambda o: _unstack_heads(o, c), outs)
        ycs = _each(lambda y: y - _mm_exact_rhs(y, ones, pieces=2) * (1.0 / HEAD_DIM), ys)
        yns = _each(lambda yc: yc * lax.rsqrt(_mm_exact_rhs(yc * yc, ones, pieces=2) * (1.0 / HEAD_DIM) + GN_EPS)
                    * g_ref[...] + b_ref[...], ycs)
        for j in range(ns):
            y_ref[j, rows, :] = (yns[j] + bonuses[j] * vs[j]) * gs[j]
        return tuple((xs[j][c - 1:c, :], sts[j]) for j in range(ns))

    init = tuple((sho_ref[j], jnp.concatenate([s_ref[j, h] for h in range(N_HEADS)], axis=1)) for j in range(ns))
    fin = lax.fori_loop(0, tblk // c, chunk, init)
    for j in range(ns):
        last_row, st = fin[j]
        sho_ref[j] = last_row
        for h in range(N_HEADS):
            s_ref[j, h] = st[:, h * HEAD_DIM:(h + 1) * HEAD_DIM]


def _rwkv_lora_weights(w2, a2, g2):
    m = jnp.zeros((LANES, 3 * W_MIX), F32)
    m = m.at[:LORA_W, :W_MIX].set(w2)
    m = m.at[LORA_W:LORA_W + LORA_A, W_MIX:2 * W_MIX].set(a2)
    return m.at[LORA_W + LORA_A:, 2 * W_MIX:].set(g2)


def _rwkv_call(p_d, shift, s0, mu, w0, a0, lora_w, k_k, k_a, r_k, ln_g, ln_b, row0, t):
    nseq = shift.shape[0]
    c = math.gcd(t, RWKV_CHUNK)
    ns = _pick(nseq, (SEQ_GROUP, 2, 1))
    tblk = min(t, SEQ_TBLK)
    nt = t // tblk
    blk0 = row0 // tblk
    assert row0 % tblk == 0 and t % tblk == 0 and tblk % c == 0 and c % SUBLANES == 0
    row = lambda x: x.reshape(1, -1)
    vec = lambda w: pl.BlockSpec((1, w), lambda i, tb: (0, 0))
    kern = functools.partial(_rwkv_kernel, ns=ns, tblk=tblk, c=c)
    in_specs = [pl.BlockSpec((tblk, P_D), functools.partial(_seq_block, j=j, ns=ns, nt=nt, blk0=blk0))
                for j in range(ns)] + [
        pl.BlockSpec((ns, 1, COLS_D), lambda i, tb: (i, 0, 0)),
        pl.BlockSpec((ns, N_HEADS, HEAD_DIM, HEAD_DIM), lambda i, tb: (i, 0, 0, 0)),
        vec(COLS_D), vec(W_MIX), vec(W_MIX),
        pl.BlockSpec((LANES, 3 * W_MIX), lambda i, tb: (0, 0)),
        vec(W_MIX), vec(W_MIX), vec(W_MIX), vec(W_MIX), vec(W_MIX)]
    args = [p_d] * ns + [shift.reshape(nseq, 1, COLS_D), s0, row(mu), row(w0), row(a0), lora_w,
                         row(k_k), row(k_a), row(r_k), row(ln_g), row(ln_b)]
    y, sh, st = pl.pallas_call(
        kern, grid=(nseq // ns, nt), in_specs=in_specs,
        out_specs=[pl.BlockSpec((ns, tblk, W_MIX), lambda i, tb: (i, tb, 0)),
                   pl.BlockSpec((ns, 1, COLS_D), lambda i, tb: (i, 0, 0)),
                   pl.BlockSpec((ns, N_HEADS, HEAD_DIM, HEAD_DIM), lambda i, tb: (i, 0, 0, 0))],
        out_shape=[jax.ShapeDtypeStruct((nseq, t, W_MIX), F32),
                   jax.ShapeDtypeStruct((nseq, 1, COLS_D), F32),
                   jax.ShapeDtypeStruct((nseq, N_HEADS, HEAD_DIM, HEAD_DIM), F32)],
        compiler_params=_params(("arbitrary", "arbitrary")), name="rwkv",
    )(*args)
    return y.reshape(nseq * t, W_MIX), sh.reshape(nseq, COLS_D), st


ROUTE_IDX, ROUTE_GATE, ROUTE_RANK = 0, TOP_K, 2 * TOP_K


def _outproj_router_kernel(*refs, tm, alpha, n_prompt_tiles):
    yp_refs, ys_refs = refs[0:4], refs[4:8]
    x_ref, wo_ref, g_ref, b_ref, rw_ref, rb_ref, x1_ref, route_ref, cnt_ref, tri_ref = refs[8:]

    @pl.when(pl.program_id(0) == 0)
    def _():
        cnt_ref[...] = jnp.zeros_like(cnt_ref)
        tri_ref[...] = _tri(tm, True).astype(BF16)

    is_prompt = pl.program_id(0) < n_prompt_tiles
    mix = jnp.zeros((tm, D_MODEL), F32)
    for i in range(4):
        y = jnp.where(is_prompt, yp_refs[i][...], ys_refs[i][...])
        mix = mix + _dot(y.astype(BF16), wo_ref[pl.ds(i * W_MIX, W_MIX), :])
    x1 = _layer_norm(alpha * x_ref[...] + mix, g_ref[...], b_ref[...], LN_EPS)
    x1_ref[...] = x1

    logits = _mm(x1, rw_ref[...], passes=3) + rb_ref[...]
    lane = lax.broadcasted_iota(jnp.int32, (tm, LANES), 1)
    lane_f = lane.astype(F32)
    work = logits
    vals, hots, ids = [], [], []
    for _ in range(TOP_K):
        m = jnp.max(work, axis=-1, keepdims=True)
        idx = jnp.min(jnp.where(work == m, lane_f, float(LANES)), axis=-1, keepdims=True)
        hot = lane_f == idx
        vals.append(m)
        hots.append(hot)
        ids.append(idx)
        work = jnp.where(hot, -jnp.inf, work)
    exps = [jnp.exp(v - vals[0]) for v in vals]
    denom = exps[0] + exps[1] + exps[2] + exps[3]

    any_hot = jnp.zeros((tm, LANES), F32)
    for hot in hots:
        any_hot = any_hot + hot.astype(F32)
    before = _dot(tri_ref[...], any_hot.astype(BF16)) + cnt_ref[...]
    cnt_ref[...] = cnt_ref[...] + jnp.sum(any_hot, axis=0, keepdims=True)

    route = jnp.zeros((tm, LANES), F32)
    for kk in range(TOP_K):
        rank = jnp.sum(jnp.where(hots[kk], before, 0.0), axis=-1, keepdims=True)
        route = jnp.where(lane == ROUTE_IDX + kk, ids[kk], route)
        route = jnp.where(lane == ROUTE_GATE + kk, exps[kk] / denom, route)
        route = jnp.where(lane == ROUTE_RANK + kk, rank, route)
    route_ref[...] = route


def _outproj_router_call(ys_prompt, ys_sample, x, w_out, ln_g, ln_b, router_w, router_b, alpha):
    n = x.shape[0]
    n_p, n_s = ys_prompt[-1].shape[0], ys_sample[-1].shape[0]
    tm = _pick(math.gcd(n_p, n_s), (256, 128))
    npt = n_p // tm
    pmap = lambda y: (lambda i: (i, 0)) if y.shape[0] == n else (lambda i: (jnp.minimum(i, npt - 1), 0))
    smap = lambda y: (lambda i: (i, 0)) if y.shape[0] == n else (lambda i: (jnp.maximum(i - npt, 0), 0))
    row = lambda v: v.reshape(1, -1)
    vec = lambda w: pl.BlockSpec((1, w), lambda i: (0, 0))
    rw = jnp.pad(router_w, ((0, 0), (0, LANES - N_EXPERTS)))
    rb = jnp.pad(router_b, (0, LANES - N_EXPERTS), constant_values=NEG_BIG)
    return pl.pallas_call(
        functools.partial(_outproj_router_kernel, tm=tm, alpha=alpha, n_prompt_tiles=npt), grid=(n // tm,),
        in_specs=[pl.BlockSpec((tm, W_MIX), pmap(y)) for y in ys_prompt] + [
            pl.BlockSpec((tm, W_MIX), smap(y)) for y in ys_sample] + [
            pl.BlockSpec((tm, D_MODEL), lambda i: (i, 0)),
            pl.BlockSpec((D_MODEL, D_MODEL), lambda i: (0, 0)),
            vec(D_MODEL), vec(D_MODEL),
            pl.BlockSpec((D_MODEL, LANES), lambda i: (0, 0)), vec(LANES)],
        out_specs=[pl.BlockSpec((tm, D_MODEL), lambda i: (i, 0)),
                   pl.BlockSpec((tm, LANES), lambda i: (i, 0)),
                   pl.BlockSpec((1, LANES), lambda i: (0, 0))],
        out_shape=[jax.ShapeDtypeStruct((n, D_MODEL), F32),
                   jax.ShapeDtypeStruct((n, LANES), F32),
                   jax.ShapeDtypeStruct((1, LANES), F32)],
        scratch_shapes=[pltpu.VMEM((tm, tm), BF16)],
        compiler_params=_params(("arbitrary",)), name="outproj_router",
    )(*ys_prompt, *ys_sample, x, w_out, row(ln_g), row(ln_b), rw, row(rb))


MOE_TB = 512


def _moe_plan(route, counts, n):
    e_idx = route[:, ROUTE_IDX:ROUTE_IDX + TOP_K].astype(jnp.int32)
    rank = route[:, ROUTE_RANK:ROUTE_RANK + TOP_K].astype(jnp.int32)
    cnt = counts[0, :N_EXPERTS].astype(jnp.int32)
    padded = (cnt + MOE_TB - 1) // MOE_TB * MOE_TB
    pad_end = jnp.cumsum(padded)
    pad_start = pad_end - padded
    dest = (pad_start[e_idx] + rank).reshape(n * TOP_K)
    n_blocks = -(-n * TOP_K // MOE_TB) + N_EXPERTS
    n_used = pad_end[-1] // MOE_TB
    blk = jnp.minimum(jnp.arange(n_blocks), n_used - 1) * MOE_TB
    block_e = jnp.minimum(jnp.sum(pad_end[None, :] <= blk[:, None], axis=1), N_EXPERTS - 1).astype(jnp.int32)
    last_block_row = jnp.where(padded > 0, pad_end - MOE_TB, -1)
    tail = n_used + jnp.arange(N_EXPERTS)
    tail_row = jnp.where(tail < n_blocks, tail * MOE_TB, -1)
    zero_rows = jnp.concatenate([last_block_row, tail_row]).astype(jnp.int32)
    return dest, block_e, n_used.reshape(1).astype(jnp.int32), zero_rows, n_blocks


ROW_TILE = D_MODEL // LANES


def _store_row_tiles(ref, x):
    for s in range(ROW_TILE):
        ref[pl.ds(s, x.shape[0], stride=ROW_TILE), :] = x[:, s * LANES:(s + 1) * LANES]


def _load_row_tiles(ref, rows):
    return jnp.concatenate([ref[pl.ds(s, rows, stride=ROW_TILE), :] for s in range(ROW_TILE)], axis=1)


def _tile_rows(row):
    return pl.ds(pl.multiple_of(row * ROW_TILE, ROW_TILE), ROW_TILE)


def _dispatch_kernel(zrow_ref, dest_ref, x_ref, xs_ref, zbuf_ref, xbuf_ref, zsem, sems, *, tm, n_tiles):
    step = pl.program_id(0)
    slot = step % 2

    def zero_copy(e):
        row = pl.multiple_of(jnp.maximum(zrow_ref[e], 0) * ROW_TILE, MOE_TB * ROW_TILE)
        return pltpu.make_async_copy(zbuf_ref, xs_ref.at[pl.ds(row, MOE_TB * ROW_TILE)], zsem)

    @pl.when(pl.program_id(0) == 0)
    def _():
        zbuf_ref[...] = jnp.zeros_like(zbuf_ref)
        for e in range(2 * N_EXPERTS):
            @pl.when(zrow_ref[e] >= 0)
            def _():
                zero_copy(e).start()
        for e in range(2 * N_EXPERTS):
            @pl.when(zrow_ref[e] >= 0)
            def _():
                zero_copy(e).wait()

    def row_copy(s, t, dst_row):
        return pltpu.make_async_copy(xbuf_ref.at[s, _tile_rows(t)], xs_ref.at[_tile_rows(dst_row)], sems.at[s])

    def issue(t, c):
        for kk in range(TOP_K):
            row_copy(slot, t, dest_ref[t * TOP_K + kk]).start(priority=kk % 2)
        return c

    def drain(s):
        def body(t, c):
            for kk in range(TOP_K):
                row_copy(s, 0, 0).wait()
            return c
        lax.fori_loop(0, tm, body, 0)

    _store_row_tiles(xbuf_ref.at[slot], x_ref[...])
    lax.fori_loop(0, tm, issue, 0)

    @pl.when(step > 0)
    def _():
        drain(1 - slot)

    @pl.when(step == n_tiles - 1)
    def _():
        drain(slot)


def _dispatch_call(x1, dest, last_block_row, n_blocks):
    n = x1.shape[0]
    tm = _pick(n, (256, 128))
    return pl.pallas_call(
        functools.partial(_dispatch_kernel, tm=tm, n_tiles=n // tm),
        grid_spec=pltpu.PrefetchScalarGridSpec(
            num_scalar_prefetch=1, grid=(n // tm,),
            in_specs=[pl.BlockSpec((tm * TOP_K,), lambda i, z: (i,), memory_space=pltpu.SMEM),
                      pl.BlockSpec((tm, D_MODEL), lambda i, z: (i, 0))],
            out_specs=pl.BlockSpec(memory_space=pl.ANY),
            scratch_shapes=[pltpu.VMEM((MOE_TB * ROW_TILE, LANES), F32), pltpu.VMEM((2, tm * ROW_TILE, LANES), F32),
                            pltpu.SemaphoreType.DMA(()), pltpu.SemaphoreType.DMA((2,))]),
        out_shape=jax.ShapeDtypeStruct((n_blocks * MOE_TB * ROW_TILE, LANES), F32),
        compiler_params=_params(("arbitrary",)), name="moe_dispatch",
    )(last_block_row, dest, x1)


PAIR_GROUP = 2 * LANES


def _regroup_bias(b1):
    e, f2 = b1.shape
    return b1.reshape(e, f2 // PAIR_GROUP, LANES, 2).swapaxes(2, 3).reshape(e, 1, f2)


def _expert_kernel(be_ref, nu_ref, x_ref, w1_ref, b1_ref, w2_ref, b2_ref, y_ref, w1s_ref, w2s_ref):
    step = pl.program_id(0)
    live = step < nu_ref[0]
    new_expert = (step == 0) | (be_ref[step] != be_ref[jnp.maximum(step - 1, 0)])

    @pl.when(live & new_expert)
    def _():
        r = lax.broadcasted_iota(jnp.int32, (PAIR_GROUP, PAIR_GROUP), 0)
        c = lax.broadcasted_iota(jnp.int32, (PAIR_GROUP, PAIR_GROUP), 1)
        perm = (r == jnp.where(c < LANES, 2 * c, 2 * (c - LANES) + 1)).astype(BF16)
        for g in range(2 * D_FF // PAIR_GROUP):
            cols = pl.ds(g * PAIR_GROUP, PAIR_GROUP)
            w1s_ref[:, cols] = _dot(w1_ref[0, :, cols].astype(BF16), perm).astype(BF16)
        w2s_ref[...] = w2_ref[0].astype(BF16)

    @pl.when(live)
    def _():
        x = _load_row_tiles(x_ref, MOE_TB)
        h = _dot(x.astype(BF16), w1s_ref[...]) + b1_ref[0]
        acts = []
        for g in range(2 * D_FF // PAIR_GROUP):
            hg = jnp.minimum(h[:, g * PAIR_GROUP:g * PAIR_GROUP + LANES], SWIGLU_LIMIT)
            hl = jnp.clip(h[:, g * PAIR_GROUP + LANES:(g + 1) * PAIR_GROUP], -SWIGLU_LIMIT, SWIGLU_LIMIT)
            acts.append((hg * _sigmoid(SWIGLU_ALPHA * hg) * (hl + 1.0)).astype(BF16))
        _store_row_tiles(y_ref, _dot(jnp.concatenate(acts, axis=1), w2s_ref[...]) + b2_ref[0])

    @pl.when(jnp.logical_not(live))
    def _():
        y_ref[...] = jnp.zeros_like(y_ref)


def _expert_call(xs, block_e, n_used, w1, b1, w2, b2, e0):
    n_blocks = xs.shape[0] // (MOE_TB * ROW_TILE)
    xmap = lambda i, be, nu: (jnp.minimum(i, nu[0] - 1), 0)
    emap3 = lambda i, be, nu: (e0 + be[i], 0, 0)
    return pl.pallas_call(
        _expert_kernel,
        grid_spec=pltpu.PrefetchScalarGridSpec(
            num_scalar_prefetch=2, grid=(n_blocks,),
            in_specs=[pl.BlockSpec((MOE_TB * ROW_TILE, LANES), xmap),
                      pl.BlockSpec((1, D_MODEL, 2 * D_FF), emap3), pl.BlockSpec((1, 1, 2 * D_FF), emap3),
                      pl.BlockSpec((1, D_FF, D_MODEL), emap3), pl.BlockSpec((1, 1, D_MODEL), emap3)],
            out_specs=pl.BlockSpec((MOE_TB * ROW_TILE, LANES), lambda i, be, nu: (i, 0)),
            scratch_shapes=[pltpu.VMEM((D_MODEL, 2 * D_FF), BF16), pltpu.VMEM((D_FF, D_MODEL), BF16)]),
        out_shape=jax.ShapeDtypeStruct(xs.shape, F32),
        compiler_params=_params(("arbitrary",)), name="moe_experts",
    )(block_e, n_used, xs, w1, b1, w2, b2)


def _combine_kernel(dest_ref, dnext_ref, route_ref, x1_ref, ys_ref, g_ref, b_ref, x2_ref, buf_ref, sems,
                    *, tm, alpha, n_tiles):
    step = pl.program_id(0)
    slot = step % 2

    def row_copy(s, t, kk, src_row):
        return pltpu.make_async_copy(ys_ref.at[_tile_rows(src_row)], buf_ref.at[s, kk, _tile_rows(t)], sems.at[s])

    def issue(dref, s):
        def body(t, c):
            for kk in range(TOP_K):
                row_copy(s, t, kk, dref[t * TOP_K + kk]).start(priority=kk % 2)
            return c
        lax.fori_loop(0, tm, body, 0)

    @pl.when(step == 0)
    def _():
        issue(dest_ref, 0)

    @pl.when(step + 1 < n_tiles)
    def _():
        issue(dnext_ref, 1 - slot)

    def drain(t, c):
        for kk in range(TOP_K):
            row_copy(slot, 0, kk, 0).wait()
        return c

    lax.fori_loop(0, tm, drain, 0)
    route = route_ref[...]
    f = jnp.zeros((tm, D_MODEL), F32)
    for kk in range(TOP_K):
        f = f + _load_row_tiles(buf_ref.at[slot, kk], tm) * route[:, ROUTE_GATE + kk:ROUTE_GATE + kk + 1]
    x2_ref[...] = _layer_norm(alpha * x1_ref[...] + f, g_ref[...], b_ref[...], LN_EPS)


def _combine_call(x1, route, dest, ys, ln_g, ln_b, alpha):
    n = x1.shape[0]
    tm = _pick(n, (256, 128))
    n_tiles = n // tm
    vec = pl.BlockSpec((1, D_MODEL), lambda i: (0, 0))
    return pl.pallas_call(
        functools.partial(_combine_kernel, tm=tm, alpha=alpha, n_tiles=n_tiles), grid=(n_tiles,),
        in_specs=[pl.BlockSpec((tm * TOP_K,), lambda i: (i,), memory_space=pltpu.SMEM),
                  pl.BlockSpec((tm * TOP_K,), lambda i: (jnp.minimum(i + 1, n_tiles - 1),), memory_space=pltpu.SMEM),
                  pl.BlockSpec((tm, LANES), lambda i: (i, 0)),
                  pl.BlockSpec((tm, D_MODEL), lambda i: (i, 0)),
                  pl.BlockSpec(memory_space=pl.ANY), vec, vec],
        out_specs=pl.BlockSpec((tm, D_MODEL), lambda i: (i, 0)),
        out_shape=jax.ShapeDtypeStruct((n, D_MODEL), F32),
        scratch_shapes=[pltpu.VMEM((2, TOP_K, tm * ROW_TILE, LANES), F32), pltpu.SemaphoreType.DMA((2,))],
        compiler_params=_params(("arbitrary",)), name="moe_combine",
    )(dest, dest, route, x1, ys, ln_g.reshape(1, -1), ln_b.reshape(1, -1))


def _moe_ffn(x1, route, counts, w1, b1, w2, b2, e0, ln_g, ln_b, alpha):
    n = x1.shape[0]
    dest, block_e, n_used, zero_rows, n_blocks = _moe_plan(route, counts, n)
    xs = _dispatch_call(x1, dest, zero_rows, n_blocks)
    ys = _expert_call(xs, block_e, n_used, w1, b1, w2, b2, e0)
    return _combine_call(x1, route, dest, ys, ln_g, ln_b, alpha)


def kernel(x_prompt, x_sample, state_gdn_conv, state_gdn_S, state_cc_conv, state_rwkv_shift, state_rwkv_S, ln_in_g, ln_in_b, w_in, sgu_ln_g, sgu_ln_b, sgu_w, sgu_b, gdn_conv_w, gdn_A_log, gdn_dt_bias, gdn_norm_g, cc_dw_w, cc_dw_b, cc_ln_g, cc_ln_b, rw_mu, rw_w0, rw_w2, rw_a0, rw_a2, rw_g2, rw_k_k, rw_k_a, rw_r_k, rw_ln_g, rw_ln_b, w_out, ln_mix_g, ln_mix_b, router_w, router_b, moe_w1, moe_b1, moe_w2, moe_b2, ln_ffn_g, ln_ffn_b):
    bp, tp, _ = x_prompt.shape
    bs, ts, _ = x_sample.shape
    n_p, n_s = bp * tp, bs * ts
    depth = w_in.shape[0]
    alpha = (2 * depth) ** 0.25
    assert tp % SGU_CHUNK == 0 and SGU_CHUNK % ts == 0

    x = jnp.concatenate([x_prompt.reshape(n_p, D_MODEL), x_sample.reshape(n_s, D_MODEL)], axis=0)
    x = _ln_call(x, ln_in_g, ln_in_b)
    zeros = lambda *s: jnp.zeros(s, F32)
    n_exp = moe_w1.shape[1]
    w1_all = moe_w1.reshape(depth * n_exp, D_MODEL, 2 * D_FF)
    w2_all = moe_w2.reshape(depth * n_exp, D_FF, D_MODEL)
    b1_all = _regroup_bias(moe_b1.reshape(depth * n_exp, 2 * D_FF))
    b2_all = moe_b2.reshape(depth * n_exp, 1, D_MODEL)
    outs_p, outs_s = [], []
    for l in range(depth):
        p_d, p_bg, p_b, p_a, p_c = _proj_call(x, _reorder_w_in(w_in[l], gdn_A_log.shape[1]))

        w_eff, b_eff = _sgu_weights(sgu_w[l], sgu_b[l], ts)
        y_a, v = _sgu_call(p_a, w_eff, b_eff, sgu_ln_g[l], sgu_ln_b[l], n_p)
        v_p = v[:n_p].reshape(bp, tp, W_MIX)[:, ((tp - 1) // SGU_CHUNK) * SGU_CHUNK:]
        v_s = v[n_p:].reshape(bs, ts, W_MIX)

        gdn_w = (gdn_conv_w[l], gdn_A_log[l], gdn_dt_bias[l], gdn_norm_g[l])
        yb_p, gbuf_p, gs_p = _gdn_call(p_b, p_bg, zeros(bp, GDN_CONV - 1, GDN_QKV),
                                       zeros(bp, N_HEADS, HEAD_DIM, HEAD_DIM), *gdn_w, 0, tp)
        yb_s, gbuf_s, gs_s = _gdn_call(p_b, p_bg, state_gdn_conv[l], state_gdn_S[l], *gdn_w, n_p, ts)

        cc_w = (cc_dw_w[l], cc_dw_b[l], cc_ln_g[l], cc_ln_b[l])
        yc_p, cbuf_p = _cc_call(p_c, zeros(bp, CC_WIDTH - 1, W_MIX), *cc_w, 0, tp)
        yc_s, cbuf_s = _cc_call(p_c, state_cc_conv[l], *cc_w, n_p, ts)

        rw_w = (rw_mu[l], rw_w0[l], rw_a0[l], _rwkv_lora_weights(rw_w2[l], rw_a2[l], rw_g2[l]),
                rw_k_k[l], rw_k_a[l], rw_r_k[l].reshape(-1), rw_ln_g[l], rw_ln_b[l])
        yd_p, rsh_p, rs_p = _rwkv_call(p_d, zeros(bp, COLS_D), zeros(bp, N_HEADS, HEAD_DIM, HEAD_DIM), *rw_w, 0, tp)
        yd_s, rsh_s, rs_s = _rwkv_call(p_d, state_rwkv_shift[l], state_rwkv_S[l], *rw_w, n_p, ts)

        x1, route, counts = _outproj_router_call((y_a, yb_p, yc_p, yd_p), (y_a, yb_s, yc_s, yd_s), x,
                                                 w_out[l].astype(BF16), ln_mix_g[l], ln_mix_b[l],
                                                 router_w[l], router_b[l], alpha)
        x = _moe_ffn(x1, route, counts, w1_all, b1_all, w2_all, b2_all, l * n_exp,
                     ln_ffn_g[l], ln_ffn_b[l], alpha)
        outs_p.append((v_p, gbuf_p, gs_p, cbuf_p, rsh_p, rs_p))
        outs_s.append((v_s, gbuf_s, gs_s, cbuf_s, rsh_s, rs_s))

    stack = lambda outs, i: jnp.stack([o[i] for o in outs])
    res = [x[:n_p].reshape(bp, tp, D_MODEL), x[n_p:].reshape(bs, ts, D_MODEL)]
    for i in range(6):
        res += [stack(outs_p, i), stack(outs_s, i)]
    return tuple(res)
```

```python
import functools
import math

import jax
import jax.numpy as jnp
from jax import lax
from jax.experimental import pallas as pl
from jax.experimental.pallas import tpu as pltpu

F32 = jnp.float32
BF16 = jnp.bfloat16

D_MODEL = 1024
HEAD_DIM = 64
W_MIX = 256
N_HEADS = W_MIX // HEAD_DIM
SGU_CHUNK = 128
GDN_CONV = 4
GDN_CHUNK = 32
CC_WIDTH = 31
RWKV_CHUNK = 32
SEQ_GROUP = 8
SEQ_TBLK = 256
RWKV_PASSES = 1
GDN_PASSES = 1
LORA_W, LORA_A, LORA_G = 32, 32, 64
COLS_D = 3 * W_MIX + LORA_W + LORA_A + LORA_G
N_EXPERTS = 32
TOP_K = 4
D_FF = D_MODEL
SWIGLU_ALPHA = 1.702
SWIGLU_LIMIT = 7.0
LN_EPS = 1e-5
RMS_EPS = 1e-6
GN_EPS = 64e-5
LANES = 128
SUBLANES = 8
VMEM_LIMIT = 56 * 1024 * 1024
NEG_BIG = -1e30

P_D = COLS_D
P_BG = LANES
P_B = 4 * W_MIX
P_A = 2 * W_MIX
P_C = 2 * W_MIX
P_TOTAL = P_D + P_BG + P_B + P_A + P_C


def _pick(n, cands):
    for c in cands:
        if n % c == 0:
            return c
    raise ValueError(f"no tile in {cands} divides {n}")


def _params(sem):
    return pltpu.CompilerParams(dimension_semantics=sem, vmem_limit_bytes=VMEM_LIMIT)


def _layer_norm(x, g, b, eps):
    xc = x - jnp.mean(x, -1, keepdims=True)
    var = jnp.mean(xc * xc, -1, keepdims=True)
    return xc * lax.rsqrt(var + eps) * g + b


def _sigmoid(x):
    return 1.0 / (1.0 + jnp.exp(-x))


def _silu(x):
    return x * _sigmoid(x)


def _softplus(x):
    return jnp.maximum(x, 0.0) + jnp.log(1.0 + jnp.exp(-jnp.abs(x)))


def _dot(a, b):
    return jnp.dot(a, b, preferred_element_type=F32)


def _head_ones():
    r = lax.broadcasted_iota(jnp.int32, (W_MIX, W_MIX), 0) // HEAD_DIM
    c = lax.broadcasted_iota(jnp.int32, (W_MIX, W_MIX), 1) // HEAD_DIM
    return (r == c).astype(F32)


def _tri(n, strict):
    r = lax.broadcasted_iota(jnp.int32, (n, n), 0)
    c = lax.broadcasted_iota(jnp.int32, (n, n), 1)
    return (r > c) if strict else (r >= c)


_NN =(((1,), (0,)), ((), ()))
_NT = (((1,), (1,)), ((), ()))
_TN = (((0,), (0,)), ((), ()))


def _split2(x):
    hi = x.astype(BF16)
    return hi, (x - hi.astype(F32)).astype(BF16)


def _split3(x):
    hi = x.astype(BF16)
    r = x - hi.astype(F32)
    mid = r.astype(BF16)
    return hi, mid, (r - mid.astype(F32)).astype(BF16)


def _mm(a, b, dn=_NN, passes=1):
    d = lambda x, y: lax.dot_general(x, y, dn, preferred_element_type=F32)
    if passes == 1:
        return d(a.astype(BF16), b.astype(BF16))
    a_hi, a_lo = _split2(a)
    b_hi, b_lo = _split2(b)
    return d(a_hi, b_hi) + (d(a_lo, b_hi) + d(a_hi, b_lo))


def _mm_exact_rhs(a, sel, dn=_NN, pieces=3):
    d = lambda x: lax.dot_general(x, sel, dn, preferred_element_type=F32)
    if pieces == 2:
        hi, lo = _split2(a)
        return d(hi) + d(lo)
    hi, mid, lo = _split3(a)
    return d(hi) + (d(mid) + d(lo))


def _mm_exact_lhs(sel, b, dn=_NN, pieces=3):
    d = lambda x: lax.dot_general(sel, x, dn, preferred_element_type=F32)
    if pieces == 2:
        hi, lo = _split2(b)
        return d(hi) + d(lo)
    hi, mid, lo = _split3(b)
    return d(hi) + (d(mid) + d(lo))


def _block_neumann_inverse(xs, block, passes):
    n = xs[0].shape[0]
    eye = (lax.broadcasted_iota(jnp.int32, (n, n), 0) == lax.broadcasted_iota(jnp.int32, (n, n), 1)).astype(F32)
    accs = [eye + x for x in xs]
    ps = list(xs)
    k = 2
    while k < block:
        ps = [_mm(p, p, passes=passes) for p in ps]
        accs = [acc + _mm(acc, p, passes=passes) for acc, p in zip(accs, ps)]
        k *= 2
    return accs


def _each(fn, *lists):
    return [fn(*args) for args in zip(*lists)]


def _stack_masked(x, c):
    lane_head = lax.broadcasted_iota(jnp.int32, (c, W_MIX), 1) // HEAD_DIM
    return jnp.concatenate([jnp.where(lane_head == h, x, 0.0) for h in range(N_HEADS)], axis=0)


def _stack_heads(x):
    return jnp.concatenate([x[:, h * HEAD_DIM:(h + 1) * HEAD_DIM] for h in range(N_HEADS)], axis=0)


def _unstack_heads(x, c):
    return jnp.concatenate([x[h * c:(h + 1) * c] for h in range(N_HEADS)], axis=1)


def _seq_block(i, tb, *, j, ns, nt, blk0):
    return (blk0 + (i * ns + j) * nt + tb, 0)


def _block_tri(c, strict):
    n = N_HEADS * c
    r = lax.broadcasted_iota(jnp.int32, (n, n), 0)
    q = lax.broadcasted_iota(jnp.int32, (n, n), 1)
    same = (r // c) == (q // c)
    return same & ((r > q) if strict else (r >= q))


def _ln_kernel(x_ref, g_ref, b_ref, o_ref):
    o_ref[...] = _layer_norm(x_ref[...], g_ref[...], b_ref[...], LN_EPS)


def _ln_call(x, g, b):
    n = x.shape[0]
    tm = _pick(n, (1024, 512, 256, 128))
    return pl.pallas_call(
        _ln_kernel, grid=(n // tm,),
        in_specs=[pl.BlockSpec((tm, D_MODEL), lambda i: (i, 0)),
                  pl.BlockSpec((1, D_MODEL), lambda i: (0, 0)),
                  pl.BlockSpec((1, D_MODEL), lambda i: (0, 0))],
        out_specs=pl.BlockSpec((tm, D_MODEL), lambda i: (i, 0)),
        out_shape=jax.ShapeDtypeStruct((n, D_MODEL), F32),
        compiler_params=_params(("parallel",)), name="ln_in",
    )(x, g.reshape(1, -1), b.reshape(1, -1))


def _proj_kernel(x_ref, w_ref, pd_ref, pbg_ref, pb_ref, pa_ref, pc_ref):
    p = _dot(x_ref[...].astype(BF16), w_ref[...])
    o = 0
    for ref, w in ((pd_ref, P_D), (pbg_ref, P_BG), (pb_ref, P_B), (pa_ref, P_A), (pc_ref, P_C)):
        ref[...] = p[:, o:o + w]
        o += w


def _proj_call(x, w_cat):
    n = x.shape[0]
    tm = _pick(n, (512, 256, 128))
    widths = (P_D, P_BG, P_B, P_A, P_C)
    return pl.pallas_call(
        _proj_kernel, grid=(n // tm,),
        in_specs=[pl.BlockSpec((tm, D_MODEL), lambda i: (i, 0)),
                  pl.BlockSpec((D_MODEL, P_TOTAL), lambda i: (0, 0))],
        out_specs=[pl.BlockSpec((tm, w), lambda i: (i, 0)) for w in widths],
        out_shape=[jax.ShapeDtypeStruct((n, w), F32) for w in widths],
        compiler_params=_params(("parallel",)), name="proj_in",
    )(x, w_cat)


def _reorder_w_in(w_in, h_b):
    cols_a = 2 * W_MIX
    cols_b = 3 * W_MIX + 2 * h_b + W_MIX
    o1, o2 = cols_a, cols_a + cols_b
    o3 = o2 + 2 * W_MIX
    wa, wb, wc, wd = w_in[:, :o1], w_in[:, o1:o2], w_in[:, o2:o3], w_in[:, o3:]
    qkv, bg, z = wb[:, :3 * W_MIX], wb[:, 3 * W_MIX:3 * W_MIX + 2 * h_b], wb[:, 3 * W_MIX + 2 * h_b:]
    bg = jnp.pad(bg, ((0, 0), (0, P_BG - 2 * h_b)))
    return jnp.concatenate([wd, bg, qkv, z, wa, wc], axis=1).astype(BF16)


def _sgu_kernel(p_ref, w_ref, b_ref, g_ref, beta_ref, y_ref, v_ref, *, n_chunks):
    w = w_ref[0]
    bias = b_ref[0]
    lane_head = lax.broadcasted_iota(jnp.int32, (SGU_CHUNK, W_MIX), 1) // HEAD_DIM
    for c in range(n_chunks):
        rows = pl.ds(c * SGU_CHUNK, SGU_CHUNK)
        x = p_ref[rows, :]
        h = 0.5 * x * (1.0 + lax.erf(x * (1.0 / math.sqrt(2.0))))
        u = h[:, :W_MIX]
        v = _layer_norm(h[:, W_MIX:], g_ref[...], beta_ref[...], LN_EPS)
        v_ref[rows, :] = v
        vb = jnp.concatenate([jnp.where(lane_head == hh, v, 0.0) for hh in range(N_HEADS)], axis=0)
        s = _dot(w, vb.astype(BF16)) + bias
        y_ref[rows, :] = u * s


def _sgu_weights(sgu_w, sgu_b, t_s):
    causal = jnp.tril(jnp.ones((SGU_CHUNK, SGU_CHUNK), bool))
    wp = jnp.where(causal, sgu_w, 0.0)
    reps = SGU_CHUNK // t_s
    ws = jnp.stack([jnp.kron(jnp.eye(reps, dtype=F32), wp[h, :t_s, :t_s]) for h in range(N_HEADS)])
    cat = lambda w: jnp.concatenate([w[h] for h in range(N_HEADS)], axis=1)
    w_eff = jnp.stack([cat(wp), cat(ws)]).astype(BF16)
    bp = jnp.repeat(sgu_b.T, HEAD_DIM, axis=1)
    bs = jnp.tile(bp[:t_s], (reps, 1))
    return w_eff, jnp.stack([bp, bs])


def _sgu_call(p_a, w_eff, b_eff, ln_g, ln_b, n_prompt_rows):
    n = p_a.shape[0]
    tb = _pick(math.gcd(n_prompt_rows, n - n_prompt_rows), (1024, 512, 256, 128))
    n_prompt_tiles = n_prompt_rows // tb
    grp = lambda i: jnp.minimum(i // n_prompt_tiles, 1)
    return pl.pallas_call(
        functools.partial(_sgu_kernel, n_chunks=tb // SGU_CHUNK), grid=(n // tb,),
        in_specs=[pl.BlockSpec((tb, P_A), lambda i: (i, 0)),
                  pl.BlockSpec((1, SGU_CHUNK, N_HEADS * SGU_CHUNK), lambda i: (grp(i), 0, 0)),
                  pl.BlockSpec((1, SGU_CHUNK, W_MIX), lambda i: (grp(i), 0, 0)),
                  pl.BlockSpec((1, W_MIX), lambda i: (0, 0)),
                  pl.BlockSpec((1, W_MIX), lambda i: (0, 0))],
        out_specs=[pl.BlockSpec((tb, W_MIX), lambda i: (i, 0))] * 2,
        out_shape=[jax.ShapeDtypeStruct((n, W_MIX), F32)] * 2,
        compiler_params=_params(("parallel",)), name="sgu",
    )(p_a, w_eff, b_eff, ln_g.reshape(1, -1), ln_b.reshape(1, -1))


CC_HDR = 32


def _cc_kernel(p_ref, buf_ref, w_ref, wb_ref, g_ref, b_ref, y_ref, nb_ref, xp_ref, *, sb, t, tt):
    hist = CC_WIDTH - 1
    ones = _head_ones().astype(BF16)
    w = w_ref[...]
    for s in range(sb):
        x = p_ref[pl.ds(s * t, t), :]
        xp_ref[pl.ds(CC_HDR - hist, hist), :] = buf_ref[s]
        xp_ref[pl.ds(CC_HDR, t), :] = x[:, :W_MIX] * _sigmoid(x[:, W_MIX:])
        nb_ref[s] = xp_ref[pl.ds(t + CC_HDR - hist, hist), :]

        def tile(i, carry):
            base = pl.multiple_of(i * tt, SUBLANES)
            win = xp_ref[pl.ds(base, tt + CC_HDR), :]
            acc = jnp.zeros((tt, W_MIX), F32)
            shifted = [win[b:] for b in range(SUBLANES)]
            for j in range(CC_WIDTH):
                o = j + CC_HDR - hist
                a8 = (o // SUBLANES) * SUBLANES
                acc = acc + shifted[o % SUBLANES][a8:a8 + tt] * w[j:j + 1]
            hh = acc + wb_ref[...]
            mean = _mm_exact_rhs(hh, ones, pieces=2) * (1.0 / HEAD_DIM)
            xc = hh - mean
            var = _mm_exact_rhs(xc * xc, ones, pieces=2) * (1.0 / HEAD_DIM)
            yy = xc * lax.rsqrt(var + LN_EPS) * g_ref[...] + b_ref[...]
            y_ref[pl.ds(pl.multiple_of(s * t + base, SUBLANES), tt), :] = _silu(yy)
            return carry

        lax.fori_loop(0, t // tt, tile, 0)


def _cc_call(p_c, buf, w, wb, g, b, row0, t):
    nseq = buf.shape[0]
    sb = 1 if t >= 256 else _pick(nseq, (16, 8, 4, 2, 1))
    tt = min(t, 256)
    rows = sb * t
    blk0 = row0 // rows
    assert row0 % rows == 0 and t % tt == 0
    kern = functools.partial(_cc_kernel, sb=sb, t=t, tt=tt)
    in_specs = [pl.BlockSpec((rows, P_C), lambda i: (blk0 + i, 0)),
                pl.BlockSpec((sb, CC_WIDTH - 1, W_MIX), lambda i: (i, 0, 0)),
                pl.BlockSpec((CC_WIDTH, W_MIX), lambda i: (0, 0)),
                pl.BlockSpec((1, W_MIX), lambda i: (0, 0)),
                pl.BlockSpec((1, W_MIX), lambda i: (0, 0)),
                pl.BlockSpec((1, W_MIX), lambda i: (0, 0))]
    args = [p_c, buf, w, wb.reshape(1, -1), g.reshape(1, -1), b.reshape(1, -1)]
    return pl.pallas_call(
        kern, grid=(nseq // sb,), in_specs=in_specs,
        out_specs=[pl.BlockSpec((rows, W_MIX), lambda i: (i, 0)),
                   pl.BlockSpec((sb, CC_WIDTH - 1, W_MIX), lambda i: (i, 0, 0))],
        out_shape=[jax.ShapeDtypeStruct((nseq * t, W_MIX), F32),
                   jax.ShapeDtypeStruct((nseq, CC_WIDTH - 1, W_MIX), F32)],
        scratch_shapes=[pltpu.VMEM((t + CC_HDR, W_MIX), F32)],
        compiler_params=_params(("arbitrary",)), name="cc",
    )(*args)


GDN_HDR = 8
GDN_QKV = 3 * W_MIX


def _lane_expand(src_lane0):
    r = lax.broadcasted_iota(jnp.int32, (LANES, W_MIX), 0)
    c = lax.broadcasted_iota(jnp.int32, (LANES, W_MIX), 1) // HEAD_DIM
    return (r == c + src_lane0).astype(F32)


def _gdn_kernel(*refs, ns, tblk, c):
    p_refs, bg_refs = refs[:ns], refs[ns:2 * ns]
    buf_ref, s0_ref, cw_ref, alog_ref, dt_ref, ng_ref, y_ref, nb_ref, s_ref, hdr_ref = refs[2 * ns:]
    hist = GDN_CONV - 1
    hc = N_HEADS * c

    @pl.when(pl.program_id(1) == 0)
    def _():
        nb_ref[...] = buf_ref[...]
        s_ref[...] = s0_ref[...]
        hdr_ref[...] = jnp.zeros_like(hdr_ref)

    ones = _head_ones().astype(BF16)
    e_beta = _lane_expand(0).astype(BF16)
    e_g = _lane_expand(N_HEADS).astype(BF16)
    tri_ones = _tri(c, False).astype(BF16)
    strict_bd = _block_tri(c, True)
    incl_bd = _block_tri(c, False)
    eye_hc = (lax.broadcasted_iota(jnp.int32, (hc, hc), 0) == lax.broadcasted_iota(jnp.int32, (hc, hc), 1))
    eye_w = (lax.broadcasted_iota(jnp.int32, (W_MIX, W_MIX), 0) == lax.broadcasted_iota(jnp.int32, (W_MIX, W_MIX), 1))
    first_lane = (lax.broadcasted_iota(jnp.int32, (W_MIX, hc), 0) % HEAD_DIM == 0).astype(BF16)
    ones_hc = jnp.ones((hc, hc), BF16)
    ones_wv = jnp.ones((W_MIX, HEAD_DIM), BF16)
    cw = cw_ref[...]
    neg_a = -jnp.exp(alog_ref[...])
    mm = functools.partial(_mm, passes=GDN_PASSES)

    def chunk(n, carries):
        rows = pl.ds(pl.multiple_of(n * c, SUBLANES), c)
        sm = functools.partial(_stack_masked, c=c)
        cat = jnp.concatenate
        tails = [cr[0] for cr in carries]
        sts = [cr[1] for cr in carries]
        wins = [cat([tails[j], p_refs[j][rows, :GDN_QKV]], axis=0) for j in range(ns)]

        def conv_act(win):
            conv = jnp.zeros((c, GDN_QKV), F32)
            for tap in range(GDN_CONV):
                o = tap + GDN_HDR - hist
                conv = conv + win[o:o + c] * cw[tap:tap + 1]
            return _silu(conv)

        acts = _each(conv_act, wins)
        qs = [a[:, :W_MIX] for a in acts]
        ks = [a[:, W_MIX:2 * W_MIX] for a in acts]
        vs = [a[:, 2 * W_MIX:] for a in acts]
        sqs = _each(lambda q, k: _mm_exact_rhs(cat([q * q, k * k], axis=0), ones, pieces=2), qs, ks)
        qs = _each(lambda q, sq: q * lax.rsqrt(sq[:c] + 1e-6) * (HEAD_DIM ** -0.5), qs, sqs)
        ks = _each(lambda k, sq: k * lax.rsqrt(sq[c:] + 1e-6), ks, sqs)
        bgs = [bg_refs[j][rows, :] for j in range(ns)]
        betas = _each(lambda bg: _mm_exact_rhs(_sigmoid(bg), e_beta, pieces=2), bgs)
        gsums = _each(lambda bg: _mm_exact_lhs(tri_ones, neg_a * _softplus(bg + dt_ref[...])), bgs)
        gcums = _each(lambda gs: _mm_exact_rhs(gs, e_g), gsums)
        kbs = _each(lambda k, b: k * b, ks, betas)
        vbs = _each(lambda v, b: v * b, vs, betas)
        egcs = _each(jnp.exp, gcums)
        glasts = [gc[c - 1:c, :] for gc in gcums]
        kdecs = _each(lambda k, gl, gc: k * jnp.exp(gl - gc), ks, glasts, gcums)

        gcols = _each(lambda gc: _mm_exact_rhs(sm(gc), first_lane), gcums)
        grows = _each(lambda gcol: _mm_exact_lhs(ones_hc, jnp.where(eye_hc, gcol, 0.0)), gcols)
        decays = _each(lambda gcol, grow: jnp.where(incl_bd, jnp.exp(gcol - grow), 0.0), gcols, grows)
        prods = _each(lambda kb, q, k: mm(cat([sm(kb), sm(q)], axis=0), sm(k), _NT), kbs, qs, ks)
        lms = _each(lambda pr, dec: jnp.where(strict_bd, pr[:hc] * dec, 0.0), prods, decays)
        aqks = _each(lambda pr, dec: pr[hc:] * dec, prods, decays)
        tinvs = _block_neumann_inverse([-lm for lm in lms], c, GDN_PASSES)
        uws = _each(lambda ti, vb, kb, egc: mm(ti, cat([_stack_heads(vb), sm(kb * egc)], axis=1)),
                    tinvs, vbs, kbs, egcs)
        wss = _each(lambda uw, q, egc, st: mm(cat([uw[:, HEAD_DIM:], sm(q * egc)], axis=0), st),
                    uws, qs, egcs, sts)
        vnews = _each(lambda uw, ws: uw[:, :HEAD_DIM] - ws[:hc], uws, wss)
        outs = _each(lambda ws, aqk, vn: _unstack_heads(ws[hc:] + mm(aqk, vn), c), wss, aqks, vnews)
        grs = _each(lambda gl: _mm_exact_rhs(jnp.where(eye_w, jnp.exp(gl), 0.0), ones_wv), glasts)
        sts = _each(lambda st, gr, kd, vn: st * gr + mm(sm(kd), vn, _TN), sts, grs, kdecs, vnews)
        outs = _each(lambda o: o * lax.rsqrt(_mm_exact_rhs(o * o, ones, pieces=2) * (1.0 / HEAD_DIM) + RMS_EPS)
                     * ng_ref[...], outs)
        for j in range(ns):
            y_ref[j, rows, :] = outs[j] * _silu(p_refs[j][rows, GDN_QKV:])
        return tuple((wins[j][c:c + GDN_HDR], sts[j]) for j in range(ns))

    init = []
    for j in range(ns):
        hdr_ref[j, pl.ds(GDN_HDR - hist, hist), :] = nb_ref[j]
        init.append((hdr_ref[j], jnp.concatenate([s_ref[j, h] for h in range(N_HEADS)], axis=0)))
    fin = lax.fori_loop(0, tblk // c, chunk, tuple(init))
    for j in range(ns):
        tail, st = fin[j]
        nb_ref[j] = tail[GDN_HDR - hist:]
        for h in range(N_HEADS):
            s_ref[j, h] = st[h * HEAD_DIM:(h + 1) * HEAD_DIM]


def _gdn_call(p_b, p_bg, buf, s0, conv_w, a_log, dt_bias, norm_g, row0, t):
    nseq = buf.shape[0]
    h_b = a_log.shape[0]
    assert h_b == N_HEADS
    c = math.gcd(t, GDN_CHUNK)
    ns = _pick(nseq, (SEQ_GROUP, 2, 1))
    tblk = min(t, SEQ_TBLK)
    nt = t // tblk
    blk0 = row0 // tblk
    assert row0 % tblk == 0 and t % tblk == 0 and tblk % c == 0 and c % SUBLANES == 0
    lane_pad = lambda x: jnp.pad(x.reshape(1, -1), ((0, 0), (h_b, LANES - 2 * h_b)))
    seq_rows = lambda w: [pl.BlockSpec((tblk, w), functools.partial(_seq_block, j=j, ns=ns, nt=nt, blk0=blk0))
                          for j in range(ns)]
    kern = functools.partial(_gdn_kernel, ns=ns, tblk=tblk, c=c)
    in_specs = seq_rows(P_B) + seq_rows(P_BG) + [
        pl.BlockSpec((ns, GDN_CONV - 1, GDN_QKV), lambda i, tb: (i, 0, 0)),
        pl.BlockSpec((ns, N_HEADS, HEAD_DIM, HEAD_DIM), lambda i, tb: (i, 0, 0, 0)),
        pl.BlockSpec((GDN_CONV, GDN_QKV), lambda i, tb: (0, 0)),
        pl.BlockSpec((1, LANES), lambda i, tb: (0, 0)),
        pl.BlockSpec((1, LANES), lambda i, tb: (0, 0)),
        pl.BlockSpec((1, W_MIX), lambda i, tb: (0, 0))]
    args = [p_b] * ns + [p_bg] * ns + [buf, s0, conv_w, lane_pad(a_log), lane_pad(dt_bias),
                                       jnp.tile(norm_g, N_HEADS).reshape(1, -1)]
    y, nb, st = pl.pallas_call(
        kern, grid=(nseq // ns, nt), in_specs=in_specs,
        out_specs=[pl.BlockSpec((ns, tblk, W_MIX), lambda i, tb: (i, tb, 0)),
                   pl.BlockSpec((ns, GDN_CONV - 1, GDN_QKV), lambda i, tb: (i, 0, 0)),
                   pl.BlockSpec((ns, N_HEADS, HEAD_DIM, HEAD_DIM), lambda i, tb: (i, 0, 0, 0))],
        out_shape=[jax.ShapeDtypeStruct((nseq, t, W_MIX), F32),
                   jax.ShapeDtypeStruct((nseq, GDN_CONV - 1, GDN_QKV), F32),
                   jax.ShapeDtypeStruct((nseq, N_HEADS, HEAD_DIM, HEAD_DIM), F32)],
        scratch_shapes=[pltpu.VMEM((ns, GDN_HDR, GDN_QKV), F32)],
        compiler_params=_params(("arbitrary", "arbitrary")), name="gdn",
    )(*args)
    return y.reshape(nseq * t, W_MIX), nb, st


def _rwkv_kernel(*refs, ns, tblk, c):
    p_refs = refs[:ns]
    (sh_ref, s0_ref, mu_ref, w0_ref, a0_ref, lora_ref, kk_ref, ka_ref, rk_ref, g_ref, b_ref,
     y_ref, sho_ref, s_ref) = refs[ns:]
    hc = N_HEADS * c
    ones = _head_ones().astype(BF16)
    tri_ones = _tri(c, False).astype(BF16)
    strict_bd = _block_tri(c, True)
    incl_bd = _block_tri(c, False)
    lane = lax.broadcasted_iota(jnp.int32, (c, LANES), 1)
    row_id = lax.broadcasted_iota(jnp.int32, (c, COLS_D), 0)
    mm = functools.partial(_mm, passes=RWKV_PASSES)

    @pl.when(pl.program_id(1) == 0)
    def _():
        sho_ref[...] = sh_ref[...]
        s_ref[...] = s0_ref[...]

    def chunk(n, carries):
        rows = pl.ds(pl.multiple_of(n * c, SUBLANES), c)
        sm = functools.partial(_stack_masked, c=c)
        cat = jnp.concatenate
        prev_rows = [cr[0] for cr in carries]
        sts = [cr[1] for cr in carries]
        xs = [p_refs[j][rows, :] for j in range(ns)]
        xls = _each(lambda x, pr: x + (jnp.where(row_id == 0, pr, pltpu.roll(x, 1, 0)) - x) * mu_ref[...],
                    xs, prev_rows)
        rs = [xl[:, :W_MIX] for xl in xls]
        ks = [xl[:, W_MIX:2 * W_MIX] for xl in xls]
        vs = [xl[:, 2 * W_MIX:3 * W_MIX] for xl in xls]

        def lora_act(xl):
            lo = xl[:, 3 * W_MIX:]
            return jnp.where(lane < LORA_W, jnp.tanh(lo), jnp.where(lane < LORA_W + LORA_A, lo, _sigmoid(lo)))

        loras = _each(lambda xl: _mm(lora_act(xl), lora_ref[...], passes=3), xls)
        lws = _each(lambda lr: -jnp.exp(-_softplus(-(w0_ref[...] + lr[:, :W_MIX])) - 0.5), loras)
        a_s = _each(lambda lr: _sigmoid(a0_ref[...] + lr[:, W_MIX:2 * W_MIX]), loras)
        gs = [lr[:, 2 * W_MIX:] for lr in loras]
        kkps = _each(lambda k: k * kk_ref[...], ks)
        k2s = _each(lambda k, a: k * (1.0 + (a - 1.0) * ka_ref[...]), ks, a_s)
        sums = _each(lambda kkp, r, k2: _mm_exact_rhs(cat([kkp * kkp, r * k2 * rk_ref[...]], axis=0), ones, pieces=2),
                     kkps, rs, k2s)
        kks = _each(lambda kkp, sm_: kkp * lax.rsqrt(sm_[:c] + 1e-6), kkps, sums)
        bonuses = [sm_[c:] for sm_ in sums]
        cums = _each(lambda lw: _mm_exact_lhs(tri_ones, lw, pieces=2), lws)
        invs = _each(lambda cum: jnp.exp(-cum), cums)
        a_hats = _each(lambda kk, cum, lw: -kk * jnp.exp(cum - lw), kks, cums, lws)
        b_hats = _each(lambda kk, a, inv: kk * a * inv, kks, a_s, invs)
        c_hats = _each(lambda k2, inv: k2 * inv, k2s, invs)
        q_hats = _each(lambda r, cum: r * jnp.exp(cum), rs, cums)
        gam_cs = [jnp.exp(cum[c - 1:c, :]) for cum in cums]

        xaqs = _each(lambda ah, qh: cat([sm(ah), sm(qh)], axis=0), a_hats, q_hats)
        bcss = _each(lambda bh, ch: cat([sm(bh), sm(ch)], axis=0), b_hats, c_hats)
        prods = _each(lambda xaq, bcs: mm(xaq, bcs, _NT), xaqs, bcss)
        a_ms = [jnp.where(strict_bd, pr[:hc, :hc], 0.0) for pr in prods]
        b_ms = [jnp.where(strict_bd, pr[:hc, hc:], 0.0) for pr in prods]
        p_qs = [cat([jnp.where(incl_bd, pr[hc:, :hc], 0.0), jnp.where(incl_bd, pr[hc:, hc:], 0.0)], axis=1)
                for pr in prods]
        tinvs = _block_neumann_inverse(a_ms, c, RWKV_PASSES)
        vss = _each(_stack_heads, vs)
        zos = _each(lambda xaq, st: mm(xaq, st, _NT), xaqs, sts)
        bvs = _each(lambda bm, v_: mm(bm, v_), b_ms, vss)
        zs = _each(lambda ti, zo, bv: mm(ti, zo[:hc] + bv), tinvs, zos, bvs)
        zvs = _each(lambda z, v_: cat([z, v_], axis=0), zs, vss)
        outs = _each(lambda zo, pq, zv: zo[hc:] + mm(pq, zv), zos, p_qs, zvs)
        sts = _each(lambda st, zv, bcs, gc: (st + mm(zv, bcs, _TN)) * gc, sts, zvs, bcss, gam_cs)

        ys = _each(lambda o: _unstack_heads(o, c), outs)
        ycs = _each(lambda y: y - _mm_exact_rhs(y, ones, pieces=2) * (1.0 / HEAD_DIM), ys)
        yns = _each(lambda yc: yc * lax.rsqrt(_mm_exact_rhs(yc * yc, ones, pieces=2) * (1.0 / HEAD_DIM) + GN_EPS)
                    * g_ref[...] + b_ref[...], ycs)
        for j in range(ns):
            y_ref[j, rows, :] = (yns[j] + bonuses[j] * vs[j]) * gs[j]
        return tuple((xs[j][c - 1:c, :], sts[j]) for j in range(ns))

    init = tuple((sho_ref[j], jnp.concatenate([s_ref[j, h] for h in range(N_HEADS)], axis=1)) for j in range(ns))
    fin = lax.fori_loop(0, tblk // c, chunk, init)
    for j in range(ns):
        last_row, st = fin[j]
        sho_ref[j] = last_row
        for h in range(N_HEADS):
            s_ref[j, h] = st[:, h * HEAD_DIM:(h + 1) * HEAD_DIM]


def _rwkv_lora_weights(w2, a2, g2):
    m = jnp.zeros((LANES, 3 * W_MIX), F32)
    m = m.at[:LORA_W, :W_MIX].set(w2)
    m = m.at[LORA_W:LORA_W + LORA_A, W_MIX:2 * W_MIX].set(a2)
    return m.at[LORA_W + LORA_A:, 2 * W_MIX:].set(g2)


def _rwkv_call(p_d, shift, s0, mu, w0, a0, lora_w, k_k, k_a, r_k, ln_g, ln_b, row0, t):
    nseq = shift.shape[0]
    c = math.gcd(t, RWKV_CHUNK)
    ns = _pick(nseq, (SEQ_GROUP, 2, 1))
    tblk = min(t, SEQ_TBLK)
    nt = t // tblk
    blk0 = row0 // tblk
    assert row0 % tblk == 0 and t % tblk == 0 and tblk % c == 0 and c % SUBLANES == 0
    row = lambda x: x.reshape(1, -1)
    vec = lambda w: pl.BlockSpec((1, w), lambda i, tb: (0, 0))
    kern = functools.partial(_rwkv_kernel, ns=ns, tblk=tblk, c=c)
    in_specs = [pl.BlockSpec((tblk, P_D), functools.partial(_seq_block, j=j, ns=ns, nt=nt, blk0=blk0))
                for j in range(ns)] + [
        pl.BlockSpec((ns, 1, COLS_D), lambda i, tb: (i, 0, 0)),
        pl.BlockSpec((ns, N_HEADS, HEAD_DIM, HEAD_DIM), lambda i, tb: (i, 0, 0, 0)),
        vec(COLS_D), vec(W_MIX), vec(W_MIX),
        pl.BlockSpec((LANES, 3 * W_MIX), lambda i, tb: (0, 0)),
        vec(W_MIX), vec(W_MIX), vec(W_MIX), vec(W_MIX), vec(W_MIX)]
    args = [p_d] * ns + [shift.reshape(nseq, 1, COLS_D), s0, row(mu), row(w0), row(a0), lora_w,
                         row(k_k), row(k_a), row(r_k), row(ln_g), row(ln_b)]
    y, sh, st = pl.pallas_call(
        kern, grid=(nseq // ns, nt), in_specs=in_specs,
        out_specs=[pl.BlockSpec((ns, tblk, W_MIX), lambda i, tb: (i, tb, 0)),
                   pl.BlockSpec((ns, 1, COLS_D), lambda i, tb: (i, 0, 0)),
                   pl.BlockSpec((ns, N_HEADS, HEAD_DIM, HEAD_DIM), lambda i, tb: (i, 0, 0, 0))],
        out_shape=[jax.ShapeDtypeStruct((nseq, t, W_MIX), F32),
                   jax.ShapeDtypeStruct((nseq, 1, COLS_D), F32),
                   jax.ShapeDtypeStruct((nseq, N_HEADS, HEAD_DIM, HEAD_DIM), F32)],
        compiler_params=_params(("arbitrary", "arbitrary")), name="rwkv",
    )(*args)
    return y.reshape(nseq * t, W_MIX), sh.reshape(nseq, COLS_D), st


ROUTE_IDX, ROUTE_GATE, ROUTE_RANK = 0, TOP_K, 2 * TOP_K


def _outproj_router_kernel(*refs, tm, alpha, n_prompt_tiles):
    yp_refs, ys_refs = refs[0:4], refs[4:8]
    x_ref, wo_ref, g_ref, b_ref, rw_ref, rb_ref, x1_ref, route_ref, cnt_ref, tri_ref = refs[8:]

    @pl.when(pl.program_id(0) == 0)
    def _():
        cnt_ref[...] = jnp.zeros_like(cnt_ref)
        tri_ref[...] = _tri(tm, True).astype(BF16)

    is_prompt = pl.program_id(0) < n_prompt_tiles
    mix = jnp.zeros((tm, D_MODEL), F32)
    for i in range(4):
        y = jnp.where(is_prompt, yp_refs[i][...], ys_refs[i][...])
        mix = mix + _dot(y.astype(BF16), wo_ref[pl.ds(i * W_MIX, W_MIX), :])
    x1 = _layer_norm(alpha * x_ref[...] + mix, g_ref[...], b_ref[...], LN_EPS)
    x1_ref[...] = x1

    logits = _mm(x1, rw_ref[...], passes=3) + rb_ref[...]
    lane = lax.broadcasted_iota(jnp.int32, (tm, LANES), 1)
    lane_f = lane.astype(F32)
    work = logits
    vals, hots, ids = [], [], []
    for _ in range(TOP_K):
        m = jnp.max(work, axis=-1, keepdims=True)
        idx = jnp.min(jnp.where(work == m, lane_f, float(LANES)), axis=-1, keepdims=True)
        hot = lane_f == idx
        vals.append(m)
        hots.append(hot)
        ids.append(idx)
        work = jnp.where(hot, -jnp.inf, work)
    exps = [jnp.exp(v - vals[0]) for v in vals]
    denom = exps[0] + exps[1] + exps[2] + exps[3]

    any_hot = jnp.zeros((tm, LANES), F32)
    for hot in hots:
        any_hot = any_hot + hot.astype(F32)
    before = _dot(tri_ref[...], any_hot.astype(BF16)) + cnt_ref[...]
    cnt_ref[...] = cnt_ref[...] + jnp.sum(any_hot, axis=0, keepdims=True)

    route = jnp.zeros((tm, LANES), F32)
    for kk in range(TOP_K):
        rank = jnp.sum(jnp.where(hots[kk], before, 0.0), axis=-1, keepdims=True)
        route = jnp.where(lane == ROUTE_IDX + kk, ids[kk], route)
        route = jnp.where(lane == ROUTE_GATE + kk, exps[kk] / denom, route)
        route = jnp.where(lane == ROUTE_RANK + kk, rank, route)
    route_ref[...] = route


def _outproj_router_call(ys_prompt, ys_sample, x, w_out, ln_g, ln_b, router_w, router_b, alpha):
    n = x.shape[0]
    n_p, n_s = ys_prompt[-1].shape[0], ys_sample[-1].shape[0]
    tm = _pick(math.gcd(n_p, n_s), (256, 128))
    npt = n_p // tm
    pmap = lambda y: (lambda i: (i, 0)) if y.shape[0] == n else (lambda i: (jnp.minimum(i, npt - 1), 0))
    smap = lambda y: (lambda i: (i, 0)) if y.shape[0] == n else (lambda i: (jnp.maximum(i - npt, 0), 0))
    row = lambda v: v.reshape(1, -1)
    vec = lambda w: pl.BlockSpec((1, w), lambda i: (0, 0))
    rw = jnp.pad(router_w, ((0, 0), (0, LANES - N_EXPERTS)))
    rb = jnp.pad(router_b, (0, LANES - N_EXPERTS), constant_values=NEG_BIG)
    return pl.pallas_call(
        functools.partial(_outproj_router_kernel, tm=tm, alpha=alpha, n_prompt_tiles=npt), grid=(n // tm,),
        in_specs=[pl.BlockSpec((tm, W_MIX), pmap(y)) for y in ys_prompt] + [
            pl.BlockSpec((tm, W_MIX), smap(y)) for y in ys_sample] + [
            pl.BlockSpec((tm, D_MODEL), lambda i: (i, 0)),
            pl.BlockSpec((D_MODEL, D_MODEL), lambda i: (0, 0)),
            vec(D_MODEL), vec(D_MODEL),
            pl.BlockSpec((D_MODEL, LANES), lambda i: (0, 0)), vec(LANES)],
        out_specs=[pl.BlockSpec((tm, D_MODEL), lambda i: (i, 0)),
                   pl.BlockSpec((tm, LANES), lambda i: (i, 0)),
                   pl.BlockSpec((1, LANES), lambda i: (0, 0))],
        out_shape=[jax.ShapeDtypeStruct((n, D_MODEL), F32),
                   jax.ShapeDtypeStruct((n, LANES), F32),
                   jax.ShapeDtypeStruct((1, LANES), F32)],
        scratch_shapes=[pltpu.VMEM((tm, tm), BF16)],
        compiler_params=_params(("arbitrary",)), name="outproj_router",
    )(*ys_prompt, *ys_sample, x, w_out, row(ln_g), row(ln_b), rw, row(rb))


MOE_TB = 512


def _moe_plan(route, counts, n):
    e_idx = route[:, ROUTE_IDX:ROUTE_IDX + TOP_K].astype(jnp.int32)
    rank = route[:, ROUTE_RANK:ROUTE_RANK + TOP_K].astype(jnp.int32)
    cnt = counts[0, :N_EXPERTS].astype(jnp.int32)
    padded = (cnt + MOE_TB - 1) // MOE_TB * MOE_TB
    pad_end = jnp.cumsum(padded)
    pad_start = pad_end - padded
    dest = (pad_start[e_idx] + rank).reshape(n * TOP_K)
    n_blocks = -(-n * TOP_K // MOE_TB) + N_EXPERTS
    n_used = pad_end[-1] // MOE_TB
    blk = jnp.minimum(jnp.arange(n_blocks), n_used - 1) * MOE_TB
    block_e = jnp.minimum(jnp.sum(pad_end[None, :] <= blk[:, None], axis=1), N_EXPERTS - 1).astype(jnp.int32)
    last_block_row = jnp.where(padded > 0, pad_end - MOE_TB, -1)
    tail = n_used + jnp.arange(N_EXPERTS)
    tail_row = jnp.where(tail < n_blocks, tail * MOE_TB, -1)
    zero_rows = jnp.concatenate([last_block_row, tail_row]).astype(jnp.int32)
    return dest, block_e, n_used.reshape(1).astype(jnp.int32), zero_rows, n_blocks


ROW_TILE = D_MODEL // LANES


def _store_row_tiles(ref, x):
    for s in range(ROW_TILE):
        ref[pl.ds(s, x.shape[0], stride=ROW_TILE), :] = x[:, s * LANES:(s + 1) * LANES]


def _load_row_tiles(ref, rows):
    return jnp.concatenate([ref[pl.ds(s, rows, stride=ROW_TILE), :] for s in range(ROW_TILE)], axis=1)


def _tile_rows(row):
    return pl.ds(pl.multiple_of(row * ROW_TILE, ROW_TILE), ROW_TILE)


def _dispatch_kernel(zrow_ref, dest_ref, x_ref, xs_ref, zbuf_ref, xbuf_ref, zsem, sems, *, tm, n_tiles):
    step = pl.program_id(0)
    slot = step % 2

    def zero_copy(e):
        row = pl.multiple_of(jnp.maximum(zrow_ref[e], 0) * ROW_TILE, MOE_TB * ROW_TILE)
        return pltpu.make_async_copy(zbuf_ref, xs_ref.at[pl.ds(row, MOE_TB * ROW_TILE)], zsem)

    @pl.when(pl.program_id(0) == 0)
    def _():
        zbuf_ref[...] = jnp.zeros_like(zbuf_ref)
        for e in range(2 * N_EXPERTS):
            @pl.when(zrow_ref[e] >= 0)
            def _():
                zero_copy(e).start()
        for e in range(2 * N_EXPERTS):
            @pl.when(zrow_ref[e] >= 0)
            def _():
                zero_copy(e).wait()

    def row_copy(s, t, dst_row):
        return pltpu.make_async_copy(xbuf_ref.at[s, _tile_rows(t)], xs_ref.at[_tile_rows(dst_row)], sems.at[s])

    def issue(t, c):
        for kk in range(TOP_K):
            row_copy(slot, t, dest_ref[t * TOP_K + kk]).start(priority=kk % 2)
        return c

    def drain(s):
        def body(t, c):
            for kk in range(TOP_K):
                row_copy(s, 0, 0).wait()
            return c
        lax.fori_loop(0, tm, body, 0)

    _store_row_tiles(xbuf_ref.at[slot], x_ref[...])
    lax.fori_loop(0, tm, issue, 0)

    @pl.when(step > 0)
    def _():
        drain(1 - slot)

    @pl.when(step == n_tiles - 1)
    def _():
        drain(slot)


def _dispatch_call(x1, dest, last_block_row, n_blocks):
    n = x1.shape[0]
    tm = _pick(n, (256, 128))
    return pl.pallas_call(
        functools.partial(_dispatch_kernel, tm=tm, n_tiles=n // tm),
        grid_spec=pltpu.PrefetchScalarGridSpec(
            num_scalar_prefetch=1, grid=(n // tm,),
            in_specs=[pl.BlockSpec((tm * TOP_K,), lambda i, z: (i,), memory_space=pltpu.SMEM),
                      pl.BlockSpec((tm, D_MODEL), lambda i, z: (i, 0))],
            out_specs=pl.BlockSpec(memory_space=pl.ANY),
            scratch_shapes=[pltpu.VMEM((MOE_TB * ROW_TILE, LANES), F32), pltpu.VMEM((2, tm * ROW_TILE, LANES), F32),
                            pltpu.SemaphoreType.DMA(()), pltpu.SemaphoreType.DMA((2,))]),
        out_shape=jax.ShapeDtypeStruct((n_blocks * MOE_TB * ROW_TILE, LANES), F32),
        compiler_params=_params(("arbitrary",)), name="moe_dispatch",
    )(last_block_row, dest, x1)


PAIR_GROUP = 2 * LANES


def _regroup_bias(b1):
    e, f2 = b1.shape
    return b1.reshape(e, f2 // PAIR_GROUP, LANES, 2).swapaxes(2, 3).reshape(e, 1, f2)


def _expert_kernel(be_ref, nu_ref, x_ref, w1_ref, b1_ref, w2_ref, b2_ref, y_ref, w1s_ref, w2s_ref):
    step = pl.program_id(0)
    live = step < nu_ref[0]
    new_expert = (step == 0) | (be_ref[step] != be_ref[jnp.maximum(step - 1, 0)])

    @pl.when(live & new_expert)
    def _():
        r = lax.broadcasted_iota(jnp.int32, (PAIR_GROUP, PAIR_GROUP), 0)
        c = lax.broadcasted_iota(jnp.int32, (PAIR_GROUP, PAIR_GROUP), 1)
        perm = (r == jnp.where(c < LANES, 2 * c, 2 * (c - LANES) + 1)).astype(BF16)
        for g in range(2 * D_FF // PAIR_GROUP):
            cols = pl.ds(g * PAIR_GROUP, PAIR_GROUP)
            w1s_ref[:, cols] = _dot(w1_ref[0, :, cols].astype(BF16), perm).astype(BF16)
        w2s_ref[...] = w2_ref[0].astype(BF16)

    @pl.when(live)
    def _():
        x = _load_row_tiles(x_ref, MOE_TB)
        h = _dot(x.astype(BF16), w1s_ref[...]) + b1_ref[0]
        acts = []
        for g in range(2 * D_FF // PAIR_GROUP):
            hg = jnp.minimum(h[:, g * PAIR_GROUP:g * PAIR_GROUP + LANES], SWIGLU_LIMIT)
            hl = jnp.clip(h[:, g * PAIR_GROUP + LANES:(g + 1) * PAIR_GROUP], -SWIGLU_LIMIT, SWIGLU_LIMIT)
            acts.append((hg * _sigmoid(SWIGLU_ALPHA * hg) * (hl + 1.0)).astype(BF16))
        _store_row_tiles(y_ref, _dot(jnp.concatenate(acts, axis=1), w2s_ref[...]) + b2_ref[0])

    @pl.when(jnp.logical_not(live))
    def _():
        y_ref[...] = jnp.zeros_like(y_ref)


def _expert_call(xs, block_e, n_used, w1, b1, w2, b2, e0):
    n_blocks = xs.shape[0] // (MOE_TB * ROW_TILE)
    xmap = lambda i, be, nu: (jnp.minimum(i, nu[0] - 1), 0)
    emap3 = lambda i, be, nu: (e0 + be[i], 0, 0)
    return pl.pallas_call(
        _expert_kernel,
        grid_spec=pltpu.PrefetchScalarGridSpec(
            num_scalar_prefetch=2, grid=(n_blocks,),
            in_specs=[pl.BlockSpec((MOE_TB * ROW_TILE, LANES), xmap),
                      pl.BlockSpec((1, D_MODEL, 2 * D_FF), emap3), pl.BlockSpec((1, 1, 2 * D_FF), emap3),
                      pl.BlockSpec((1, D_FF, D_MODEL), emap3), pl.BlockSpec((1, 1, D_MODEL), emap3)],
            out_specs=pl.BlockSpec((MOE_TB * ROW_TILE, LANES), lambda i, be, nu: (i, 0)),
            scratch_shapes=[pltpu.VMEM((D_MODEL, 2 * D_FF), BF16), pltpu.VMEM((D_FF, D_MODEL), BF16)]),
        out_shape=jax.ShapeDtypeStruct(xs.shape, F32),
        compiler_params=_params(("arbitrary",)), name="moe_experts",
    )(block_e, n_used, xs, w1, b1, w2, b2)


def _combine_kernel(dest_ref, dnext_ref, route_ref, x1_ref, ys_ref, g_ref, b_ref, x2_ref, buf_ref, sems,
                    *, tm, alpha, n_tiles):
    step = pl.program_id(0)
    slot = step % 2

    def row_copy(s, t, kk, src_row):
        return pltpu.make_async_copy(ys_ref.at[_tile_rows(src_row)], buf_ref.at[s, kk, _tile_rows(t)], sems.at[s])

    def issue(dref, s):
        def body(t, c):
            for kk in range(TOP_K):
                row_copy(s, t, kk, dref[t * TOP_K + kk]).start(priority=kk % 2)
            return c
        lax.fori_loop(0, tm, body, 0)

    @pl.when(step == 0)
    def _():
        issue(dest_ref, 0)

    @pl.when(step + 1 < n_tiles)
    def _():
        issue(dnext_ref, 1 - slot)

    def drain(t, c):
        for kk in range(TOP_K):
            row_copy(slot, 0, kk, 0).wait()
        return c

    lax.fori_loop(0, tm, drain, 0)
    route = route_ref[...]
    f = jnp.zeros((tm, D_MODEL), F32)
    for kk in range(TOP_K):
        f = f + _load_row_tiles(buf_ref.at[slot, kk], tm) * route[:, ROUTE_GATE + kk:ROUTE_GATE + kk + 1]
    x2_ref[...] = _layer_norm(alpha * x1_ref[...] + f, g_ref[...], b_ref[...], LN_EPS)


def _combine_call(x1, route, dest, ys, ln_g, ln_b, alpha):
    n = x1.shape[0]
    tm = _pick(n, (256, 128))
    n_tiles = n // tm
    vec = pl.BlockSpec((1, D_MODEL), lambda i: (0, 0))
    return pl.pallas_call(
        functools.partial(_combine_kernel, tm=tm, alpha=alpha, n_tiles=n_tiles), grid=(n_tiles,),
        in_specs=[pl.BlockSpec((tm * TOP_K,), lambda i: (i,), memory_space=pltpu.SMEM),
                  pl.BlockSpec((tm * TOP_K,), lambda i: (jnp.minimum(i + 1, n_tiles - 1),), memory_space=pltpu.SMEM),
                  pl.BlockSpec((tm, LANES), lambda i: (i, 0)),
                  pl.BlockSpec((tm, D_MODEL), lambda i: (i, 0)),
                  pl.BlockSpec(memory_space=pl.ANY), vec, vec],
        out_specs=pl.BlockSpec((tm, D_MODEL), lambda i: (i, 0)),
        out_shape=jax.ShapeDtypeStruct((n, D_MODEL), F32),
        scratch_shapes=[pltpu.VMEM((2, TOP_K, tm * ROW_TILE, LANES), F32), pltpu.SemaphoreType.DMA((2,))],
        compiler_params=_params(("arbitrary",)), name="moe_combine",
    )(dest, dest, route, x1, ys, ln_g.reshape(1, -1), ln_b.reshape(1, -1))


def _moe_ffn(x1, route, counts, w1, b1, w2, b2, e0, ln_g, ln_b, alpha):
    n = x1.shape[0]
    dest, block_e, n_used, zero_rows, n_blocks = _moe_plan(route, counts, n)
    xs = _dispatch_call(x1, dest, zero_rows, n_blocks)
    ys = _expert_call(xs, block_e, n_used, w1, b1, w2, b2, e0)
    return _combine_call(x1, route, dest, ys, ln_g, ln_b, alpha)


def kernel(x_prompt, x_sample, state_gdn_conv, state_gdn_S, state_cc_conv, state_rwkv_shift, state_rwkv_S, ln_in_g, ln_in_b, w_in, sgu_ln_g, sgu_ln_b, sgu_w, sgu_b, gdn_conv_w, gdn_A_log, gdn_dt_bias, gdn_norm_g, cc_dw_w, cc_dw_b, cc_ln_g, cc_ln_b, rw_mu, rw_w0, rw_w2, rw_a0, rw_a2, rw_g2, rw_k_k, rw_k_a, rw_r_k, rw_ln_g, rw_ln_b, w_out, ln_mix_g, ln_mix_b, router_w, router_b, moe_w1, moe_b1, moe_w2, moe_b2, ln_ffn_g, ln_ffn_b):
    bp, tp, _ = x_prompt.shape
    bs, ts, _ = x_sample.shape
    n_p, n_s = bp * tp, bs * ts
    depth = w_in.shape[0]
    alpha = (2 * depth) ** 0.25
    assert tp % SGU_CHUNK == 0 and SGU_CHUNK % ts == 0

    x = jnp.concatenate([x_prompt.reshape(n_p, D_MODEL), x_sample.reshape(n_s, D_MODEL)], axis=0)
    x = _ln_call(x, ln_in_g, ln_in_b)
    zeros = lambda *s: jnp.zeros(s, F32)
    n_exp = moe_w1.shape[1]
    w1_all = moe_w1.reshape(depth * n_exp, D_MODEL, 2 * D_FF)
    w2_all = moe_w2.reshape(depth * n_exp, D_FF, D_MODEL)
    b1_all = _regroup_bias(moe_b1.reshape(depth * n_exp, 2 * D_FF))
    b2_all = moe_b2.reshape(depth * n_exp, 1, D_MODEL)
    outs_p, outs_s = [], []
    for l in range(depth):
        p_d, p_bg, p_b, p_a, p_c = _proj_call(x, _reorder_w_in(w_in[l], gdn_A_log.shape[1]))

        w_eff, b_eff = _sgu_weights(sgu_w[l], sgu_b[l], ts)
        y_a, v = _sgu_call(p_a, w_eff, b_eff, sgu_ln_g[l], sgu_ln_b[l], n_p)
        v_p = v[:n_p].reshape(bp, tp, W_MIX)[:, ((tp - 1) // SGU_CHUNK) * SGU_CHUNK:]
        v_s = v[n_p:].reshape(bs, ts, W_MIX)

        gdn_w = (gdn_conv_w[l], gdn_A_log[l], gdn_dt_bias[l], gdn_norm_g[l])
        yb_p, gbuf_p, gs_p = _gdn_call(p_b, p_bg, zeros(bp, GDN_CONV - 1, GDN_QKV),
                                       zeros(bp, N_HEADS, HEAD_DIM, HEAD_DIM), *gdn_w, 0, tp)
        yb_s, gbuf_s, gs_s = _gdn_call(p_b, p_bg, state_gdn_conv[l], state_gdn_S[l], *gdn_w, n_p, ts)

        cc_w = (cc_dw_w[l], cc_dw_b[l], cc_ln_g[l], cc_ln_b[l])
        yc_p, cbuf_p = _cc_call(p_c, zeros(bp, CC_WIDTH - 1, W_MIX), *cc_w, 0, tp)
        yc_s, cbuf_s = _cc_call(p_c, state_cc_conv[l], *cc_w, n_p, ts)

        rw_w = (rw_mu[l], rw_w0[l], rw_a0[l], _rwkv_lora_weights(rw_w2[l], rw_a2[l], rw_g2[l]),
                rw_k_k[l], rw_k_a[l], rw_r_k[l].reshape(-1), rw_ln_g[l], rw_ln_b[l])
        yd_p, rsh_p, rs_p = _rwkv_call(p_d, zeros(bp, COLS_D), zeros(bp, N_HEADS, HEAD_DIM, HEAD_DIM), *rw_w, 0, tp)
        yd_s, rsh_s, rs_s = _rwkv_call(p_d, state_rwkv_shift[l], state_rwkv_S[l], *rw_w, n_p, ts)

        x1, route, counts = _outproj_router_call((y_a, yb_p, yc_p, yd_p), (y_a, yb_s, yc_s, yd_s), x,
                                                 w_out[l].astype(BF16), ln_mix_g[l], ln_mix_b[l],
                                                 router_w[l], router_b[l], alpha)
        x = _moe_ffn(x1, route, counts, w1_all, b1_all, w2_all, b2_all, l * n_exp,
                     ln_ffn_g[l], ln_ffn_b[l], alpha)
        outs_p.append((v_p, gbuf_p, gs_p, cbuf_p, rsh_p, rs_p))
        outs_s.append((v_s, gbuf_s, gs_s, cbuf_s, rsh_s, rs_s))

    stack = lambda outs, i: jnp.stack([o[i] for o in outs])
    res = [x[:n_p].reshape(bp, tp, D_MODEL), x[n_p:].reshape(bs, ts, D_MODEL)]
    for i in range(6):
        res += [stack(outs_p, i), stack(outs_s, i)]
    return tuple(res)
```
